```python
import math, functools
import jax, jax.numpy as jnp
from jax import lax
import numpy as np

D_MODEL = 2048
BATCH = 2
SEQ = 8192
DEPTH = 1
DEC_BATCH = 32
DEC_SEQ = 32
PAST_LEN = 2048

CHUNK = 64
D_MIX = D_MODEL
D_SSM = D_MIX // 2
SSM_GROUP = 16
N_SSM_GROUPS = D_SSM // SSM_GROUP
SSM_STATE = 64
D_ATTN = D_MIX - D_SSM
HEAD_DIM = 128
N_HEADS = D_ATTN // HEAD_DIM
D_IN = D_SSM + 3 * D_ATTN + N_HEADS
D_FF = 4 * D_MODEL
Q_BLOCK = 128
EPS = 1e-6
DT_MIN = 1e-3
DT_MAX = 1e-1
FORGET_BIAS = 3.0
NEG_INF = -1e30

kernel_name = 'hymba_s5_fox_stream_step'


def rmsnorm(x, g):
    x32 = x.astype(jnp.float32)
    y = x32 * lax.rsqrt(jnp.mean(x32 * x32, axis=-1, keepdims=True) + EPS)
    return (y * g.astype(jnp.float32)).astype(x.dtype)


def _cmul(ar, ai, br, bi):
    return ar * br - ai * bi, ar * bi + ai * br


def _scan_combine(e1, e2):
    a1r, a1i, b1r, b1i = e1
    a2r, a2i, b2r, b2i = e2
    ar, ai = _cmul(a2r, a2i, a1r, a1i)
    br, bi = _cmul(a2r, a2i, b1r, b1i)
    return ar, ai, br + b2r, bi + b2i


def s5_mixer(u, h0_re, h0_im, a_re, a_im, log_step, b_re, b_im, c_re, c_im, d, w_glu):
    n, l, _ = u.shape
    f32 = jnp.float32
    u32 = u.astype(f32).reshape(n, l, N_SSM_GROUPS, SSM_GROUP)
    a_re = a_re.astype(f32)
    a_im = a_im.astype(f32)
    step = jnp.exp(log_step.astype(f32))[:, None]
    mag = jnp.exp(a_re * step)
    abar_re = mag * jnp.cos(a_im * step)
    abar_im = mag * jnp.sin(a_im * step)
    den = a_re * a_re + a_im * a_im
    nr = abar_re - 1.0
    ni = abar_im
    fr = (nr * a_re + ni * a_im) / den
    fi = (ni * a_re - nr * a_im) / den
    b_re = b_re.astype(f32)
    b_im = b_im.astype(f32)
    bbar_re = fr[..., None] * b_re - fi[..., None] * b_im
    bbar_im = fr[..., None] * b_im + fi[..., None] * b_re
    bu_re = jnp.einsum('nlgh,gph->nlgp', u32, bbar_re)
    bu_im = jnp.einsum('nlgh,gph->nlgp', u32, bbar_im)
    ih_re, ih_im = _cmul(abar_re, abar_im, h0_re.astype(f32), h0_im.astype(f32))
    bu_re = bu_re.at[:, 0].add(ih_re)
    bu_im = bu_im.at[:, 0].add(ih_im)
    a_br = jnp.broadcast_to(abar_re, bu_re.shape)
    a_bi = jnp.broadcast_to(abar_im, bu_im.shape)
    _, _, h_re, h_im = lax.associative_scan(_scan_combine, (a_br, a_bi, bu_re, bu_im), axis=1)
    y = (jnp.einsum('nlgp,ghp->nlgh', h_re, c_re.astype(f32))
         - jnp.einsum('nlgp,ghp->nlgh', h_im, c_im.astype(f32))
         + d.astype(f32) * u32)
    y = y.reshape(n, l, D_SSM)
    gy = jax.nn.gelu(y)
    out = gy * jax.nn.sigmoid(gy @ w_glu.astype(f32))
    return out.astype(u.dtype), h_re[:, -1], h_im[:, -1]


def _attend(q, cq, q_pos, k, v, ck, k_pos):
    s = jnp.einsum('nqhd,nkhd->nhqk', q, k).astype(jnp.float32) * (HEAD_DIM ** -0.5)
    s = s + jnp.transpose(cq, (0, 2, 1))[..., None] - jnp.transpose(ck, (0, 2, 1))[:, :, None, :]
    mask = k_pos[None, :] <= q_pos[:, None]
    s = jnp.where(mask, s, NEG_INF)
    p = jax.nn.softmax(s, axis=-1)
    return jnp.einsum('nhqk,nkhd->nqhd', p.astype(v.dtype), v)


def fox_prompt(q, k, v, logf):
    n, l = q.shape[0], q.shape[1]
    c = jnp.cumsum(logf.astype(jnp.float32), axis=1)
    pos = jnp.arange(l)

    def block(i):
        start = i * Q_BLOCK
        qb = lax.dynamic_slice_in_dim(q, start, Q_BLOCK, axis=1)
        cb = lax.dynamic_slice_in_dim(c, start, Q_BLOCK, axis=1)
        qp = start + jnp.arange(Q_BLOCK)
        return _attend(qb, cb, qp, k, v, c, pos)

    o = lax.map(block, jnp.arange(l // Q_BLOCK))
    return jnp.transpose(o, (1, 0, 2, 3, 4)).reshape(n, l, D_ATTN)


def fox_sample(q, k, v, logf, cache_k, cache_v, cache_logf):
    n, s = q.shape[0], q.shape[1]
    past = cache_k.shape[1]
    k_all = jnp.concatenate([cache_k.astype(k.dtype), k], axis=1)
    v_all = jnp.concatenate([cache_v.astype(v.dtype), v], axis=1)
    c_all = jnp.cumsum(jnp.concatenate([cache_logf.astype(jnp.float32), logf.astype(jnp.float32)], axis=1), axis=1)
    k_pos = jnp.arange(past + s)
    q_pos = past + jnp.arange(s)
    o = _attend(q, c_all[:, past:], q_pos, k_all, v_all, c_all, k_pos)
    return o.reshape(n, s, D_ATTN)


def trunk_layer(x, h0_re, h0_im, attn_fn, g_norm_mix, w_in, b_f, a_re, a_im, log_step,
                b_re, b_im, c_re, c_im, d, w_glu, g_q, g_k, g_out_ssm, g_out_attn,
                w_out, g_norm_mlp, w_up, w_down):
    lead = x.shape[:2]
    h = rmsnorm(x, g_norm_mix)
    z = h @ w_in
    u = z[..., :D_SSM]
    q = z[..., D_SSM:D_SSM + D_ATTN].reshape(lead + (N_HEADS, HEAD_DIM))
    k = z[..., D_SSM + D_ATTN:D_SSM + 2 * D_ATTN].reshape(lead + (N_HEADS, HEAD_DIM))
    v = z[..., D_SSM + 2 * D_ATTN:D_SSM + 3 * D_ATTN].reshape(lead + (N_HEADS, HEAD_DIM))
    logf = jax.nn.log_sigmoid((z[..., D_SSM + 3 * D_ATTN:] + b_f).astype(jnp.float32))
    q = rmsnorm(q, g_q)
    k = rmsnorm(k, g_k)
    ssm_out, h_re, h_im = s5_mixer(u, h0_re, h0_im, a_re, a_im, log_step, b_re, b_im, c_re, c_im, d, w_glu)
    attn_out = attn_fn(q, k, v, logf)
    mix = jnp.concatenate([rmsnorm(ssm_out, g_out_ssm), rmsnorm(attn_out.astype(x.dtype), g_out_attn)], axis=-1)
    x = x + mix @ w_out
    hm = rmsnorm(x, g_norm_mlp)
    x = x + jnp.square(jax.nn.relu(hm @ w_up)) @ w_down
    return x, k, v, logf, h_re, h_im


def setup_inputs(seed: int = 0) -> dict:
    key = jax.random.key(seed)
    ks = jax.random.split(key, 32)
    f32 = jnp.float32

    def nrm(k, shape, scale):
        return jax.random.normal(k, shape, f32) * scale

    def gain(k, shape):
        return 1.0 + 0.02 * jax.random.normal(k, shape, f32)

    G, P, H = N_SSM_GROUPS, SSM_STATE, SSM_GROUP
    x_prompt = nrm(ks[0], (BATCH, SEQ, D_MODEL), 1.0)
    x_sample = nrm(ks[1], (DEC_BATCH, DEC_SEQ, D_MODEL), 1.0)
    cache_k = nrm(ks[2], (DEPTH, DEC_BATCH, PAST_LEN, N_HEADS, HEAD_DIM), 1.0)
    cache_v = nrm(ks[3], (DEPTH, DEC_BATCH, PAST_LEN, N_HEADS, HEAD_DIM), 1.0)
    cache_logf = jax.nn.log_sigmoid(FORGET_BIAS + jax.random.normal(ks[4], (DEPTH, DEC_BATCH, PAST_LEN, N_HEADS), f32))
    state_ssm_re = nrm(ks[5], (DEPTH, DEC_BATCH, G, P), 0.5)
    state_ssm_im = nrm(ks[6], (DEPTH, DEC_BATCH, G, P), 0.5)
    g_norm_mix = gain(ks[7], (DEPTH, D_MODEL))
    w_in = nrm(ks[8], (DEPTH, D_MODEL, D_IN), D_MODEL ** -0.5)
    b_f = FORGET_BIAS + nrm(ks[9], (DEPTH, N_HEADS), 0.1)
    n_idx = jnp.arange(P, dtype=f32)
    ssm_a_re = -0.5 + nrm(ks[10], (DEPTH, G, P), 0.01)
    ssm_a_im = math.pi * n_idx + nrm(ks[11], (DEPTH, G, P), 0.01)
    ssm_log_step = jax.random.uniform(ks[12], (DEPTH, G), f32, math.log(DT_MIN), math.log(DT_MAX))
    ssm_b_re = nrm(ks[13], (DEPTH, G, P, H), (2 * H) ** -0.5)
    ssm_b_im = nrm(ks[14], (DEPTH, G, P, H), (2 * H) ** -0.5)
    ssm_c_re = nrm(ks[15], (DEPTH, G, H, P), P ** -0.5)
    ssm_c_im = nrm(ks[16], (DEPTH, G, H, P), P ** -0.5)
    ssm_d = nrm(ks[17], (DEPTH, G, H), 1.0)
    w_glu = nrm(ks[18], (DEPTH, D_SSM, D_SSM), D_SSM ** -0.5)
    g_q = gain(ks[19], (DEPTH, HEAD_DIM))
    g_k = gain(ks[20], (DEPTH, HEAD_DIM))
    g_out_ssm = gain(ks[21], (DEPTH, D_SSM))
    g_out_attn = gain(ks[22], (DEPTH, D_ATTN))
    w_out = nrm(ks[23], (DEPTH, D_MIX, D_MODEL), D_MIX ** -0.5)
    g_norm_mlp = gain(ks[24], (DEPTH, D_MODEL))
    w_up = nrm(ks[25], (DEPTH, D_MODEL, D_FF), D_MODEL ** -0.5)
    w_down = nrm(ks[26], (DEPTH, D_FF, D_MODEL), D_FF ** -0.5)
    return {'x_prompt': x_prompt, 'x_sample': x_sample,
            'cache_k': cache_k, 'cache_v': cache_v, 'cache_logf': cache_logf,
            'state_ssm_re': state_ssm_re, 'state_ssm_im': state_ssm_im,
            'g_norm_mix': g_norm_mix, 'w_in': w_in, 'b_f': b_f,
            'ssm_a_re': ssm_a_re, 'ssm_a_im': ssm_a_im, 'ssm_log_step': ssm_log_step,
            'ssm_b_re': ssm_b_re, 'ssm_b_im': ssm_b_im, 'ssm_c_re': ssm_c_re, 'ssm_c_im': ssm_c_im,
            'ssm_d': ssm_d, 'w_glu': w_glu, 'g_q': g_q, 'g_k': g_k,
            'g_out_ssm': g_out_ssm, 'g_out_attn': g_out_attn, 'w_out': w_out,
            'g_norm_mlp': g_norm_mlp, 'w_up': w_up, 'w_down': w_down}


def reference(x_prompt, x_sample, cache_k, cache_v, cache_logf, state_ssm_re, state_ssm_im,
              g_norm_mix, w_in, b_f, ssm_a_re, ssm_a_im, ssm_log_step, ssm_b_re, ssm_b_im,
              ssm_c_re, ssm_c_im, ssm_d, w_glu, g_q, g_k, g_out_ssm, g_out_attn, w_out,
              g_norm_mlp, w_up, w_down):
    y_p = x_prompt
    y_s = x_sample
    kp, vp, lfp, hrp, hip = [], [], [], [], []
    ks_, vs_, lfs, hrs, his = [], [], [], [], []
    for l in range(DEPTH):
        w = (g_norm_mix[l], w_in[l], b_f[l], ssm_a_re[l], ssm_a_im[l], ssm_log_step[l],
             ssm_b_re[l], ssm_b_im[l], ssm_c_re[l], ssm_c_im[l], ssm_d[l], w_glu[l],
             g_q[l], g_k[l], g_out_ssm[l], g_out_attn[l], w_out[l], g_norm_mlp[l], w_up[l], w_down[l])
        h0 = jnp.zeros((y_p.shape[0], N_SSM_GROUPS, SSM_STATE), jnp.float32)
        y_p, k1, v1, lf1, hr1, hi1 = trunk_layer(y_p, h0, h0, fox_prompt, *w)
        samp_attn = functools.partial(fox_sample, cache_k=cache_k[l], cache_v=cache_v[l], cache_logf=cache_logf[l])
        y_s, k2, v2, lf2, hr2, hi2 = trunk_layer(y_s, state_ssm_re[l], state_ssm_im[l], samp_attn, *w)
        kp.append(k1); vp.append(v1); lfp.append(lf1); hrp.append(hr1); hip.append(hi1)
        ks_.append(k2); vs_.append(v2); lfs.append(lf2); hrs.append(hr2); his.append(hi2)
    return (y_p, y_s,
            jnp.stack(kp), jnp.stack(vp), jnp.stack(lfp), jnp.stack(hrp), jnp.stack(hip),
            jnp.stack(ks_), jnp.stack(vs_), jnp.stack(lfs), jnp.stack(hrs), jnp.stack(his))
```

```python
import functools
import math

import jax
import jax.numpy as jnp
from jax import lax
from jax.experimental import pallas as pl
from jax.experimental.pallas import tpu as pltpu

D_MODEL = 2048
D_SSM = 1024
SSM_GROUP = 16
N_SSM_GROUPS = 64
SSM_STATE = 64
D_ATTN = 1024
HEAD_DIM = 128
N_HEADS = 8
D_FF = 8192
EPS = 1e-6
NEG_INF = -1e30
LOG2E = math.log2(math.e)

LANES = 128
SLAB = 256
N_SLABS = D_SSM // SLAB
SLAB_GROUPS = SLAB // SSM_GROUP
SLAB_STATE = SLAB_GROUPS * SSM_STATE
VMEM_LIMIT = 56 * 1024 * 1024

F32 = jnp.float32
BF16 = jnp.bfloat16


def _params(sem, vmem=VMEM_LIMIT):
    return pltpu.CompilerParams(dimension_semantics=sem, vmem_limit_bytes=vmem)


def _rms(x, g):
    return x * lax.rsqrt(jnp.mean(x * x, axis=-1, keepdims=True) + EPS) * g


def _in_proj_kernel(x_ref, g_ref, w_ref, wf_ref, bf_ref, gq_ref, gk_ref,
                    u_ref, qb_ref, k_ref, kb_ref, v_ref, vb_ref, lf_ref):
    hb = _rms(x_ref[...], g_ref[...]).astype(BF16)

    u_ref[...] = jnp.dot(hb, w_ref[:, 0:D_SSM], preferred_element_type=F32)

    q = jnp.dot(hb, w_ref[:, D_SSM:D_SSM + D_ATTN], preferred_element_type=F32)
    gq = gq_ref[...]
    for h in range(N_HEADS):
        sl = slice(h * HEAD_DIM, (h + 1) * HEAD_DIM)
        qb_ref[:, sl] = (_rms(q[:, sl], gq) * (HEAD_DIM ** -0.5 * LOG2E)).astype(BF16)

    k = jnp.dot(hb, w_ref[:, D_SSM + D_ATTN:D_SSM + 2 * D_ATTN], preferred_element_type=F32)
    gk = gk_ref[...]
    for h in range(N_HEADS):
        sl = slice(h * HEAD_DIM, (h + 1) * HEAD_DIM)
        kn = _rms(k[:, sl], gk)
        k_ref[:, sl] = kn
        kb_ref[:, sl] = kn.astype(BF16)

    v = jnp.dot(hb, w_ref[:, D_SSM + 2 * D_ATTN:D_SSM + 3 * D_ATTN], preferred_element_type=F32)
    v_ref[...] = v
    vb_ref[...] = v.astype(BF16)

    zf = jnp.dot(hb, wf_ref[...], preferred_element_type=F32) + bf_ref[...]
    lf_ref[...] = jnp.minimum(zf, 0.0) - jnp.log1p(jnp.exp(-jnp.abs(zf)))


def _in_proj(x, g, w_main, w_f, b_f, g_q, g_k, *, tt):
    t = x.shape[0]
    assert t % tt == 0
    row = lambda i: (i, 0)
    const = lambda i: (0, 0)
    once = pl.Buffered(1)
    wide = lambda dt: jax.ShapeDtypeStruct((t, D_ATTN), dt)
    return pl.pallas_call(
        _in_proj_kernel,
        grid=(t // tt,),
        in_specs=[
            pl.BlockSpec((tt, D_MODEL), row),
            pl.BlockSpec((1, D_MODEL), const),
            pl.BlockSpec((D_MODEL, D_SSM + 3 * D_ATTN), const, pipeline_mode=once),
            pl.BlockSpec((D_MODEL, LANES), const, pipeline_mode=once),
            pl.BlockSpec((1, LANES), const),
            pl.BlockSpec((1, HEAD_DIM), const),
            pl.BlockSpec((1, HEAD_DIM), const),
        ],
        out_specs=[
            pl.BlockSpec((tt, D_SSM), row),
            pl.BlockSpec((tt, D_ATTN), row),
            pl.BlockSpec((tt, D_ATTN), row),
            pl.BlockSpec((tt, D_ATTN), row),
            pl.BlockSpec((tt, D_ATTN), row),
            pl.BlockSpec((tt, D_ATTN), row),
            pl.BlockSpec((tt, LANES), row),
        ],
        out_shape=[wide(F32), wide(BF16), wide(F32), wide(BF16), wide(F32), wide(BF16),
                   jax.ShapeDtypeStruct((t, LANES), F32)],
        compiler_params=_params(("arbitrary",)),
        name="in_proj",
    )(x, g, w_main, w_f, b_f, g_q, g_k)


def _cumsum_kernel(x_ref, o_ref):
    rows, length = x_ref.shape
    lane = lax.broadcasted_iota(jnp.int32, (rows, LANES), 1)
    carry = jnp.zeros((rows, 1), F32)
    for b in range(length // LANES):
        sl = slice(b * LANES, (b + 1) * LANES)
        x = x_ref[:, sl]
        d = 1
        while d < LANES:
            x = x + jnp.where(lane >= d, pltpu.roll(x, d, axis=1), 0.0)
            d *= 2
        x = x + carry
        o_ref[:, sl] = x
        carry = x[:, LANES - 1:LANES]


def _cumsum_lanes(x):
    rows, length = x.shape
    assert length % LANES == 0 and rows % 8 == 0
    return pl.pallas_call(
        _cumsum_kernel,
        out_shape=jax.ShapeDtypeStruct((rows, length), F32),
        name="cumsum",
    )(x)


def _s5_tables(a_re, a_im, log_step, b_re, b_im, c_re, c_im, d, tc):
    step = jnp.exp(log_step)[:, None]
    mag = jnp.exp(a_re * step)
    abar_re = mag * jnp.cos(a_im * step)
    abar_im = mag * jnp.sin(a_im * step)
    den = a_re * a_re + a_im * a_im
    nr = abar_re - 1.0
    ni = abar_im
    fr = (nr * a_re + ni * a_im) / den
    fi = (ni * a_re - nr * a_im) / den
    bbar_re = fr[..., None] * b_re - fi[..., None] * b_im
    bbar_im = fr[..., None] * b_im + fi[..., None] * b_re
    eye = jnp.eye(SLAB_GROUPS, dtype=F32)

    def in_blockdiag(b):
        b = b.reshape(N_SLABS, SLAB_GROUPS, SSM_STATE, SSM_GROUP)
        return jnp.einsum('sgph,gk->sghkp', b, eye).reshape(N_SLABS, SLAB, SLAB_STATE)

    def out_blockdiag(c):
        c = c.reshape(N_SLABS, SLAB_GROUPS, SSM_GROUP, SSM_STATE)
        return jnp.einsum('sghp,gk->sgpkh', c, eye).reshape(N_SLABS, SLAB_STATE, SLAB)

    bb = jnp.concatenate([in_blockdiag(bbar_re), in_blockdiag(bbar_im)], axis=-1).astype(BF16)
    cc = jnp.concatenate([out_blockdiag(c_re), out_blockdiag(-c_im)], axis=1).astype(BF16)
    n = jnp.arange(1, tc + 1, dtype=F32)[:, None, None]
    pmag = jnp.exp(a_re[None] * step[None] * n)
    ang = a_im[None] * step[None] * n
    p_re = (pmag * jnp.cos(ang)).reshape(tc, N_SLABS, SLAB_STATE).transpose(1, 0, 2)
    p_im = (pmag * jnp.sin(ang)).reshape(tc, N_SLABS, SLAB_STATE).transpose(1, 0, 2)
    pw = jnp.concatenate([p_re, p_im], axis=-1)
    dd = d.reshape(N_SLABS, 1, SLAB)
    return bb, cc, pw, dd


def _s5_kernel(u_ref, bb_ref, cc_ref, pw_ref, d_ref, h0_ref, y_ref, hfin_ref,
               st_s, g_s, carry_s, *, tc, nc, tiles_per_seq, carry_mode):
    i = pl.program_id(1)
    u = u_ref[...]
    nblk = SLAB_STATE // LANES
    bu = jnp.dot(u.astype(BF16), bb_ref[...], preferred_element_type=F32)
    for b in range(2 * nblk):
        st_s[b] = bu[:, b * LANES:(b + 1) * LANES]
    lanes = lambda b: slice(b * LANES, (b + 1) * LANES)

    if carry_mode:
        @pl.when(i % tiles_per_seq == 0)
        def _():
            carry_s[...] = jnp.zeros_like(carry_s)
    else:
        g_s[...] = h0_ref[...]

    for b in range(nblk):
        re, im = lanes(b), lanes(nblk + b)
        ar = pw_ref[0:1, re]
        ai = pw_ref[0:1, im]
        hr = jnp.zeros((nc, LANES), F32)
        hi = jnp.zeros((nc, LANES), F32)
        for t in range(tc):
            rows = pl.ds(t, nc, stride=tc)
            hr, hi = (ar * hr - ai * hi + st_s[b, rows, :],
                      ar * hi + ai * hr + st_s[nblk + b, rows, :])
            st_s[b, rows, :] = hr
            st_s[nblk + b, rows, :] = hi

        ar = pw_ref[tc - 1:tc, re]
        ai = pw_ref[tc - 1:tc, im]
        if carry_mode:
            gr = carry_s[0:1, re]
            gi = carry_s[0:1, im]
            for c in range(nc):
                g_s[c:c + 1, re] = gr
                g_s[c:c + 1, im] = gi
                gr, gi = (ar * gr - ai * gi + hr[c:c + 1, :],
                          ar * gi + ai * gr + hi[c:c + 1, :])
            carry_s[0:1, re] = gr
            carry_s[0:1, im] = gi
            hfin_ref[0:1, re] = gr
            hfin_ref[0:1, im] = gi
            gr = g_s[:, re]
            gi = g_s[:, im]
        else:
            gr = g_s[:, re]
            gi = g_s[:, im]
            hfin_ref[:, re] = ar * gr - ai * gi + hr
            hfin_ref[:, im] = ar * gi + ai * gr + hi

        for t in range(tc):
            rows = pl.ds(t, nc, stride=tc)
            pr = pw_ref[t:t + 1, re]
            pi = pw_ref[t:t + 1, im]
            st_s[b, rows, :] = st_s[b, rows, :] + (pr * gr - pi * gi)
            st_s[nblk + b, rows, :] = st_s[nblk + b, rows, :] + (pr * gi + pi * gr)

    h = jnp.concatenate([st_s[b].astype(BF16) for b in range(2 * nblk)], axis=1)
    y_ref[...] = jnp.dot(h, cc_ref[...], preferred_element_type=F32) + d_ref[...] * u


def _s5(u, tables, h0, *, tt, tc, seq_len, carry_mode):
    bb, cc, pw, dd = tables
    t = u.shape[0]
    nc = tt // tc
    n_tiles = t // tt
    assert t % tt == 0 and tt % tc == 0
    if carry_mode:
        assert seq_len % tt == 0
        tiles_per_seq = seq_len // tt
        n_seq = t // seq_len
        hfin_shape = (N_SLABS, n_seq, 1, 2 * SLAB_STATE)
        hfin_spec = pl.BlockSpec((None, None, 1, 2 * SLAB_STATE),
                                 lambda s, i: (s, i // tiles_per_seq, 0, 0))
        h0 = jnp.zeros((N_SLABS, 8, 2 * SLAB_STATE), F32)
        h0_spec = pl.BlockSpec((None, 8, 2 * SLAB_STATE), lambda s, i: (s, 0, 0))
    else:
        assert seq_len == tc
        tiles_per_seq = 1
        hfin_shape = (N_SLABS, n_tiles, nc, 2 * SLAB_STATE)
        hfin_spec = pl.BlockSpec((None, None, nc, 2 * SLAB_STATE), lambda s, i: (s, i, 0, 0))
        h0_spec = pl.BlockSpec((None, nc, 2 * SLAB_STATE), lambda s, i: (s, i, 0))
    kern = functools.partial(_s5_kernel, tc=tc, nc=nc, tiles_per_seq=tiles_per_seq,
                             carry_mode=carry_mode)
    y, hfin = pl.pallas_call(
        kern,
        grid=(N_SLABS, n_tiles),
        in_specs=[
            pl.BlockSpec((tt, SLAB), lambda s, i: (i, s)),
            pl.BlockSpec((None, SLAB, 2 * SLAB_STATE), lambda s, i: (s, 0, 0)),
            pl.BlockSpec((None, 2 * SLAB_STATE, SLAB), lambda s, i: (s, 0, 0)),
            pl.BlockSpec((None, tc, 2 * SLAB_STATE), lambda s, i: (s, 0, 0)),
            pl.BlockSpec((None, 1, SLAB), lambda s, i: (s, 0, 0)),
            h0_spec,
        ],
        out_specs=[pl.BlockSpec((tt, SLAB), lambda s, i: (i, s)), hfin_spec],
        out_shape=[jax.ShapeDtypeStruct((t, D_SSM), F32),
                   jax.ShapeDtypeStruct(hfin_shape, F32)],
        scratch_shapes=[pltpu.VMEM((2 * SLAB_STATE // LANES, tt, LANES), F32),
                        pltpu.VMEM((nc, 2 * SLAB_STATE), F32),
                        pltpu.VMEM((8, 2 * SLAB_STATE), F32)],
        compiler_params=_params(("arbitrary", "arbitrary")),
        name="s5",
    )(u, bb, cc, pw, dd, h0)
    return y, hfin.reshape(N_SLABS, -1, 2 * SLAB_STATE)


def _state_to_gp(hfin):
    n = hfin.shape[1]
    h = hfin.transpose(1, 0, 2)
    re = h[..., :SLAB_STATE].reshape(n, N_SSM_GROUPS, SSM_STATE)
    im = h[..., SLAB_STATE:].reshape(n, N_SSM_GROUPS, SSM_STATE)
    return re, im


def _state_from_gp(re, im):
    n = re.shape[0]
    h = jnp.concatenate([re.reshape(n, N_SLABS, SLAB_STATE), im.reshape(n, N_SLABS, SLAB_STATE)],
                        axis=-1)
    return h.transpose(1, 0, 2)


def _col_from_row(row):
    n = row.shape[1]
    r = lax.broadcasted_iota(jnp.int32, (n, n), 0)
    c = lax.broadcasted_iota(jnp.int32, (n, n), 1)
    return jnp.sum(jnp.where(r == c, jnp.broadcast_to(row, (n, n)), 0.0), axis=1, keepdims=True)


def _attn_prompt_kernel(q_ref, k_ref, v_ref, cq_ref, ck_ref, o_ref, m_s, l_s, acc_s, *, bq):
    i = pl.program_id(1)
    q = q_ref[...]
    cq = _col_from_row(cq_ref[...]) * LOG2E
    m_s[...] = jnp.full_like(m_s, NEG_INF)
    l_s[...] = jnp.zeros_like(l_s)
    acc_s[...] = jnp.zeros_like(acc_s)

    def step(j, masked):
        start = pl.multiple_of(j * bq, bq)
        k = k_ref[pl.ds(start, bq), :]
        v = v_ref[pl.ds(start, bq), :]
        ck = ck_ref[:, pl.ds(start, bq)] * LOG2E
        s = lax.dot_general(q, k, (((1,), (1,)), ((), ())), preferred_element_type=F32)
        s = s + cq - ck
        if masked:
            r = lax.broadcasted_iota(jnp.int32, (bq, bq), 0)
            c = lax.broadcasted_iota(jnp.int32, (bq, bq), 1)
            s = jnp.where(c <= r, s, NEG_INF)
        m_old = m_s[...]
        m_new = jnp.maximum(m_old, jnp.max(s, axis=1, keepdims=True))
        alpha = jnp.exp2(m_old - m_new)
        p = jnp.exp2(s - m_new)
        l_s[...] = alpha * l_s[...] + jnp.sum(p, axis=1, keepdims=True)
        acc_s[...] = alpha * acc_s[...] + jnp.dot(p.astype(BF16), v, preferred_element_type=F32)
        m_s[...] = m_new

    def body(j, carry):
        step(j, False)
        return carry

    lax.fori_loop(0, i, body, 0)
    step(i, True)
    o_ref[...] = acc_s[...] / l_s[...]


def _attn_prompt(qb, kb, vb, c_rows, *, n_seq, seq_len, bq):
    nq = seq_len // bq
    assert seq_len % bq == 0
    kern = functools.partial(_attn_prompt_kernel, bq=bq)
    seq_blocks = seq_len // bq
    return pl.pallas_call(
        kern,
        grid=(n_seq * N_HEADS, nq),
        in_specs=[
            pl.BlockSpec((bq, HEAD_DIM), lambda g, i: ((g // N_HEADS) * seq_blocks + i, g % N_HEADS)),
            pl.BlockSpec((seq_len, HEAD_DIM), lambda g, i: (g // N_HEADS, g % N_HEADS)),
            pl.BlockSpec((seq_len, HEAD_DIM), lambda g, i: (g // N_HEADS, g % N_HEADS)),
            pl.BlockSpec((None, 1, bq), lambda g, i: (g, 0, i)),
            pl.BlockSpec((None, 1, seq_len), lambda g, i: (g, 0, 0)),
        ],
        out_specs=pl.BlockSpec((bq, HEAD_DIM),
                               lambda g, i: ((g // N_HEADS) * seq_blocks + i, g % N_HEADS)),
        out_shape=jax.ShapeDtypeStruct((n_seq * seq_len, D_ATTN), F32),
        scratch_shapes=[pltpu.VMEM((bq, 1), F32), pltpu.VMEM((bq, 1), F32),
                        pltpu.VMEM((bq, HEAD_DIM), F32)],
        compiler_params=_params(("arbitrary", "arbitrary")),
        name="attn_prompt",
    )(qb, kb, vb, c_rows, c_rows)


def _attn_sample_kernel(q_ref, kn_ref, vn_ref, ck_ref, cv_ref, c_ref, o_ref, *, past, s_new):
    r = lax.broadcasted_iota(jnp.int32, (s_new, s_new), 0)
    col = lax.broadcasted_iota(jnp.int32, (s_new, s_new), 1)
    causal = col <= r
    nt = (((1,), (1,)), ((), ()))
    for h in range(N_HEADS):
        sl = slice(h * HEAD_DIM, (h + 1) * HEAD_DIM)
        q = q_ref[:, sl]
        c_past = c_ref[h:h + 1, 0:past] * LOG2E
        c_new = c_ref[h:h + 1, past:past + s_new] * LOG2E
        cq = _col_from_row(c_new)
        s_p = lax.dot_general(q, ck_ref[:, sl].astype(BF16), nt, preferred_element_type=F32)
        s_p = s_p + cq - c_past
        s_n = lax.dot_general(q, kn_ref[:, sl], nt, preferred_element_type=F32)
        s_n = jnp.where(causal, s_n + cq - c_new, NEG_INF)
        m = jnp.maximum(jnp.max(s_p, axis=1, keepdims=True), jnp.max(s_n, axis=1, keepdims=True))
        p_p = jnp.exp2(s_p - m)
        p_n = jnp.exp2(s_n - m)
        l = jnp.sum(p_p, axis=1, keepdims=True) + jnp.sum(p_n, axis=1, keepdims=True)
        acc = (jnp.dot(p_p.astype(BF16), cv_ref[:, sl].astype(BF16), preferred_element_type=F32)
               + jnp.dot(p_n.astype(BF16), vn_ref[:, sl], preferred_element_type=F32))
        o_ref[:, sl] = acc / l


def _attn_sample(qb, kb, vb, cache_k, cache_v, c_all, *, n_seq, s_new, past):
    kern = functools.partial(_attn_sample_kernel, past=past, s_new=s_new)
    new = pl.BlockSpec((s_new, D_ATTN), lambda b: (b, 0))
    cache = pl.BlockSpec((None, past, D_ATTN), lambda b: (b, 0, 0))
    return pl.pallas_call(
        kern,
        grid=(n_seq,),
        in_specs=[new, new, new, cache, cache,
                  pl.BlockSpec((N_HEADS, c_all.shape[1]), lambda b: (b, 0))],
        out_specs=new,
        out_shape=jax.ShapeDtypeStruct((n_seq * s_new, D_ATTN), F32),
        compiler_params=_params(("arbitrary",)),
        name="attn_sample",
    )(qb, kb, vb, cache_k, cache_v, c_all)


def _gelu_tanh(x):
    return 0.5 * x * (1.0 + jnp.tanh(math.sqrt(2.0 / math.pi) * (x + 0.044715 * (x * x * x))))


def _mix_kernel(x_ref, y_ref, a_ref, wglu_ref, gs_ref, ga_ref, wout_ref, gm_ref, x1_ref, hm_ref):
    gy = _gelu_tanh(y_ref[...])
    gate = jax.nn.sigmoid(jnp.dot(gy.astype(BF16), wglu_ref[...], preferred_element_type=F32))
    ssm = _rms(gy * gate, gs_ref[...]).astype(BF16)
    att = _rms(a_ref[...], ga_ref[...]).astype(BF16)
    x1 = (x_ref[...]
          + jnp.dot(ssm, wout_ref[0:D_SSM, :], preferred_element_type=F32)
          + jnp.dot(att, wout_ref[D_SSM:D_SSM + D_ATTN, :], preferred_element_type=F32))
    x1_ref[...] = x1
    hm_ref[...] = _rms(x1, gm_ref[...]).astype(BF16)


def _mix(x, y, attn, w_glu, g_ssm, g_attn, w_out, g_mlp, *, tt):
    t = x.shape[0]
    assert t % tt == 0
    row = lambda i: (i, 0)
    const = lambda i: (0, 0)
    once = pl.Buffered(1)
    return pl.pallas_call(
        _mix_kernel,
        grid=(t // tt,),
        in_specs=[
            pl.BlockSpec((tt, D_MODEL), row),
            pl.BlockSpec((tt, D_SSM), row),
            pl.BlockSpec((tt, D_ATTN), row),
            pl.BlockSpec((D_SSM, D_SSM), const, pipeline_mode=once),
            pl.BlockSpec((1, D_SSM), const),
            pl.BlockSpec((1, D_ATTN), const),
            pl.BlockSpec((D_SSM + D_ATTN, D_MODEL), const, pipeline_mode=once),
            pl.BlockSpec((1, D_MODEL), const),
        ],
        out_specs=[pl.BlockSpec((tt, D_MODEL), row), pl.BlockSpec((tt, D_MODEL), row)],
        out_shape=[jax.ShapeDtypeStruct((t, D_MODEL), F32), jax.ShapeDtypeStruct((t, D_MODEL), BF16)],
        compiler_params=_params(("arbitrary",)),
        name="mix",
    )(x, y, attn, w_glu, g_ssm, g_attn, w_out, g_mlp)


def _mlp_kernel(x1_ref, hm_ref, wup_ref, wdn_ref, o_ref):
    j = pl.program_id(1)

    @pl.when(j == 0)
    def _():
        o_ref[...] = x1_ref[...]

    a = jnp.maximum(jnp.dot(hm_ref[...], wup_ref[...], preferred_element_type=F32), 0.0)
    o_ref[...] += jnp.dot((a * a).astype(BF16), wdn_ref[...], preferred_element_type=F32)


def _mlp(x1, hm, w_up, w_down, *, tt, tf):
    t = x1.shape[0]
    assert t % tt == 0 and D_FF % tf == 0
    return pl.pallas_call(
        _mlp_kernel,
        grid=(t // tt, D_FF // tf),
        in_specs=[
            pl.BlockSpec((tt, D_MODEL), lambda i, j: (i, 0)),
            pl.BlockSpec((tt, D_MODEL), lambda i, j: (i, 0)),
            pl.BlockSpec((D_MODEL, tf), lambda i, j: (0, j)),
            pl.BlockSpec((tf, D_MODEL), lambda i, j: (j, 0)),
        ],
        out_specs=pl.BlockSpec((tt, D_MODEL), lambda i, j: (i, 0)),
        out_shape=jax.ShapeDtypeStruct((t, D_MODEL), F32),
        compiler_params=_params(("arbitrary", "arbitrary")),
        name="mlp",
    )(x1, hm, w_up, w_down)


def _layer(x, weights, *, n_seq, seq_len, cache=None, h0=None, tiles):
    (g_mix, w_main, w_f, b_f, g_q, g_k, s5_raw, w_glu, g_ssm, g_attn, w_out, g_mlp,
     w_up, w_down) = weights
    t = n_seq * seq_len
    x2 = x.reshape(t, D_MODEL)
    u, qb, k, kb, v, vb, lf = _in_proj(x2, g_mix, w_main, w_f, b_f, g_q, g_k, tt=tiles["proj"])
    logf = lf[:, :N_HEADS]
    lf_rows = logf.reshape(n_seq, seq_len, N_HEADS).transpose(0, 2, 1).reshape(n_seq * N_HEADS, seq_len)

    if cache is None:
        tables = _s5_tables(*s5_raw, tc=16)
        y, hfin = _s5(u, tables, None, tt=tiles["s5"], tc=16, seq_len=seq_len, carry_mode=True)
        c_rows = _cumsum_lanes(lf_rows)
        attn = _attn_prompt(qb, kb, vb, c_rows.reshape(n_seq * N_HEADS, 1, seq_len),
                            n_seq=n_seq, seq_len=seq_len, bq=tiles["bq"])
    else:
        cache_k, cache_v, cache_logf = cache
        past = cache_k.shape[1]
        tables = _s5_tables(*s5_raw, tc=seq_len)
        y, hfin = _s5(u, tables, h0, tt=tiles["s5"], tc=seq_len, seq_len=seq_len, carry_mode=False)
        past_rows = cache_logf.transpose(0, 2, 1).reshape(n_seq * N_HEADS, past)
        total = past + seq_len
        padded = -(-total // LANES) * LANES
        lf_all = jnp.concatenate(
            [past_rows, lf_rows, jnp.zeros((n_seq * N_HEADS, padded - total), F32)], axis=1)
        c_all = _cumsum_lanes(lf_all)
        attn = _attn_sample(qb, kb, vb, cache_k.reshape(n_seq, past, D_ATTN),
                            cache_v.reshape(n_seq, past, D_ATTN), c_all,
                            n_seq=n_seq, s_new=seq_len, past=past)

    x1, hm = _mix(x2, y, attn, w_glu, g_ssm, g_attn, w_out, g_mlp, tt=tiles["mix"])
    out = _mlp(x1, hm, w_up, w_down, tt=tiles["mlp"], tf=tiles["tf"])
    h_re, h_im = _state_to_gp(hfin)
    return (out.reshape(n_seq, seq_len, D_MODEL),
            k.reshape(n_seq, seq_len, N_HEADS, HEAD_DIM),
            v.reshape(n_seq, seq_len, N_HEADS, HEAD_DIM),
            logf.reshape(n_seq, seq_len, N_HEADS), h_re, h_im)


def _prep_weights(l, g_norm_mix, w_in, b_f, ssm_a_re, ssm_a_im, ssm_log_step, ssm_b_re, ssm_b_im,
                  ssm_c_re, ssm_c_im, ssm_d, w_glu, g_q, g_k, g_out_ssm, g_out_attn, w_out,
                  g_norm_mlp, w_up, w_down):
    n_main = D_SSM + 3 * D_ATTN
    w = w_in[l]
    w_main = w[:, :n_main].astype(BF16)
    w_f = jnp.pad(w[:, n_main:], ((0, 0), (0, LANES - N_HEADS))).astype(BF16)
    b = jnp.pad(b_f[l], (0, LANES - N_HEADS)).reshape(1, LANES)
    s5_raw = (ssm_a_re[l], ssm_a_im[l], ssm_log_step[l], ssm_b_re[l], ssm_b_im[l],
              ssm_c_re[l], ssm_c_im[l], ssm_d[l])
    return (g_norm_mix[l].reshape(1, D_MODEL), w_main, w_f, b,
            g_q[l].reshape(1, HEAD_DIM), g_k[l].reshape(1, HEAD_DIM), s5_raw,
            w_glu[l].astype(BF16), g_out_ssm[l].reshape(1, D_SSM), g_out_attn[l].reshape(1, D_ATTN),
            w_out[l].astype(BF16), g_norm_mlp[l].reshape(1, D_MODEL),
            w_up[l].astype(BF16), w_down[l].astype(BF16))


PROMPT_TILES = dict(proj=512, s5=512, bq=512, mix=256, mlp=512, tf=1024)
SAMPLE_TILES = dict(proj=512, s5=512, mix=256, mlp=512, tf=1024)


def kernel(x_prompt, x_sample, cache_k, cache_v, cache_logf, state_ssm_re, state_ssm_im,
           g_norm_mix, w_in, b_f, ssm_a_re, ssm_a_im, ssm_log_step, ssm_b_re, ssm_b_im,
           ssm_c_re, ssm_c_im, ssm_d, w_glu, g_q, g_k, g_out_ssm, g_out_attn, w_out,
           g_norm_mlp, w_up, w_down):
    depth = w_in.shape[0]
    y_p, y_s = x_prompt, x_sample
    outs_p, outs_s = [], []
    for l in range(depth):
        weights = _prep_weights(l, g_norm_mix, w_in, b_f, ssm_a_re, ssm_a_im, ssm_log_step,
                                ssm_b_re, ssm_b_im, ssm_c_re, ssm_c_im, ssm_d, w_glu, g_q, g_k,
                                g_out_ssm, g_out_attn, w_out, g_norm_mlp, w_up, w_down)
        n_p, l_p = y_p.shape[0], y_p.shape[1]
        y_p, *rest_p = _layer(y_p, weights, n_seq=n_p, seq_len=l_p, tiles=PROMPT_TILES)
        n_s, l_s = y_s.shape[0], y_s.shape[1]
        h0 = _state_from_gp(state_ssm_re[l], state_ssm_im[l])
        y_s, *rest_s = _layer(y_s, weights, n_seq=n_s, seq_len=l_s,
                              cache=(cache_k[l], cache_v[l], cache_logf[l]), h0=h0,
                              tiles=SAMPLE_TILES)
        outs_p.append(rest_p)
        outs_s.append(rest_s)
    stack = lambda outs, idx: jnp.stack([o[idx] for o in outs])
    return (y_p, y_s,
            stack(outs_p, 0), stack(outs_p, 1), stack(outs_p, 2), stack(outs_p, 3), stack(outs_p, 4),
            stack(outs_s, 0), stack(outs_s, 1), stack(outs_s, 2), stack(outs_s, 3), stack(outs_s, 4))
```

```python
import functools
import math

import jax
import jax.numpy as jnp
from jax import lax
from jax.experimental import pallas as pl
from jax.experimental.pallas import tpu as pltpu

D_MODEL = 2048
D_SSM = 1024
SSM_GROUP = 16
N_SSM_GROUPS = 64
SSM_STATE = 64
D_ATTN = 1024
HEAD_DIM = 128
N_HEADS = 8
D_FF = 8192
EPS = 1e-6
NEG_INF = -1e30
LOG2E = math.log2(math.e)

LANES = 128
SLAB = 256
N_SLABS = D_SSM // SLAB
SLAB_GROUPS = SLAB // SSM_GROUP
SLAB_STATE = SLAB_GROUPS * SSM_STATE
VMEM_LIMIT = 56 * 1024 * 1024

F32 = jnp.float32
BF16 = jnp.bfloat16


def _params(sem, vmem=VMEM_LIMIT):
    return pltpu.CompilerParams(dimension_semantics=sem, vmem_limit_bytes=vmem)


def _rms(x, g):
    return x * lax.rsqrt(jnp.mean(x * x, axis=-1, keepdims=True) + EPS) * g


def _in_proj_kernel(x_ref, g_ref, w_ref, wf_ref, bf_ref, gq_ref, gk_ref,
                    u_ref, qb_ref, k_ref, kb_ref, v_ref, vb_ref, lf_ref):
    hb = _rms(x_ref[...], g_ref[...]).astype(BF16)

    u_ref[...] = jnp.dot(hb, w_ref[:, 0:D_SSM], preferred_element_type=F32)

    q = jnp.dot(hb, w_ref[:, D_SSM:D_SSM + D_ATTN], preferred_element_type=F32)
    gq = gq_ref[...]
    for h in range(N_HEADS):
        sl = slice(h * HEAD_DIM, (h + 1) * HEAD_DIM)
        qb_ref[:, sl] = (_rms(q[:, sl], gq) * (HEAD_DIM ** -0.5 * LOG2E)).astype(BF16)

    k = jnp.dot(hb, w_ref[:, D_SSM + D_ATTN:D_SSM + 2 * D_ATTN], preferred_element_type=F32)
    gk = gk_ref[...]
    for h in range(N_HEADS):
        sl = slice(h * HEAD_DIM, (h + 1) * HEAD_DIM)
        kn = _rms(k[:, sl], gk)
        k_ref[:, sl] = kn
        kb_ref[:, sl] = kn.astype(BF16)

    v = jnp.dot(hb, w_ref[:, D_SSM + 2 * D_ATTN:D_SSM + 3 * D_ATTN], preferred_element_type=F32)
    v_ref[...] = v
    vb_ref[...] = v.astype(BF16)

    zf = jnp.dot(hb, wf_ref[...], preferred_element_type=F32) + bf_ref[...]
    lf_ref[...] = jnp.minimum(zf, 0.0) - jnp.log1p(jnp.exp(-jnp.abs(zf)))


def _in_proj(x, g, w_main, w_f, b_f, g_q, g_k, *, tt):
    t = x.shape[0]
    assert t % tt == 0
    row = lambda i: (i, 0)
    const = lambda i: (0, 0)
    once = pl.Buffered(1)
    wide = lambda dt: jax.ShapeDtypeStruct((t, D_ATTN), dt)
    return pl.pallas_call(
        _in_proj_kernel,
        grid=(t // tt,),
        in_specs=[
            pl.BlockSpec((tt, D_MODEL), row),
            pl.BlockSpec((1, D_MODEL), const),
            pl.BlockSpec((D_MODEL, D_SSM + 3 * D_ATTN), const, pipeline_mode=once),
            pl.BlockSpec((D_MODEL, LANES), const, pipeline_mode=once),
            pl.BlockSpec((1, LANES), const),
            pl.BlockSpec((1, HEAD_DIM), const),
            pl.BlockSpec((1, HEAD_DIM), const),
        ],
        out_specs=[
            pl.BlockSpec((tt, D_SSM), row),
            pl.BlockSpec((tt, D_ATTN), row),
            pl.BlockSpec((tt, D_ATTN), row),
            pl.BlockSpec((tt, D_ATTN), row),
            pl.BlockSpec((tt, D_ATTN), row),
            pl.BlockSpec((tt, D_ATTN), row),
            pl.BlockSpec((tt, LANES), row),
        ],
        out_shape=[wide(F32), wide(BF16), wide(F32), wide(BF16), wide(F32), wide(BF16),
                   jax.ShapeDtypeStruct((t, LANES), F32)],
        compiler_params=_params(("arbitrary",)),
        name="in_proj",
    )(x, g, w_main, w_f, b_f, g_q, g_k)


def _cumsum_kernel(x_ref, o_ref):
    rows, length = x_ref.shape
    lane = lax.broadcasted_iota(jnp.int32, (rows, LANES), 1)
    carry = jnp.zeros((rows, 1), F32)
    for b in range(length // LANES):
        sl = slice(b * LANES, (b + 1) * LANES)
        x = x_ref[:, sl]
        d = 1
        while d < LANES:
            x = x + jnp.where(lane >= d, pltpu.roll(x, d, axis=1), 0.0)
            d *= 2
        x = x + carry
        o_ref[:, sl] = x * LOG2E
        carry = x[:, LANES - 1:LANES]


def _cumsum_lanes(x):
    rows, length = x.shape
    assert length % LANES == 0 and rows % 8 == 0
    return pl.pallas_call(
        _cumsum_kernel,
        out_shape=jax.ShapeDtypeStruct((rows, length), F32),
        name="cumsum",
    )(x)


def _s5_tables(a_re, a_im, log_step, b_re, b_im, c_re, c_im, d, tc):
    step = jnp.exp(log_step)[:, None]
    mag = jnp.exp(a_re * step)
    abar_re = mag * jnp.cos(a_im * step)
    abar_im = mag * jnp.sin(a_im * step)
    den = a_re * a_re + a_im * a_im
    nr = abar_re - 1.0
    ni = abar_im
    fr = (nr * a_re + ni * a_im) / den
    fi = (ni * a_re - nr * a_im) / den
    bbar_re = fr[..., None] * b_re - fi[..., None] * b_im
    bbar_im = fr[..., None] * b_im + fi[..., None] * b_re
    eye = jnp.eye(SLAB_GROUPS, dtype=F32)

    def in_blockdiag(b):
        b = b.reshape(N_SLABS, SLAB_GROUPS, SSM_STATE, SSM_GROUP)
        return jnp.einsum('sgph,gk->sghkp', b, eye).reshape(N_SLABS, SLAB, SLAB_STATE)

    def out_blockdiag(c):
        c = c.reshape(N_SLABS, SLAB_GROUPS, SSM_GROUP, SSM_STATE)
        return jnp.einsum('sghp,gk->sgpkh', c, eye).reshape(N_SLABS, SLAB_STATE, SLAB)

    bb = jnp.concatenate([in_blockdiag(bbar_re), in_blockdiag(bbar_im)], axis=-1).astype(BF16)
    cc = jnp.concatenate([out_blockdiag(c_re), out_blockdiag(-c_im)], axis=1).astype(BF16)
    n = jnp.arange(1, tc + 1, dtype=F32)[:, None, None]
    pmag = jnp.exp(a_re[None] * step[None] * n)
    ang = a_im[None] * step[None] * n
    p_re = (pmag * jnp.cos(ang)).reshape(tc, N_SLABS, SLAB_STATE).transpose(1, 0, 2)
    p_im = (pmag * jnp.sin(ang)).reshape(tc, N_SLABS, SLAB_STATE).transpose(1, 0, 2)
    pw = jnp.concatenate([p_re, p_im], axis=-1)
    dd = d.reshape(N_SLABS, 1, SLAB)
    return bb, cc, pw, dd


def _s5_kernel(u_ref, bb_ref, cc_ref, pw_ref, d_ref, h0_ref, y_ref, hfin_ref,
               st_s, g_s, carry_s, *, tc, nc, tiles_per_seq, carry_mode):
    i = pl.program_id(1)
    u = u_ref[...]
    nblk = SLAB_STATE // LANES
    bu = jnp.dot(u.astype(BF16), bb_ref[...], preferred_element_type=F32)
    for b in range(2 * nblk):
        st_s[b] = bu[:, b * LANES:(b + 1) * LANES]
    lanes = lambda b: slice(b * LANES, (b + 1) * LANES)

    if carry_mode:
        @pl.when(i % tiles_per_seq == 0)
        def _():
            carry_s[...] = jnp.zeros_like(carry_s)
    else:
        g_s[...] = h0_ref[...]

    for b in range(nblk):
        re, im = lanes(b), lanes(nblk + b)
        ar = pw_ref[0:1, re]
        ai = pw_ref[0:1, im]
        hr = jnp.zeros((nc, LANES), F32)
        hi = jnp.zeros((nc, LANES), F32)
        for t in range(tc):
            rows = pl.ds(t, nc, stride=tc)
            hr, hi = (ar * hr - ai * hi + st_s[b, rows, :],
                      ar * hi + ai * hr + st_s[nblk + b, rows, :])
            st_s[b, rows, :] = hr
            st_s[nblk + b, rows, :] = hi

        ar = pw_ref[tc - 1:tc, re]
        ai = pw_ref[tc - 1:tc, im]
        if carry_mode:
            gr = carry_s[0:1, re]
            gi = carry_s[0:1, im]
            for c in range(nc):
                g_s[c:c + 1, re] = gr
                g_s[c:c + 1, im] = gi
                gr, gi = (ar * gr - ai * gi + hr[c:c + 1, :],
                          ar * gi + ai * gr + hi[c:c + 1, :])
            carry_s[0:1, re] = gr
            carry_s[0:1, im] = gi
            hfin_ref[0:1, re] = gr
            hfin_ref[0:1, im] = gi
            gr = g_s[:, re]
            gi = g_s[:, im]
        else:
            gr = g_s[:, re]
            gi = g_s[:, im]
            hfin_ref[:, re] = ar * gr - ai * gi + hr
            hfin_ref[:, im] = ar * gi + ai * gr + hi

        for t in range(tc):
            rows = pl.ds(t, nc, stride=tc)
            pr = pw_ref[t:t + 1, re]
            pi = pw_ref[t:t + 1, im]
            st_s[b, rows, :] = st_s[b, rows, :] + (pr * gr - pi * gi)
            st_s[nblk + b, rows, :] = st_s[nblk + b, rows, :] + (pr * gi + pi * gr)

    h = jnp.concatenate([st_s[b].astype(BF16) for b in range(2 * nblk)], axis=1)
    y_ref[...] = jnp.dot(h, cc_ref[...], preferred_element_type=F32) + d_ref[...] * u


def _s5(u, tables, h0, *, tt, tc, seq_len, carry_mode):
    bb, cc, pw, dd = tables
    t = u.shape[0]
    nc = tt // tc
    n_tiles = t // tt
    assert t % tt == 0 and tt % tc == 0
    if carry_mode:
        assert seq_len % tt == 0
        tiles_per_seq = seq_len // tt
        n_seq = t // seq_len
        hfin_shape = (N_SLABS, n_seq, 1, 2 * SLAB_STATE)
        hfin_spec = pl.BlockSpec((None, None, 1, 2 * SLAB_STATE),
                                 lambda s, i: (s, i // tiles_per_seq, 0, 0))
        h0 = jnp.zeros((N_SLABS, 8, 2 * SLAB_STATE), F32)
        h0_spec = pl.BlockSpec((None, 8, 2 * SLAB_STATE), lambda s, i: (s, 0, 0))
    else:
        assert seq_len == tc
        tiles_per_seq = 1
        hfin_shape = (N_SLABS, n_tiles, nc, 2 * SLAB_STATE)
        hfin_spec = pl.BlockSpec((None, None, nc, 2 * SLAB_STATE), lambda s, i: (s, i, 0, 0))
        h0_spec = pl.BlockSpec((None, nc, 2 * SLAB_STATE), lambda s, i: (s, i, 0))
    kern = functools.partial(_s5_kernel, tc=tc, nc=nc, tiles_per_seq=tiles_per_seq,
                             carry_mode=carry_mode)
    y, hfin = pl.pallas_call(
        kern,
        grid=(N_SLABS, n_tiles),
        in_specs=[
            pl.BlockSpec((tt, SLAB), lambda s, i: (i, s)),
            pl.BlockSpec((None, SLAB, 2 * SLAB_STATE), lambda s, i: (s, 0, 0)),
            pl.BlockSpec((None, 2 * SLAB_STATE, SLAB), lambda s, i: (s, 0, 0)),
            pl.BlockSpec((None, tc, 2 * SLAB_STATE), lambda s, i: (s, 0, 0)),
            pl.BlockSpec((None, 1, SLAB), lambda s, i: (s, 0, 0)),
            h0_spec,
        ],
        out_specs=[pl.BlockSpec((tt, SLAB), lambda s, i: (i, s)), hfin_spec],
        out_shape=[jax.ShapeDtypeStruct((t, D_SSM), F32),
                   jax.ShapeDtypeStruct(hfin_shape, F32)],
        scratch_shapes=[pltpu.VMEM((2 * SLAB_STATE // LANES, tt, LANES), F32),
                        pltpu.VMEM((nc, 2 * SLAB_STATE), F32),
                        pltpu.VMEM((8, 2 * SLAB_STATE), F32)],
        compiler_params=_params(("arbitrary", "arbitrary")),
        name="s5",
    )(u, bb, cc, pw, dd, h0)
    return y, hfin.reshape(N_SLABS, -1, 2 * SLAB_STATE)


def _state_to_gp(hfin):
    n = hfin.shape[1]
    h = hfin.transpose(1, 0, 2)
    re = h[..., :SLAB_STATE].reshape(n, N_SSM_GROUPS, SSM_STATE)
    im = h[..., SLAB_STATE:].reshape(n, N_SSM_GROUPS, SSM_STATE)
    return re, im


def _state_from_gp(re, im):
    n = re.shape[0]
    h = jnp.concatenate([re.reshape(n, N_SLABS, SLAB_STATE), im.reshape(n, N_SLABS, SLAB_STATE)],
                        axis=-1)
    return h.transpose(1, 0, 2)


def _col_from_row(row):
    n = row.shape[1]
    r = lax.broadcasted_iota(jnp.int32, (n, n), 0)
    c = lax.broadcasted_iota(jnp.int32, (n, n), 1)
    return jnp.sum(jnp.where(r == c, jnp.broadcast_to(row, (n, n)), 0.0), axis=1, keepdims=True)


def _attn_prompt_kernel(q_ref, k_ref, v_ref, cq_ref, ck_ref, o_ref, m_s, l_s, acc_s, cq_s, s_s,
                        *, bq):
    i = pl.program_id(1)
    q = q_ref[...]
    cq_s[...] = jnp.broadcast_to(_col_from_row(cq_ref[...]), (bq, LANES))
    m_s[...] = jnp.full_like(m_s, NEG_INF)
    l_s[...] = jnp.zeros_like(l_s)
    acc_s[...] = jnp.zeros_like(acc_s)
    nchunk = bq // LANES

    def scores(j, slot):
        start = pl.multiple_of(j * bq, bq)
        s_s[slot] = lax.dot_general(q, k_ref[pl.ds(start, bq), :], (((1,), (1,)), ((), ())),
                                    preferred_element_type=F32)

    def reduce_block(j, slot, masked):
        start = pl.multiple_of(j * bq, bq)
        v = v_ref[pl.ds(start, bq), :]

        def logits(c):
            col0 = pl.multiple_of(start + c * LANES, LANES)
            t = s_s[slot, :, c * LANES:(c + 1) * LANES] - ck_ref[:, pl.ds(col0, LANES)]
            if masked:
                r = lax.broadcasted_iota(jnp.int32, (bq, LANES), 0)
                col = lax.broadcasted_iota(jnp.int32, (bq, LANES), 1) + c * LANES
                t = jnp.where(col <= r, t, NEG_INF)
            return t

        m_loc = functools.reduce(jnp.maximum, [logits(c) for c in range(nchunk)])
        m_old = m_s[...]
        m_new = jnp.maximum(m_old, jnp.max(m_loc, axis=1, keepdims=True) + cq_s[...])
        m_s[...] = m_new
        alpha = jnp.exp2(m_old - m_new)
        shift = cq_s[...] - m_new
        ps = [jnp.exp2(logits(c) + shift) for c in range(nchunk)]
        l_s[...] = alpha * l_s[...] + functools.reduce(jnp.add, ps)
        p = jnp.concatenate([pc.astype(BF16) for pc in ps], axis=1)
        acc_s[...] = alpha * acc_s[...] + jnp.dot(p, v, preferred_element_type=F32)

    scores(0, 0)

    def body(p, carry):
        j = 2 * p
        scores(j + 1, 1)
        reduce_block(j, 0, False)
        scores(j + 2, 0)
        reduce_block(j + 1, 1, False)
        return carry

    lax.fori_loop(0, i // 2, body, 0)

    @pl.when(i % 2 == 1)
    def _():
        scores(i, 1)
        reduce_block(i - 1, 0, False)
        reduce_block(i, 1, True)

    @pl.when(i % 2 == 0)
    def _():
        reduce_block(i, 0, True)

    o_ref[...] = acc_s[...] / jnp.sum(l_s[...], axis=1, keepdims=True)


def _attn_prompt(qb, kb, vb, c_rows, *, n_seq, seq_len, bq):
    nq = seq_len // bq
    assert seq_len % bq == 0
    kern = functools.partial(_attn_prompt_kernel, bq=bq)
    seq_blocks = seq_len // bq
    return pl.pallas_call(
        kern,
        grid=(n_seq * N_HEADS, nq),
        in_specs=[
            pl.BlockSpec((bq, HEAD_DIM), lambda g, i: ((g // N_HEADS) * seq_blocks + i, g % N_HEADS)),
            pl.BlockSpec((seq_len, HEAD_DIM), lambda g, i: (g // N_HEADS, g % N_HEADS)),
            pl.BlockSpec((seq_len, HEAD_DIM), lambda g, i: (g // N_HEADS, g % N_HEADS)),
            pl.BlockSpec((None, 1, bq), lambda g, i: (g, 0, i)),
            pl.BlockSpec((None, 1, seq_len), lambda g, i: (g, 0, 0)),
        ],
        out_specs=pl.BlockSpec((bq, HEAD_DIM),
                               lambda g, i: ((g // N_HEADS) * seq_blocks + i, g % N_HEADS)),
        out_shape=jax.ShapeDtypeStruct((n_seq * seq_len, D_ATTN), F32),
        scratch_shapes=[pltpu.VMEM((bq, LANES), F32), pltpu.VMEM((bq, LANES), F32),
                        pltpu.VMEM((bq, HEAD_DIM), F32), pltpu.VMEM((bq, LANES), F32),
                        pltpu.VMEM((2, bq, bq), F32)],
        compiler_params=_params(("arbitrary", "arbitrary")),
        name="attn_prompt",
    )(qb, kb, vb, c_rows, c_rows)


def _attn_sample_kernel(q_ref, kn_ref, vn_ref, ck_ref, cv_ref, c_ref, o_ref, *, past, s_new):
    r = lax.broadcasted_iota(jnp.int32, (s_new, s_new), 0)
    col = lax.broadcasted_iota(jnp.int32, (s_new, s_new), 1)
    causal = col <= r
    nt = (((1,), (1,)), ((), ()))
    for h in range(N_HEADS):
        sl = slice(h * HEAD_DIM, (h + 1) * HEAD_DIM)
        q = q_ref[:, sl]
        c_past = c_ref[h:h + 1, 0:past]
        c_new = c_ref[h:h + 1, past:past + s_new]
        cq = _col_from_row(c_new)
        head_rows = pl.ds(h, past, stride=N_HEADS)
        s_p = lax.dot_general(q, ck_ref[head_rows, :].astype(BF16), nt, preferred_element_type=F32)
        s_p = s_p + cq - c_past
        s_n = lax.dot_general(q, kn_ref[:, sl], nt, preferred_element_type=F32)
        s_n = jnp.where(causal, s_n + cq - c_new, NEG_INF)
        m = jnp.maximum(jnp.max(s_p, axis=1, keepdims=True), jnp.max(s_n, axis=1, keepdims=True))
        p_p = jnp.exp2(s_p - m)
        p_n = jnp.exp2(s_n - m)
        l = jnp.sum(p_p, axis=1, keepdims=True) + jnp.sum(p_n, axis=1, keepdims=True)
        acc = (jnp.dot(p_p.astype(BF16), cv_ref[head_rows, :].astype(BF16), preferred_element_type=F32)
               + jnp.dot(p_n.astype(BF16), vn_ref[:, sl], preferred_element_type=F32))
        o_ref[:, sl] = acc / l


def _attn_sample(qb, kb, vb, cache_k, cache_v, c_all, *, n_seq, s_new, past):
    kern = functools.partial(_attn_sample_kernel, past=past, s_new=s_new)
    new = pl.BlockSpec((s_new, D_ATTN), lambda b: (b, 0))
    cache_k = cache_k.reshape(n_seq, past * N_HEADS, HEAD_DIM)
    cache_v = cache_v.reshape(n_seq, past * N_HEADS, HEAD_DIM)
    cache = pl.BlockSpec((None, past * N_HEADS, HEAD_DIM), lambda b: (b, 0, 0))
    return pl.pallas_call(
        kern,
        grid=(n_seq,),
        in_specs=[new, new, new, cache, cache,
                  pl.BlockSpec((N_HEADS, c_all.shape[1]), lambda b: (b, 0))],
        out_specs=new,
        out_shape=jax.ShapeDtypeStruct((n_seq * s_new, D_ATTN), F32),
        compiler_params=_params(("arbitrary",)),
        name="attn_sample",
    )(qb, kb, vb, cache_k, cache_v, c_all)


def _gelu_tanh(x):
    return 0.5 * x * (1.0 + jnp.tanh(math.sqrt(2.0 / math.pi) * (x + 0.044715 * (x * x * x))))


def _mix_kernel(x_ref, y_ref, a_ref, wglu_ref, gs_ref, ga_ref, wout_ref, gm_ref, x1_ref, hm_ref):
    gy = _gelu_tanh(y_ref[...])
    gate = jax.nn.sigmoid(jnp.dot(gy.astype(BF16), wglu_ref[...], preferred_element_type=F32))
    ssm = _rms(gy * gate, gs_ref[...]).astype(BF16)
    att = _rms(a_ref[...], ga_ref[...]).astype(BF16)
    x1 = (x_ref[...]
          + jnp.dot(ssm, wout_ref[0:D_SSM, :], preferred_element_type=F32)
          + jnp.dot(att, wout_ref[D_SSM:D_SSM + D_ATTN, :], preferred_element_type=F32))
    x1_ref[...] = x1
    hm_ref[...] = _rms(x1, gm_ref[...]).astype(BF16)


def _mix(x, y, attn, w_glu, g_ssm, g_attn, w_out, g_mlp, *, tt):
    t = x.shape[0]
    assert t % tt == 0
    row = lambda i: (i, 0)
    const = lambda i: (0, 0)
    once = pl.Buffered(1)
    return pl.pallas_call(
        _mix_kernel,
        grid=(t // tt,),
        in_specs=[
            pl.BlockSpec((tt, D_MODEL), row),
            pl.BlockSpec((tt, D_SSM), row),
            pl.BlockSpec((tt, D_ATTN), row),
            pl.BlockSpec((D_SSM, D_SSM), const, pipeline_mode=once),
            pl.BlockSpec((1, D_SSM), const),
            pl.BlockSpec((1, D_ATTN), const),
            pl.BlockSpec((D_SSM + D_ATTN, D_MODEL), const, pipeline_mode=once),
            pl.BlockSpec((1, D_MODEL), const),
        ],
        out_specs=[pl.BlockSpec((tt, D_MODEL), row), pl.BlockSpec((tt, D_MODEL), row)],
        out_shape=[jax.ShapeDtypeStruct((t, D_MODEL), F32), jax.ShapeDtypeStruct((t, D_MODEL), BF16)],
        compiler_params=_params(("arbitrary",)),
        name="mix",
    )(x, y, attn, w_glu, g_ssm, g_attn, w_out, g_mlp)


def _mlp_kernel(x1_ref, hm_ref, wup_ref, wdn_ref, o_ref):
    j = pl.program_id(1)

    @pl.when(j == 0)
    def _():
        o_ref[...] = x1_ref[...]

    a = jnp.maximum(jnp.dot(hm_ref[...], wup_ref[...], preferred_element_type=F32), 0.0)
    o_ref[...] += jnp.dot((a * a).astype(BF16), wdn_ref[...], preferred_element_type=F32)


def _mlp(x1, hm, w_up, w_down, *, tt, tf):
    t = x1.shape[0]
    assert t % tt == 0 and D_FF % tf == 0
    return pl.pallas_call(
        _mlp_kernel,
        grid=(t // tt, D_FF // tf),
        in_specs=[
            pl.BlockSpec((tt, D_MODEL), lambda i, j: (i, 0)),
            pl.BlockSpec((tt, D_MODEL), lambda i, j: (i, 0)),
            pl.BlockSpec((D_MODEL, tf), lambda i, j: (0, j)),
            pl.BlockSpec((tf, D_MODEL), lambda i, j: (j, 0)),
        ],
        out_specs=pl.BlockSpec((tt, D_MODEL), lambda i, j: (i, 0)),
        out_shape=jax.ShapeDtypeStruct((t, D_MODEL), F32),
        compiler_params=_params(("arbitrary", "arbitrary")),
        name="mlp",
    )(x1, hm, w_up, w_down)


def _layer(x, weights, *, n_seq, seq_len, cache=None, h0=None, tiles):
    (g_mix, w_main, w_f, b_f, g_q, g_k, s5_raw, w_glu, g_ssm, g_attn, w_out, g_mlp,
     w_up, w_down) = weights
    t = n_seq * seq_len
    x2 = x.reshape(t, D_MODEL)
    u, qb, k, kb, v, vb, lf = _in_proj(x2, g_mix, w_main, w_f, b_f, g_q, g_k, tt=tiles["proj"])
    logf = lf[:, :N_HEADS]
    lf_rows = logf.reshape(n_seq, seq_len, N_HEADS).transpose(0, 2, 1).reshape(n_seq * N_HEADS, seq_len)

    if cache is None:
        tables = _s5_tables(*s5_raw, tc=16)
        y, hfin = _s5(u, tables, None, tt=tiles["s5"], tc=16, seq_len=seq_len, carry_mode=True)
        c_rows = _cumsum_lanes(lf_rows)
        attn = _attn_prompt(qb, kb, vb, c_rows.reshape(n_seq * N_HEADS, 1, seq_len),
                            n_seq=n_seq, seq_len=seq_len, bq=tiles["bq"])
    else:
        cache_k, cache_v, cache_logf = cache
        past = cache_k.shape[1]
        tables = _s5_tables(*s5_raw, tc=seq_len)
        y, hfin = _s5(u, tables, h0, tt=tiles["s5"], tc=seq_len, seq_len=seq_len, carry_mode=False)
        past_rows = cache_logf.transpose(0, 2, 1).reshape(n_seq * N_HEADS, past)
        total = past + seq_len
        padded = -(-total // LANES) * LANES
        lf_all = jnp.concatenate(
            [past_rows, lf_rows, jnp.zeros((n_seq * N_HEADS, padded - total), F32)], axis=1)
        c_all = _cumsum_lanes(lf_all)
        attn = _attn_sample(qb, kb, vb, cache_k, cache_v, c_all,
                            n_seq=n_seq, s_new=seq_len, past=past)

    x1, hm = _mix(x2, y, attn, w_glu, g_ssm, g_attn, w_out, g_mlp, tt=tiles["mix"])
    out = _mlp(x1, hm, w_up, w_down, tt=tiles["mlp"], tf=tiles["tf"])
    h_re, h_im = _state_to_gp(hfin)
    return (out.reshape(n_seq, seq_len, D_MODEL),
            k.reshape(n_seq, seq_len, N_HEADS, HEAD_DIM),
            v.reshape(n_seq, seq_len, N_HEADS, HEAD_DIM),
            logf.reshape(n_seq, seq_len, N_HEADS), h_re, h_im)


def _prep_weights(l, g_norm_mix, w_in, b_f, ssm_a_re, ssm_a_im, ssm_log_step, ssm_b_re, ssm_b_im,
                  ssm_c_re, ssm_c_im, ssm_d, w_glu, g_q, g_k, g_out_ssm, g_out_attn, w_out,
                  g_norm_mlp, w_up, w_down):
    n_main = D_SSM + 3 * D_ATTN
    w = w_in[l]
    w_main = w[:, :n_main].astype(BF16)
    w_f = jnp.pad(w[:, n_main:], ((0, 0), (0, LANES - N_HEADS))).astype(BF16)
    b = jnp.pad(b_f[l], (0, LANES - N_HEADS)).reshape(1, LANES)
    s5_raw = (ssm_a_re[l], ssm_a_im[l], ssm_log_step[l], ssm_b_re[l], ssm_b_im[l],
              ssm_c_re[l], ssm_c_im[l], ssm_d[l])
    return (g_norm_mix[l].reshape(1, D_MODEL), w_main, w_f, b,
            g_q[l].reshape(1, HEAD_DIM), g_k[l].reshape(1, HEAD_DIM), s5_raw,
            w_glu[l].astype(BF16), g_out_ssm[l].reshape(1, D_SSM), g_out_attn[l].reshape(1, D_ATTN),
            w_out[l].astype(BF16), g_norm_mlp[l].reshape(1, D_MODEL),
            w_up[l].astype(BF16), w_down[l].astype(BF16))


PROMPT_TILES = dict(proj=512, s5=512, bq=512, mix=256, mlp=512, tf=1024)
SAMPLE_TILES = dict(proj=512, s5=512, mix=256, mlp=512, tf=1024)


def kernel(x_prompt, x_sample, cache_k, cache_v, cache_logf, state_ssm_re, state_ssm_im,
           g_norm_mix, w_in, b_f, ssm_a_re, ssm_a_im, ssm_log_step, ssm_b_re, ssm_b_im,
           ssm_c_re, ssm_c_im, ssm_d, w_glu, g_q, g_k, g_out_ssm, g_out_attn, w_out,
           g_norm_mlp, w_up, w_down):
    depth = w_in.shape[0]
    y_p, y_s = x_prompt, x_sample
    outs_p, outs_s = [], []
    for l in range(depth):
        weights = _prep_weights(l, g_norm_mix, w_in, b_f, ssm_a_re, ssm_a_im, ssm_log_step,
                                ssm_b_re, ssm_b_im, ssm_c_re, ssm_c_im, ssm_d, w_glu, g_q, g_k,
                                g_out_ssm, g_out_attn, w_out, g_norm_mlp, w_up, w_down)
        n_p, l_p = y_p.shape[0], y_p.shape[1]
        y_p, *rest_p = _layer(y_p, weights, n_seq=n_p, seq_len=l_p, tiles=PROMPT_TILES)
        n_s, l_s = y_s.shape[0], y_s.shape[1]
        h0 = _state_from_gp(state_ssm_re[l], state_ssm_im[l])
        y_s, *rest_s = _layer(y_s, weights, n_seq=n_s, seq_len=l_s,
                              cache=(cache_k[l], cache_v[l], cache_logf[l]), h0=h0,
                              tiles=SAMPLE_TILES)
        outs_p.append(rest_p)
        outs_s.append(rest_s)
    stack = lambda outs, idx: jnp.stack([o[idx] for o in outs])
    return (y_p, y_s,
            stack(outs_p, 0), stack(outs_p, 1), stack(outs_p, 2), stack(outs_p, 3), stack(outs_p, 4),
            stack(outs_s, 0), stack(outs_s, 1), stack(outs_s, 2), stack(outs_s, 3), stack(outs_s, 4))
```

```python
import functools
import math

import jax
import jax.numpy as jnp
from jax import lax
from jax.experimental import pallas as pl
from jax.experimental.pallas import tpu as pltpu

D_MODEL = 2048
D_SSM = 1024
SSM_GROUP = 16
N_SSM_GROUPS = 64
SSM_STATE = 64
D_ATTN = 1024
HEAD_DIM = 128
N_HEADS = 8
D_FF = 8192
EPS = 1e-6
NEG_INF = -1e30
LOG2E = math.log2(math.e)

LANES = 128
SLAB = 256
N_SLABS = D_SSM // SLAB
SLAB_GROUPS = SLAB // SSM_GROUP
SLAB_STATE = SLAB_GROUPS * SSM_STATE
VMEM_LIMIT = 56 * 1024 * 1024

F32 = jnp.float32
BF16 = jnp.bfloat16


def _params(sem, vmem=VMEM_LIMIT):
    return pltpu.CompilerParams(dimension_semantics=sem, vmem_limit_bytes=vmem)


def _rms(x, g):
    return x * lax.rsqrt(jnp.mean(x * x, axis=-1, keepdims=True) + EPS) * g


def _in_proj_kernel(x_ref, g_ref, w_ref, wf_ref, bf_ref, gq_ref, gk_ref,
                    u_ref, qb_ref, k_ref, kb_ref, v_ref, vb_ref, lf_ref):
    hb = _rms(x_ref[...], g_ref[...]).astype(BF16)

    u_ref[...] = jnp.dot(hb, w_ref[:, 0:D_SSM], preferred_element_type=F32)

    q = jnp.dot(hb, w_ref[:, D_SSM:D_SSM + D_ATTN], preferred_element_type=F32)
    gq = gq_ref[...]
    for h in range(N_HEADS):
        sl = slice(h * HEAD_DIM, (h + 1) * HEAD_DIM)
        qb_ref[:, sl] = (_rms(q[:, sl], gq) * (HEAD_DIM ** -0.5 * LOG2E)).astype(BF16)

    k = jnp.dot(hb, w_ref[:, D_SSM + D_ATTN:D_SSM + 2 * D_ATTN], preferred_element_type=F32)
    gk = gk_ref[...]
    for h in range(N_HEADS):
        sl = slice(h * HEAD_DIM, (h + 1) * HEAD_DIM)
        kn = _rms(k[:, sl], gk)
        k_ref[:, sl] = kn
        kb_ref[:, sl] = kn.astype(BF16)

    v = jnp.dot(hb, w_ref[:, D_SSM + 2 * D_ATTN:D_SSM + 3 * D_ATTN], preferred_element_type=F32)
    v_ref[...] = v
    vb_ref[...] = v.astype(BF16)

    zf = jnp.dot(hb, wf_ref[...], preferred_element_type=F32) + bf_ref[...]
    lf_ref[...] = jnp.minimum(zf, 0.0) - jnp.log1p(jnp.exp(-jnp.abs(zf)))


def _in_proj(x, g, w_main, w_f, b_f, g_q, g_k, *, tt):
    t = x.shape[0]
    assert t % tt == 0
    row = lambda i: (i, 0)
    const = lambda i: (0, 0)
    once = pl.Buffered(1)
    wide = lambda dt: jax.ShapeDtypeStruct((t, D_ATTN), dt)
    return pl.pallas_call(
        _in_proj_kernel,
        grid=(t // tt,),
        in_specs=[
            pl.BlockSpec((tt, D_MODEL), row),
            pl.BlockSpec((1, D_MODEL), const),
            pl.BlockSpec((D_MODEL, D_SSM + 3 * D_ATTN), const, pipeline_mode=once),
            pl.BlockSpec((D_MODEL, LANES), const, pipeline_mode=once),
            pl.BlockSpec((1, LANES), const),
            pl.BlockSpec((1, HEAD_DIM), const),
            pl.BlockSpec((1, HEAD_DIM), const),
        ],
        out_specs=[
            pl.BlockSpec((tt, D_SSM), row),
            pl.BlockSpec((tt, D_ATTN), row),
            pl.BlockSpec((tt, D_ATTN), row),
            pl.BlockSpec((tt, D_ATTN), row),
            pl.BlockSpec((tt, D_ATTN), row),
            pl.BlockSpec((tt, D_ATTN), row),
            pl.BlockSpec((tt, LANES), row),
        ],
        out_shape=[wide(F32), wide(BF16), wide(F32), wide(BF16), wide(F32), wide(BF16),
                   jax.ShapeDtypeStruct((t, LANES), F32)],
        compiler_params=_params(("arbitrary",)),
        name="in_proj",
    )(x, g, w_main, w_f, b_f, g_q, g_k)


def _cumsum_kernel(x_ref, o_ref):
    rows, length = x_ref.shape
    lane = lax.broadcasted_iota(jnp.int32, (rows, LANES), 1)
    carry = jnp.zeros((rows, 1), F32)
    for b in range(length // LANES):
        sl = slice(b * LANES, (b + 1) * LANES)
        x = x_ref[:, sl]
        d = 1
        while d < LANES:
            x = x + jnp.where(lane >= d, pltpu.roll(x, d, axis=1), 0.0)
            d *= 2
        x = x + carry
        o_ref[:, sl] = x * LOG2E
        carry = x[:, LANES - 1:LANES]


def _cumsum_lanes(x):
    rows, length = x.shape
    assert length % LANES == 0 and rows % 8 == 0
    return pl.pallas_call(
        _cumsum_kernel,
        out_shape=jax.ShapeDtypeStruct((rows, length), F32),
        name="cumsum",
    )(x)


def _s5_tables(a_re, a_im, log_step, b_re, b_im, c_re, c_im, d, tc):
    step = jnp.exp(log_step)[:, None]
    mag = jnp.exp(a_re * step)
    abar_re = mag * jnp.cos(a_im * step)
    abar_im = mag * jnp.sin(a_im * step)
    den = a_re * a_re + a_im * a_im
    nr = abar_re - 1.0
    ni = abar_im
    fr = (nr * a_re + ni * a_im) / den
    fi = (ni * a_re - nr * a_im) / den
    bbar_re = fr[..., None] * b_re - fi[..., None] * b_im
    bbar_im = fr[..., None] * b_im + fi[..., None] * b_re
    eye = jnp.eye(SLAB_GROUPS, dtype=F32)

    def in_blockdiag(b):
        b = b.reshape(N_SLABS, SLAB_GROUPS, SSM_STATE, SSM_GROUP)
        return jnp.einsum('sgph,gk->sghkp', b, eye).reshape(N_SLABS, SLAB, SLAB_STATE)

    def out_blockdiag(c):
        c = c.reshape(N_SLABS, SLAB_GROUPS, SSM_GROUP, SSM_STATE)
        return jnp.einsum('sghp,gk->sgpkh', c, eye).reshape(N_SLABS, SLAB_STATE, SLAB)

    bb = jnp.concatenate([in_blockdiag(bbar_re), in_blockdiag(bbar_im)], axis=-1).astype(BF16)
    cc = jnp.concatenate([out_blockdiag(c_re), out_blockdiag(-c_im)], axis=1).astype(BF16)
    n = jnp.arange(1, tc + 1, dtype=F32)[:, None, None]
    pmag = jnp.exp(a_re[None] * step[None] * n)
    ang = a_im[None] * step[None] * n
    p_re = (pmag * jnp.cos(ang)).reshape(tc, N_SLABS, SLAB_STATE).transpose(1, 0, 2)
    p_im = (pmag * jnp.sin(ang)).reshape(tc, N_SLABS, SLAB_STATE).transpose(1, 0, 2)
    pw = jnp.concatenate([p_re, p_im], axis=-1)
    dd = d.reshape(N_SLABS, 1, SLAB)
    return bb, cc, pw, dd


def _s5_kernel(u_ref, perm_ref, bb_ref, cc_ref, pw_ref, d_ref, h0_ref, y_ref, hfin_ref,
               st_s, hb_s, g_s, carry_s, *, tc, nc, tiles_per_seq, carry_mode):
    i = pl.program_id(1)
    u = u_ref[...]
    u_hi = u.astype(BF16)
    u_lo = (u - u_hi.astype(F32)).astype(BF16)
    up = jnp.dot(perm_ref[...], jnp.concatenate([u_hi, u_lo], axis=1),
                 preferred_element_type=F32)
    u_hi = up[:, :SLAB]
    u = u_hi + up[:, SLAB:]
    uh = u_hi.astype(BF16)
    y_ref[...] = d_ref[...] * u
    blk = 2 * LANES
    nblk = SLAB_STATE // blk
    grp = max(1, 16 // nc)
    lanes = lambda b: slice(b * blk, (b + 1) * blk)
    step_rows = lambda t: slice(t * nc, (t + 1) * nc)

    if carry_mode:
        @pl.when(i % tiles_per_seq == 0)
        def _():
            carry_s[...] = jnp.zeros_like(carry_s)
    else:
        g_s[...] = h0_ref[...]

    for b in range(nblk):
        re, im = lanes(b), lanes(nblk + b)
        st_s[:, re] = jnp.dot(uh, bb_ref[:, re], preferred_element_type=F32)
        st_s[:, im] = jnp.dot(uh, bb_ref[:, im], preferred_element_type=F32)
        ar = pw_ref[0:1, re]
        ai = pw_ref[0:1, im]
        hr = jnp.zeros((nc, blk), F32)
        hi = jnp.zeros((nc, blk), F32)
        for t in range(tc):
            rows = step_rows(t)
            hr, hi = (ar * hr - ai * hi + st_s[rows, re],
                      ar * hi + ai * hr + st_s[rows, im])
            st_s[rows, re] = hr
            st_s[rows, im] = hi

        ar = pw_ref[tc - 1:tc, re]
        ai = pw_ref[tc - 1:tc, im]
        if carry_mode:
            gr = carry_s[0:1, re]
            gi = carry_s[0:1, im]
            for c in range(nc):
                g_s[c:c + 1, re] = gr
                g_s[c:c + 1, im] = gi
                gr, gi = (ar * gr - ai * gi + hr[c:c + 1, :],
                          ar * gi + ai * gr + hi[c:c + 1, :])
            carry_s[0:1, re] = gr
            carry_s[0:1, im] = gi
            hfin_ref[0:1, re] = gr
            hfin_ref[0:1, im] = gi
            gr = g_s[:, re]
            gi = g_s[:, im]
        else:
            gr = g_s[:, re]
            gi = g_s[:, im]
            hfin_ref[:, re] = ar * gr - ai * gi + hr
            hfin_ref[:, im] = ar * gi + ai * gr + hi

        for t0 in range(0, tc, grp):
            full_r, full_i = [], []
            for t in range(t0, t0 + grp):
                rows = step_rows(t)
                pr = pw_ref[t:t + 1, re]
                pi = pw_ref[t:t + 1, im]
                full_r.append(st_s[rows, re] + (pr * gr - pi * gi))
                full_i.append(st_s[rows, im] + (pr * gi + pi * gr))
            rows = slice(t0 * nc, (t0 + grp) * nc)
            hb_s[rows, re] = jnp.concatenate(full_r, axis=0).astype(BF16)
            hb_s[rows, im] = jnp.concatenate(full_i, axis=0).astype(BF16)

        y_ref[...] += (jnp.dot(hb_s[:, re], cc_ref[re, :], preferred_element_type=F32)
                       + jnp.dot(hb_s[:, im], cc_ref[im, :], preferred_element_type=F32))


def _step_major_perm(tt, tc):
    nc = tt // tc
    r_out = jnp.arange(tt)
    r_in = (r_out % nc) * tc + r_out // nc
    return (r_in[:, None] == jnp.arange(tt)[None, :]).astype(BF16)


def _s5(u, tables, perm, h0, *, tt, tc, seq_len, carry_mode):
    bb, cc, pw, dd = tables
    t = u.shape[0]
    nc = tt // tc
    n_tiles = t // tt
    assert t % tt == 0 and tt % tc == 0
    if carry_mode:
        assert seq_len % tt == 0
        tiles_per_seq = seq_len // tt
        n_seq = t // seq_len
        hfin_shape = (N_SLABS, n_seq, 1, 2 * SLAB_STATE)
        hfin_spec = pl.BlockSpec((None, None, 1, 2 * SLAB_STATE),
                                 lambda s, i: (s, i // tiles_per_seq, 0, 0))
        h0 = jnp.zeros((N_SLABS, 8, 2 * SLAB_STATE), F32)
        h0_spec = pl.BlockSpec((None, 8, 2 * SLAB_STATE), lambda s, i: (s, 0, 0))
    else:
        assert seq_len == tc
        tiles_per_seq = 1
        hfin_shape = (N_SLABS, n_tiles, nc, 2 * SLAB_STATE)
        hfin_spec = pl.BlockSpec((None, None, nc, 2 * SLAB_STATE), lambda s, i: (s, i, 0, 0))
        h0_spec = pl.BlockSpec((None, nc, 2 * SLAB_STATE), lambda s, i: (s, i, 0))
    kern = functools.partial(_s5_kernel, tc=tc, nc=nc, tiles_per_seq=tiles_per_seq,
                             carry_mode=carry_mode)
    y, hfin = pl.pallas_call(
        kern,
        grid=(N_SLABS, n_tiles),
        in_specs=[
            pl.BlockSpec((tt, SLAB), lambda s, i: (i, s)),
            pl.BlockSpec((tt, tt), lambda s, i: (0, 0), pipeline_mode=pl.Buffered(1)),
            pl.BlockSpec((None, SLAB, 2 * SLAB_STATE), lambda s, i: (s, 0, 0)),
            pl.BlockSpec((None, 2 * SLAB_STATE, SLAB), lambda s, i: (s, 0, 0)),
            pl.BlockSpec((None, tc, 2 * SLAB_STATE), lambda s, i: (s, 0, 0)),
            pl.BlockSpec((None, 1, SLAB), lambda s, i: (s, 0, 0)),
            h0_spec,
        ],
        out_specs=[pl.BlockSpec((tt, SLAB), lambda s, i: (i, s)), hfin_spec],
        out_shape=[jax.ShapeDtypeStruct((t, D_SSM), F32),
                   jax.ShapeDtypeStruct(hfin_shape, F32)],
        scratch_shapes=[pltpu.VMEM((tt, 2 * SLAB_STATE), F32),
                        pltpu.VMEM((tt, 2 * SLAB_STATE), BF16),
                        pltpu.VMEM((nc, 2 * SLAB_STATE), F32),
                        pltpu.VMEM((8, 2 * SLAB_STATE), F32)],
        compiler_params=_params(("arbitrary", "arbitrary")),
        name="s5",
    )(u, perm, bb, cc, pw, dd, h0)
    return y, hfin.reshape(N_SLABS, -1, 2 * SLAB_STATE)


def _state_to_gp(hfin):
    n = hfin.shape[1]
    h = hfin.transpose(1, 0, 2)
    re = h[..., :SLAB_STATE].reshape(n, N_SSM_GROUPS, SSM_STATE)
    im = h[..., SLAB_STATE:].reshape(n, N_SSM_GROUPS, SSM_STATE)
    return re, im


def _state_from_gp(re, im):
    n = re.shape[0]
    h = jnp.concatenate([re.reshape(n, N_SLABS, SLAB_STATE), im.reshape(n, N_SLABS, SLAB_STATE)],
                        axis=-1)
    return h.transpose(1, 0, 2)


def _col_from_row(row):
    n = row.shape[1]
    r = lax.broadcasted_iota(jnp.int32, (n, n), 0)
    c = lax.broadcasted_iota(jnp.int32, (n, n), 1)
    return jnp.sum(jnp.where(r == c, jnp.broadcast_to(row, (n, n)), 0.0), axis=1, keepdims=True)


def _attn_prompt_kernel(q_ref, k_ref, v_ref, cq_ref, ck_ref, o_ref, m_s, l_s, acc_s, cq_s, s_s,
                        *, bq):
    i = pl.program_id(1)
    q = q_ref[...]
    cq_s[...] = jnp.broadcast_to(_col_from_row(cq_ref[...]), (bq, LANES))
    m_s[...] = jnp.full_like(m_s, NEG_INF)
    l_s[...] = jnp.zeros_like(l_s)
    acc_s[...] = jnp.zeros_like(acc_s)
    nchunk = bq // LANES

    def scores(j, slot):
        start = pl.multiple_of(j * bq, bq)
        s_s[slot] = lax.dot_general(q, k_ref[pl.ds(start, bq), :], (((1,), (1,)), ((), ())),
                                    preferred_element_type=F32)

    def reduce_block(j, slot, masked):
        start = pl.multiple_of(j * bq, bq)
        v = v_ref[pl.ds(start, bq), :]

        def logits(c):
            col0 = pl.multiple_of(start + c * LANES, LANES)
            t = s_s[slot, :, c * LANES:(c + 1) * LANES] - ck_ref[:, pl.ds(col0, LANES)]
            if masked:
                r = lax.broadcasted_iota(jnp.int32, (bq, LANES), 0)
                col = lax.broadcasted_iota(jnp.int32, (bq, LANES), 1) + c * LANES
                t = jnp.where(col <= r, t, NEG_INF)
            return t

        m_loc = functools.reduce(jnp.maximum, [logits(c) for c in range(nchunk)])
        m_old = m_s[...]
        m_new = jnp.maximum(m_old, jnp.max(m_loc, axis=1, keepdims=True) + cq_s[...])
        m_s[...] = m_new
        alpha = jnp.exp2(m_old - m_new)
        shift = cq_s[...] - m_new
        ps = [jnp.exp2(logits(c) + shift) for c in range(nchunk)]
        l_s[...] = alpha * l_s[...] + functools.reduce(jnp.add, ps)
        p = jnp.concatenate([pc.astype(BF16) for pc in ps], axis=1)
        acc_s[...] = alpha * acc_s[...] + jnp.dot(p, v, preferred_element_type=F32)

    scores(0, 0)

    def body(p, carry):
        j = 2 * p
        scores(j + 1, 1)
        reduce_block(j, 0, False)
        scores(j + 2, 0)
        reduce_block(j + 1, 1, False)
        return carry

    lax.fori_loop(0, i // 2, body, 0)

    @pl.when(i % 2 == 1)
    def _():
        scores(i, 1)
        reduce_block(i - 1, 0, False)
        reduce_block(i, 1, True)

    @pl.when(i % 2 == 0)
    def _():
        reduce_block(i, 0, True)

    o_ref[...] = acc_s[...] / jnp.sum(l_s[...], axis=1, keepdims=True)


def _attn_prompt(qb, kb, vb, c_rows, *, n_seq, seq_len, bq):
    nq = seq_len // bq
    assert seq_len % bq == 0
    kern = functools.partial(_attn_prompt_kernel, bq=bq)
    seq_blocks = seq_len // bq
    return pl.pallas_call(
        kern,
        grid=(n_seq * N_HEADS, nq),
        in_specs=[
            pl.BlockSpec((bq, HEAD_DIM), lambda g, i: ((g // N_HEADS) * seq_blocks + i, g % N_HEADS)),
            pl.BlockSpec((seq_len, HEAD_DIM), lambda g, i: (g // N_HEADS, g % N_HEADS)),
            pl.BlockSpec((seq_len, HEAD_DIM), lambda g, i: (g // N_HEADS, g % N_HEADS)),
            pl.BlockSpec((None, 1, bq), lambda g, i: (g, 0, i)),
            pl.BlockSpec((None, 1, seq_len), lambda g, i: (g, 0, 0)),
        ],
        out_specs=pl.BlockSpec((bq, HEAD_DIM),
                               lambda g, i: ((g // N_HEADS) * seq_blocks + i, g % N_HEADS)),
        out_shape=jax.ShapeDtypeStruct((n_seq * seq_len, D_ATTN), F32),
        scratch_shapes=[pltpu.VMEM((bq, LANES), F32), pltpu.VMEM((bq, LANES), F32),
                        pltpu.VMEM((bq, HEAD_DIM), F32), pltpu.VMEM((bq, LANES), F32),
                        pltpu.VMEM((2, bq, bq), F32)],
        compiler_params=_params(("arbitrary", "arbitrary")),
        name="attn_prompt",
    )(qb, kb, vb, c_rows, c_rows)


def _attn_sample_kernel(q_ref, kn_ref, vn_ref, ck_ref, cv_ref, c_ref, o_ref, *, past, s_new):
    r = lax.broadcasted_iota(jnp.int32, (s_new, s_new), 0)
    col = lax.broadcasted_iota(jnp.int32, (s_new, s_new), 1)
    causal = col <= r
    nt = (((1,), (1,)), ((), ()))
    for h in range(N_HEADS):
        sl = slice(h * HEAD_DIM, (h + 1) * HEAD_DIM)
        q = q_ref[:, sl]
        c_past = c_ref[h:h + 1, 0:past]
        c_new = c_ref[h:h + 1, past:past + s_new]
        cq = _col_from_row(c_new)
        head_rows = pl.ds(h, past, stride=N_HEADS)
        s_p = lax.dot_general(q, ck_ref[head_rows, :].astype(BF16), nt, preferred_element_type=F32)
        s_p = s_p + cq - c_past
        s_n = lax.dot_general(q, kn_ref[:, sl], nt, preferred_element_type=F32)
        s_n = jnp.where(causal, s_n + cq - c_new, NEG_INF)
        m = jnp.maximum(jnp.max(s_p, axis=1, keepdims=True), jnp.max(s_n, axis=1, keepdims=True))
        p_p = jnp.exp2(s_p - m)
        p_n = jnp.exp2(s_n - m)
        l = jnp.sum(p_p, axis=1, keepdims=True) + jnp.sum(p_n, axis=1, keepdims=True)
        acc = (jnp.dot(p_p.astype(BF16), cv_ref[head_rows, :].astype(BF16), preferred_element_type=F32)
               + jnp.dot(p_n.astype(BF16), vn_ref[:, sl], preferred_element_type=F32))
        o_ref[:, sl] = acc / l


def _attn_sample(qb, kb, vb, cache_k, cache_v, c_all, *, n_seq, s_new, past):
    kern = functools.partial(_attn_sample_kernel, past=past, s_new=s_new)
    new = pl.BlockSpec((s_new, D_ATTN), lambda b: (b, 0))
    cache_k = cache_k.reshape(n_seq, past * N_HEADS, HEAD_DIM)
    cache_v = cache_v.reshape(n_seq, past * N_HEADS, HEAD_DIM)
    cache = pl.BlockSpec((None, past * N_HEADS, HEAD_DIM), lambda b: (b, 0, 0))
    return pl.pallas_call(
        kern,
        grid=(n_seq,),
        in_specs=[new, new, new, cache, cache,
                  pl.BlockSpec((N_HEADS, c_all.shape[1]), lambda b: (b, 0))],
        out_specs=new,
        out_shape=jax.ShapeDtypeStruct((n_seq * s_new, D_ATTN), F32),
        compiler_params=_params(("arbitrary",)),
        name="attn_sample",
    )(qb, kb, vb, cache_k, cache_v, c_all)


def _gelu_tanh(x):
    return 0.5 * x * (1.0 + jnp.tanh(math.sqrt(2.0 / math.pi) * (x + 0.044715 * (x * x * x))))


def _mix_kernel(x_ref, y_ref, a_ref, unperm_ref, wglu_ref, gs_ref, ga_ref, wout_ref, gm_ref,
                x1_ref, hm_ref):
    gy = _gelu_tanh(y_ref[...])
    gate = jax.nn.sigmoid(jnp.dot(gy.astype(BF16), wglu_ref[...], preferred_element_type=F32))
    ssm = _rms(gy * gate, gs_ref[...]).astype(BF16)
    ssm = jnp.dot(unperm_ref[...], ssm, preferred_element_type=F32).astype(BF16)
    att = _rms(a_ref[...], ga_ref[...]).astype(BF16)
    x1 = (x_ref[...]
          + jnp.dot(ssm, wout_ref[0:D_SSM, :], preferred_element_type=F32)
          + jnp.dot(att, wout_ref[D_SSM:D_SSM + D_ATTN, :], preferred_element_type=F32))
    x1_ref[...] = x1
    hm_ref[...] = _rms(x1, gm_ref[...]).astype(BF16)


def _mix(x, y, attn, unperm, w_glu, g_ssm, g_attn, w_out, g_mlp, *, tt):
    t = x.shape[0]
    assert t % tt == 0 and unperm.shape == (tt, tt)
    row = lambda i: (i, 0)
    const = lambda i: (0, 0)
    once = pl.Buffered(1)
    return pl.pallas_call(
        _mix_kernel,
        grid=(t // tt,),
        in_specs=[
            pl.BlockSpec((tt, D_MODEL), row),
            pl.BlockSpec((tt, D_SSM), row),
            pl.BlockSpec((tt, D_ATTN), row),
            pl.BlockSpec((tt, tt), const, pipeline_mode=once),
            pl.BlockSpec((D_SSM, D_SSM), const, pipeline_mode=once),
            pl.BlockSpec((1, D_SSM), const),
            pl.BlockSpec((1, D_ATTN), const),
            pl.BlockSpec((D_SSM + D_ATTN, D_MODEL), const, pipeline_mode=once),
            pl.BlockSpec((1, D_MODEL), const),
        ],
        out_specs=[pl.BlockSpec((tt, D_MODEL), row), pl.BlockSpec((tt, D_MODEL), row)],
        out_shape=[jax.ShapeDtypeStruct((t, D_MODEL), F32), jax.ShapeDtypeStruct((t, D_MODEL), BF16)],
        compiler_params=_params(("arbitrary",)),
        name="mix",
    )(x, y, attn, unperm, w_glu, g_ssm, g_attn, w_out, g_mlp)


def _mlp_kernel(x1_ref, hm_ref, wup_ref, wdn_ref, o_ref):
    j = pl.program_id(1)

    @pl.when(j == 0)
    def _():
        o_ref[...] = x1_ref[...]

    a = jnp.maximum(jnp.dot(hm_ref[...], wup_ref[...], preferred_element_type=F32), 0.0)
    o_ref[...] += jnp.dot((a * a).astype(BF16), wdn_ref[...], preferred_element_type=F32)


def _mlp(x1, hm, w_up, w_down, *, tt, tf):
    t = x1.shape[0]
    assert t % tt == 0 and D_FF % tf == 0
    return pl.pallas_call(
        _mlp_kernel,
        grid=(t // tt, D_FF // tf),
        in_specs=[
            pl.BlockSpec((tt, D_MODEL), lambda i, j: (i, 0)),
            pl.BlockSpec((tt, D_MODEL), lambda i, j: (i, 0)),
            pl.BlockSpec((D_MODEL, tf), lambda i, j: (0, j)),
            pl.BlockSpec((tf, D_MODEL), lambda i, j: (j, 0)),
        ],
        out_specs=pl.BlockSpec((tt, D_MODEL), lambda i, j: (i, 0)),
        out_shape=jax.ShapeDtypeStruct((t, D_MODEL), F32),
        compiler_params=_params(("arbitrary", "arbitrary")),
        name="mlp",
    )(x1, hm, w_up, w_down)


def _layer(x, weights, *, n_seq, seq_len, cache=None, h0=None, tiles):
    (g_mix, w_main, w_f, b_f, g_q, g_k, s5_raw, w_glu, g_ssm, g_attn, w_out, g_mlp,
     w_up, w_down) = weights
    t = n_seq * seq_len
    x2 = x.reshape(t, D_MODEL)
    u, qb, k, kb, v, vb, lf = _in_proj(x2, g_mix, w_main, w_f, b_f, g_q, g_k, tt=tiles["proj"])
    logf = lf[:, :N_HEADS]
    lf_rows = logf.reshape(n_seq, seq_len, N_HEADS).transpose(0, 2, 1).reshape(n_seq * N_HEADS, seq_len)

    tc = tiles["chunk"] if cache is None else seq_len
    perm = _step_major_perm(tiles["s5"], tc)
    if cache is None:
        tables = _s5_tables(*s5_raw, tc=tc)
        y, hfin = _s5(u, tables, perm, None, tt=tiles["s5"], tc=tc, seq_len=seq_len,
                      carry_mode=True)
        c_rows = _cumsum_lanes(lf_rows)
        attn = _attn_prompt(qb, kb, vb, c_rows.reshape(n_seq * N_HEADS, 1, seq_len),
                            n_seq=n_seq, seq_len=seq_len, bq=tiles["bq"])
    else:
        cache_k, cache_v, cache_logf = cache
        past = cache_k.shape[1]
        tables = _s5_tables(*s5_raw, tc=tc)
        y, hfin = _s5(u, tables, perm, h0, tt=tiles["s5"], tc=tc, seq_len=seq_len,
                      carry_mode=False)
        past_rows = cache_logf.transpose(0, 2, 1).reshape(n_seq * N_HEADS, past)
        total = past + seq_len
        padded = -(-total // LANES) * LANES
        lf_all = jnp.concatenate(
            [past_rows, lf_rows, jnp.zeros((n_seq * N_HEADS, padded - total), F32)], axis=1)
        c_all = _cumsum_lanes(lf_all)
        attn = _attn_sample(qb, kb, vb, cache_k, cache_v, c_all,
                            n_seq=n_seq, s_new=seq_len, past=past)

    x1, hm = _mix(x2, y, attn, perm.T, w_glu, g_ssm, g_attn, w_out, g_mlp, tt=tiles["s5"])
    out = _mlp(x1, hm, w_up, w_down, tt=tiles["mlp"], tf=tiles["tf"])
    h_re, h_im = _state_to_gp(hfin)
    return (out.reshape(n_seq, seq_len, D_MODEL),
            k.reshape(n_seq, seq_len, N_HEADS, HEAD_DIM),
            v.reshape(n_seq, seq_len, N_HEADS, HEAD_DIM),
            logf.reshape(n_seq, seq_len, N_HEADS), h_re, h_im)


def _prep_weights(l, g_norm_mix, w_in, b_f, ssm_a_re, ssm_a_im, ssm_log_step, ssm_b_re, ssm_b_im,
                  ssm_c_re, ssm_c_im, ssm_d, w_glu, g_q, g_k, g_out_ssm, g_out_attn, w_out,
                  g_norm_mlp, w_up, w_down):
    n_main = D_SSM + 3 * D_ATTN
    w = w_in[l]
    w_main = w[:, :n_main].astype(BF16)
    w_f = jnp.pad(w[:, n_main:], ((0, 0), (0, LANES - N_HEADS))).astype(BF16)
    b = jnp.pad(b_f[l], (0, LANES - N_HEADS)).reshape(1, LANES)
    s5_raw = (ssm_a_re[l], ssm_a_im[l], ssm_log_step[l], ssm_b_re[l], ssm_b_im[l],
              ssm_c_re[l], ssm_c_im[l], ssm_d[l])
    return (g_norm_mix[l].reshape(1, D_MODEL), w_main, w_f, b,
            g_q[l].reshape(1, HEAD_DIM), g_k[l].reshape(1, HEAD_DIM), s5_raw,
            w_glu[l].astype(BF16), g_out_ssm[l].reshape(1, D_SSM), g_out_attn[l].reshape(1, D_ATTN),
            w_out[l].astype(BF16), g_norm_mlp[l].reshape(1, D_MODEL),
            w_up[l].astype(BF16), w_down[l].astype(BF16))


PROMPT_TILES = dict(proj=512, s5=512, chunk=64, bq=512, mlp=512, tf=1024)
SAMPLE_TILES = dict(proj=512, s5=512, mlp=512, tf=1024)


def kernel(x_prompt, x_sample, cache_k, cache_v, cache_logf, state_ssm_re, state_ssm_im,
           g_norm_mix, w_in, b_f, ssm_a_re, ssm_a_im, ssm_log_step, ssm_b_re, ssm_b_im,
           ssm_c_re, ssm_c_im, ssm_d, w_glu, g_q, g_k, g_out_ssm, g_out_attn, w_out,
           g_norm_mlp, w_up, w_down):
    depth = w_in.shape[0]
    y_p, y_s = x_prompt, x_sample
    outs_p, outs_s = [], []
    for l in range(depth):
        weights = _prep_weights(l, g_norm_mix, w_in, b_f, ssm_a_re, ssm_a_im, ssm_log_step,
                                ssm_b_re, ssm_b_im, ssm_c_re, ssm_c_im, ssm_d, w_glu, g_q, g_k,
                                g_out_ssm, g_out_attn, w_out, g_norm_mlp, w_up, w_down)
        n_p, l_p = y_p.shape[0], y_p.shape[1]
        y_p, *rest_p = _layer(y_p, weights, n_seq=n_p, seq_len=l_p, tiles=PROMPT_TILES)
        n_s, l_s = y_s.shape[0], y_s.shape[1]
        h0 = _state_from_gp(state_ssm_re[l], state_ssm_im[l])
        y_s, *rest_s = _layer(y_s, weights, n_seq=n_s, seq_len=l_s,
                              cache=(cache_k[l], cache_v[l], cache_logf[l]), h0=h0,
                              tiles=SAMPLE_TILES)
        outs_p.append(rest_p)
        outs_s.append(rest_s)
    stack = lambda outs, idx: jnp.stack([o[idx] for o in outs])
    return (y_p, y_s,
            stack(outs_p, 0), stack(outs_p, 1), stack(outs_p, 2), stack(outs_p, 3), stack(outs_p, 4),
            stack(outs_s, 0), stack(outs_s, 1), stack(outs_s, 2), stack(outs_s, 3), stack(outs_s, 4))
```

```python
import functools
import math

import jax
import jax.numpy as jnp
from jax import lax
from jax.experimental import pallas as pl
from jax.experimental.pallas import tpu as pltpu

D_MODEL = 2048
D_SSM = 1024
SSM_GROUP = 16
N_SSM_GROUPS = 64
SSM_STATE = 64
D_ATTN = 1024
HEAD_DIM = 128
N_HEADS = 8
D_FF = 8192
EPS = 1e-6
NEG_INF = -1e30
LOG2E = math.log2(math.e)
MAX_FIXED_SHIFT = 45.0

LANES = 128
SLAB = 256
N_SLABS = D_SSM // SLAB
SLAB_GROUPS = SLAB // SSM_GROUP
SLAB_STATE = SLAB_GROUPS * SSM_STATE
VMEM_LIMIT = 56 * 1024 * 1024

F32 = jnp.float32
BF16 = jnp.bfloat16


def _params(sem, vmem=VMEM_LIMIT):
    return pltpu.CompilerParams(dimension_semantics=sem, vmem_limit_bytes=vmem)


def _rms(x, g):
    return x * lax.rsqrt(jnp.mean(x * x, axis=-1, keepdims=True) + EPS) * g


def _in_proj_kernel(x_ref, g_ref, w_ref, wf_ref, bf_ref, gq_ref, gk_ref,
                    u_ref, qb_ref, k_ref, kb_ref, v_ref, vb_ref, lf_ref):
    hb = _rms(x_ref[...], g_ref[...]).astype(BF16)

    u_ref[...] = jnp.dot(hb, w_ref[:, 0:D_SSM], preferred_element_type=F32)

    q = jnp.dot(hb, w_ref[:, D_SSM:D_SSM + D_ATTN], preferred_element_type=F32)
    gq = gq_ref[...]
    for h in range(N_HEADS):
        sl = slice(h * HEAD_DIM, (h + 1) * HEAD_DIM)
        qb_ref[:, sl] = (_rms(q[:, sl], gq) * (HEAD_DIM ** -0.5 * LOG2E)).astype(BF16)

    k = jnp.dot(hb, w_ref[:, D_SSM + D_ATTN:D_SSM + 2 * D_ATTN], preferred_element_type=F32)
    gk = gk_ref[...]
    for h in range(N_HEADS):
        sl = slice(h * HEAD_DIM, (h + 1) * HEAD_DIM)
        kn = _rms(k[:, sl], gk)
        k_ref[:, sl] = kn
        kb_ref[:, sl] = kn.astype(BF16)

    v = jnp.dot(hb, w_ref[:, D_SSM + 2 * D_ATTN:D_SSM + 3 * D_ATTN], preferred_element_type=F32)
    v_ref[...] = v
    vb_ref[...] = v.astype(BF16)

    zf = jnp.dot(hb, wf_ref[...], preferred_element_type=F32) + bf_ref[...]
    lf_ref[...] = jnp.minimum(zf, 0.0) - jnp.log1p(jnp.exp(-jnp.abs(zf)))


def _in_proj(x, g, w_main, w_f, b_f, g_q, g_k, *, tt):
    t = x.shape[0]
    assert t % tt == 0
    row = lambda i: (i, 0)
    const = lambda i: (0, 0)
    once = pl.Buffered(1)
    wide = lambda dt: jax.ShapeDtypeStruct((t, D_ATTN), dt)
    return pl.pallas_call(
        _in_proj_kernel,
        grid=(t // tt,),
        in_specs=[
            pl.BlockSpec((tt, D_MODEL), row),
            pl.BlockSpec((1, D_MODEL), const),
            pl.BlockSpec((D_MODEL, D_SSM + 3 * D_ATTN), const, pipeline_mode=once),
            pl.BlockSpec((D_MODEL, LANES), const, pipeline_mode=once),
            pl.BlockSpec((1, LANES), const),
            pl.BlockSpec((1, HEAD_DIM), const),
            pl.BlockSpec((1, HEAD_DIM), const),
        ],
        out_specs=[
            pl.BlockSpec((tt, D_SSM), row),
            pl.BlockSpec((tt, D_ATTN), row),
            pl.BlockSpec((tt, D_ATTN), row),
            pl.BlockSpec((tt, D_ATTN), row),
            pl.BlockSpec((tt, D_ATTN), row),
            pl.BlockSpec((tt, D_ATTN), row),
            pl.BlockSpec((tt, LANES), row),
        ],
        out_shape=[wide(F32), wide(BF16), wide(F32), wide(BF16), wide(F32), wide(BF16),
                   jax.ShapeDtypeStruct((t, LANES), F32)],
        compiler_params=_params(("arbitrary",)),
        name="in_proj",
    )(x, g, w_main, w_f, b_f, g_q, g_k)


def _cumsum_kernel(x_ref, o_ref):
    rows, length = x_ref.shape
    lane = lax.broadcasted_iota(jnp.int32, (rows, LANES), 1)
    carry = jnp.zeros((rows, 1), F32)
    for b in range(length // LANES):
        sl = slice(b * LANES, (b + 1) * LANES)
        x = x_ref[:, sl]
        d = 1
        while d < LANES:
            x = x + jnp.where(lane >= d, pltpu.roll(x, d, axis=1), 0.0)
            d *= 2
        x = x + carry
        o_ref[:, sl] = x * LOG2E
        carry = x[:, LANES - 1:LANES]


def _cumsum_lanes(x):
    rows, length = x.shape
    assert length % LANES == 0 and rows % 8 == 0
    return pl.pallas_call(
        _cumsum_kernel,
        out_shape=jax.ShapeDtypeStruct((rows, length), F32),
        name="cumsum",
    )(x)


def _s5_tables(a_re, a_im, log_step, b_re, b_im, c_re, c_im, d, tc):
    step = jnp.exp(log_step)[:, None]
    mag = jnp.exp(a_re * step)
    abar_re = mag * jnp.cos(a_im * step)
    abar_im = mag * jnp.sin(a_im * step)
    den = a_re * a_re + a_im * a_im
    nr = abar_re - 1.0
    ni = abar_im
    fr = (nr * a_re + ni * a_im) / den
    fi = (ni * a_re - nr * a_im) / den
    bbar_re = fr[..., None] * b_re - fi[..., None] * b_im
    bbar_im = fr[..., None] * b_im + fi[..., None] * b_re
    eye = jnp.eye(SLAB_GROUPS, dtype=F32)

    def in_blockdiag(b):
        b = b.reshape(N_SLABS, SLAB_GROUPS, SSM_STATE, SSM_GROUP)
        return jnp.einsum('sgph,gk->sghkp', b, eye).reshape(N_SLABS, SLAB, SLAB_STATE)

    def out_blockdiag(c):
        c = c.reshape(N_SLABS, SLAB_GROUPS, SSM_GROUP, SSM_STATE)
        return jnp.einsum('sghp,gk->sgpkh', c, eye).reshape(N_SLABS, SLAB_STATE, SLAB)

    bb = jnp.concatenate([in_blockdiag(bbar_re), in_blockdiag(bbar_im)], axis=-1).astype(BF16)
    cc = jnp.concatenate([out_blockdiag(c_re), out_blockdiag(-c_im)], axis=1).astype(BF16)
    n = jnp.arange(1, tc + 1, dtype=F32)[:, None, None]
    pmag = jnp.exp(a_re[None] * step[None] * n)
    ang = a_im[None] * step[None] * n
    p_re = (pmag * jnp.cos(ang)).reshape(tc, N_SLABS, SLAB_STATE).transpose(1, 0, 2)
    p_im = (pmag * jnp.sin(ang)).reshape(tc, N_SLABS, SLAB_STATE).transpose(1, 0, 2)
    pw = jnp.concatenate([p_re, p_im], axis=-1)
    dd = d.reshape(N_SLABS, 1, SLAB)
    return bb, cc, pw, dd


def _s5_kernel(u_ref, perm_ref, bb_ref, cc_ref, pw_ref, d_ref, h0_ref, y_ref, hfin_ref,
               st_s, hb_s, g_s, carry_s, *, tc, nc, tiles_per_seq, carry_mode):
    i = pl.program_id(1)
    u = u_ref[...]
    u_hi = u.astype(BF16)
    u_lo = (u - u_hi.astype(F32)).astype(BF16)
    up = jnp.dot(perm_ref[...], jnp.concatenate([u_hi, u_lo], axis=1),
                 preferred_element_type=F32)
    u_hi = up[:, :SLAB]
    u = u_hi + up[:, SLAB:]
    uh = u_hi.astype(BF16)
    y_ref[...] = d_ref[...] * u
    blk = 2 * LANES
    nblk = SLAB_STATE // blk
    grp = max(1, 16 // nc)
    lanes = lambda b: slice(b * blk, (b + 1) * blk)
    step_rows = lambda t: slice(t * nc, (t + 1) * nc)

    if carry_mode:
        @pl.when(i % tiles_per_seq == 0)
        def _():
            carry_s[...] = jnp.zeros_like(carry_s)
    else:
        g_s[...] = h0_ref[...]

    for b in range(nblk):
        re, im = lanes(b), lanes(nblk + b)
        st_s[:, re] = jnp.dot(uh, bb_ref[:, re], preferred_element_type=F32)
        st_s[:, im] = jnp.dot(uh, bb_ref[:, im], preferred_element_type=F32)
        ar = pw_ref[0:1, re]
        ai = pw_ref[0:1, im]
        hr = jnp.zeros((nc, blk), F32)
        hi = jnp.zeros((nc, blk), F32)
        for t in range(tc):
            rows = step_rows(t)
            hr, hi = (ar * hr - ai * hi + st_s[rows, re],
                      ar * hi + ai * hr + st_s[rows, im])
            st_s[rows, re] = hr
            st_s[rows, im] = hi

        ar = pw_ref[tc - 1:tc, re]
        ai = pw_ref[tc - 1:tc, im]
        if carry_mode:
            gr = carry_s[0:1, re]
            gi = carry_s[0:1, im]
            for c in range(nc):
                g_s[c:c + 1, re] = gr
                g_s[c:c + 1, im] = gi
                gr, gi = (ar * gr - ai * gi + hr[c:c + 1, :],
                          ar * gi + ai * gr + hi[c:c + 1, :])
            carry_s[0:1, re] = gr
            carry_s[0:1, im] = gi
            hfin_ref[0:1, re] = gr
            hfin_ref[0:1, im] = gi
            gr = g_s[:, re]
            gi = g_s[:, im]
        else:
            gr = g_s[:, re]
            gi = g_s[:, im]
            hfin_ref[:, re] = ar * gr - ai * gi + hr
            hfin_ref[:, im] = ar * gi + ai * gr + hi

        for t0 in range(0, tc, grp):
            full_r, full_i = [], []
            for t in range(t0, t0 + grp):
                rows = step_rows(t)
                pr = pw_ref[t:t + 1, re]
                pi = pw_ref[t:t + 1, im]
                full_r.append(st_s[rows, re] + (pr * gr - pi * gi))
                full_i.append(st_s[rows, im] + (pr * gi + pi * gr))
            rows = slice(t0 * nc, (t0 + grp) * nc)
            hb_s[rows, re] = jnp.concatenate(full_r, axis=0).astype(BF16)
            hb_s[rows, im] = jnp.concatenate(full_i, axis=0).astype(BF16)

        y_ref[...] += (jnp.dot(hb_s[:, re], cc_ref[re, :], preferred_element_type=F32)
                       + jnp.dot(hb_s[:, im], cc_ref[im, :], preferred_element_type=F32))


def _step_major_perm(tt, tc):
    nc = tt // tc
    r_out = jnp.arange(tt)
    r_in = (r_out % nc) * tc + r_out // nc
    return (r_in[:, None] == jnp.arange(tt)[None, :]).astype(BF16)


def _s5(u, tables, perm, h0, *, tt, tc, seq_len, carry_mode):
    bb, cc, pw, dd = tables
    t = u.shape[0]
    nc = tt // tc
    n_tiles = t // tt
    assert t % tt == 0 and tt % tc == 0
    if carry_mode:
        assert seq_len % tt == 0
        tiles_per_seq = seq_len // tt
        n_seq = t // seq_len
        hfin_shape = (N_SLABS, n_seq, 1, 2 * SLAB_STATE)
        hfin_spec = pl.BlockSpec((None, None, 1, 2 * SLAB_STATE),
                                 lambda s, i: (s, i // tiles_per_seq, 0, 0))
        h0 = jnp.zeros((N_SLABS, 8, 2 * SLAB_STATE), F32)
        h0_spec = pl.BlockSpec((None, 8, 2 * SLAB_STATE), lambda s, i: (s, 0, 0))
    else:
        assert seq_len == tc
        tiles_per_seq = 1
        hfin_shape = (N_SLABS, n_tiles, nc, 2 * SLAB_STATE)
        hfin_spec = pl.BlockSpec((None, None, nc, 2 * SLAB_STATE), lambda s, i: (s, i, 0, 0))
        h0_spec = pl.BlockSpec((None, nc, 2 * SLAB_STATE), lambda s, i: (s, i, 0))
    kern = functools.partial(_s5_kernel, tc=tc, nc=nc, tiles_per_seq=tiles_per_seq,
                             carry_mode=carry_mode)
    y, hfin = pl.pallas_call(
        kern,
        grid=(N_SLABS, n_tiles),
        in_specs=[
            pl.BlockSpec((tt, SLAB), lambda s, i: (i, s)),
            pl.BlockSpec((tt, tt), lambda s, i: (0, 0), pipeline_mode=pl.Buffered(1)),
            pl.BlockSpec((None, SLAB, 2 * SLAB_STATE), lambda s, i: (s, 0, 0)),
            pl.BlockSpec((None, 2 * SLAB_STATE, SLAB), lambda s, i: (s, 0, 0)),
            pl.BlockSpec((None, tc, 2 * SLAB_STATE), lambda s, i: (s, 0, 0)),
            pl.BlockSpec((None, 1, SLAB), lambda s, i: (s, 0, 0)),
            h0_spec,
        ],
        out_specs=[pl.BlockSpec((tt, SLAB), lambda s, i: (i, s)), hfin_spec],
        out_shape=[jax.ShapeDtypeStruct((t, D_SSM), F32),
                   jax.ShapeDtypeStruct(hfin_shape, F32)],
        scratch_shapes=[pltpu.VMEM((tt, 2 * SLAB_STATE), F32),
                        pltpu.VMEM((tt, 2 * SLAB_STATE), BF16),
                        pltpu.VMEM((nc, 2 * SLAB_STATE), F32),
                        pltpu.VMEM((8, 2 * SLAB_STATE), F32)],
        compiler_params=_params(("arbitrary", "arbitrary")),
        name="s5",
    )(u, perm, bb, cc, pw, dd, h0)
    return y, hfin.reshape(N_SLABS, -1, 2 * SLAB_STATE)


def _state_to_gp(hfin):
    n = hfin.shape[1]
    h = hfin.transpose(1, 0, 2)
    re = h[..., :SLAB_STATE].reshape(n, N_SSM_GROUPS, SSM_STATE)
    im = h[..., SLAB_STATE:].reshape(n, N_SSM_GROUPS, SSM_STATE)
    return re, im


def _state_from_gp(re, im):
    n = re.shape[0]
    h = jnp.concatenate([re.reshape(n, N_SLABS, SLAB_STATE), im.reshape(n, N_SLABS, SLAB_STATE)],
                        axis=-1)
    return h.transpose(1, 0, 2)


def _col_from_row(row):
    n = row.shape[1]
    r = lax.broadcasted_iota(jnp.int32, (n, n), 0)
    c = lax.broadcasted_iota(jnp.int32, (n, n), 1)
    return jnp.sum(jnp.where(r == c, jnp.broadcast_to(row, (n, n)), 0.0), axis=1, keepdims=True)


def _attn_prompt_kernel(q_ref, k_ref, v_ref, cq_ref, ck_ref, o_ref, m_s, l_s, acc_s, cq_s, s_s,
                        *, bq, online):
    i = pl.program_id(1)
    bk = bq // 2
    q = q_ref[...]
    cq_s[...] = jnp.broadcast_to(_col_from_row(cq_ref[...]), (bq, LANES))
    if online:
        m_s[...] = jnp.full_like(m_s, NEG_INF)
    l_s[...] = jnp.zeros_like(l_s)
    acc_s[...] = jnp.zeros_like(acc_s)
    nchunk = bk // LANES

    def scores(j, slot):
        start = pl.multiple_of(j * bk, bk)
        s_s[slot] = lax.dot_general(q, k_ref[pl.ds(start, bk), :], (((1,), (1,)), ((), ())),
                                    preferred_element_type=F32)

    def reduce_block(j, slot, diag_offset=None):
        start = pl.multiple_of(j * bk, bk)
        v = v_ref[pl.ds(start, bk), :]

        def logits(c):
            col0 = pl.multiple_of(start + c * LANES, LANES)
            t = s_s[slot, :, c * LANES:(c + 1) * LANES] - ck_ref[:, pl.ds(col0, LANES)]
            if diag_offset is not None:
                r = lax.broadcasted_iota(jnp.int32, (bq, LANES), 0)
                col = lax.broadcasted_iota(jnp.int32, (bq, LANES), 1) + (diag_offset + c * LANES)
                t = jnp.where(col <= r, t, NEG_INF)
            return t

        if online:
            m_loc = functools.reduce(jnp.maximum, [logits(c) for c in range(nchunk)])
            m_old = m_s[...]
            m_new = jnp.maximum(m_old, jnp.max(m_loc, axis=1, keepdims=True) + cq_s[...])
            m_s[...] = m_new
            alpha = jnp.exp2(m_old - m_new)
            shift = cq_s[...] - m_new
        else:
            shift = cq_s[...]
        ps = [jnp.exp2(logits(c) + shift) for c in range(nchunk)]
        p = jnp.concatenate([pc.astype(BF16) for pc in ps], axis=1)
        l_new = functools.reduce(jnp.add, ps)
        acc_new = jnp.dot(p, v, preferred_element_type=F32)
        if online:
            l_s[...] = alpha * l_s[...] + l_new
            acc_s[...] = alpha * acc_s[...] + acc_new
        else:
            l_s[...] += l_new
            acc_s[...] += acc_new

    scores(0, 0)

    def body(p, carry):
        j = 2 * p
        scores(j + 1, 1)
        reduce_block(j, 0)
        scores(j + 2, 0)
        reduce_block(j + 1, 1)
        return carry

    lax.fori_loop(0, i, body, 0)
    scores(2 * i + 1, 1)
    reduce_block(2 * i, 0, diag_offset=0)
    reduce_block(2 * i + 1, 1, diag_offset=bk)

    o_ref[...] = acc_s[...] / jnp.sum(l_s[...], axis=1, keepdims=True)


def _attn_prompt(qb, kb, vb, cq_rows, ck_rows, *, n_seq, seq_len, bq, online):
    nq = seq_len // bq
    assert seq_len % bq == 0
    kern = functools.partial(_attn_prompt_kernel, bq=bq, online=online)
    seq_blocks = seq_len // bq
    return pl.pallas_call(
        kern,
        grid=(n_seq * N_HEADS, nq),
        in_specs=[
            pl.BlockSpec((bq, HEAD_DIM), lambda g, i: ((g // N_HEADS) * seq_blocks + i, g % N_HEADS)),
            pl.BlockSpec((seq_len, HEAD_DIM), lambda g, i: (g // N_HEADS, g % N_HEADS)),
            pl.BlockSpec((seq_len, HEAD_DIM), lambda g, i: (g // N_HEADS, g % N_HEADS)),
            pl.BlockSpec((None, 1, bq), lambda g, i: (g, 0, i)),
            pl.BlockSpec((None, 1, seq_len), lambda g, i: (g, 0, 0)),
        ],
        out_specs=pl.BlockSpec((bq, HEAD_DIM),
                               lambda g, i: ((g // N_HEADS) * seq_blocks + i, g % N_HEADS)),
        out_shape=jax.ShapeDtypeStruct((n_seq * seq_len, D_ATTN), F32),
        scratch_shapes=[pltpu.VMEM((bq, LANES), F32), pltpu.VMEM((bq, LANES), F32),
                        pltpu.VMEM((bq, HEAD_DIM), F32), pltpu.VMEM((bq, LANES), F32),
                        pltpu.VMEM((2, bq, bq // 2), F32)],
        compiler_params=_params(("arbitrary", "arbitrary")),
        name="attn_prompt",
    )(qb, kb, vb, cq_rows, ck_rows)


def _attn_sample_kernel(q_ref, kn_ref, vn_ref, ck_ref, cv_ref, c_ref, o_ref, *, past, s_new):
    r = lax.broadcasted_iota(jnp.int32, (s_new, s_new), 0)
    col = lax.broadcasted_iota(jnp.int32, (s_new, s_new), 1)
    causal = col <= r
    nt = (((1,), (1,)), ((), ()))
    for h in range(N_HEADS):
        sl = slice(h * HEAD_DIM, (h + 1) * HEAD_DIM)
        q = q_ref[:, sl]
        c_past = c_ref[h:h + 1, 0:past]
        c_new = c_ref[h:h + 1, past:past + s_new]
        cq = _col_from_row(c_new)
        head_rows = pl.ds(h, past, stride=N_HEADS)
        s_p = lax.dot_general(q, ck_ref[head_rows, :].astype(BF16), nt, preferred_element_type=F32)
        s_p = s_p + cq - c_past
        s_n = lax.dot_general(q, kn_ref[:, sl], nt, preferred_element_type=F32)
        s_n = jnp.where(causal, s_n + cq - c_new, NEG_INF)
        m = jnp.maximum(jnp.max(s_p, axis=1, keepdims=True), jnp.max(s_n, axis=1, keepdims=True))
        p_p = jnp.exp2(s_p - m)
        p_n = jnp.exp2(s_n - m)
        l = jnp.sum(p_p, axis=1, keepdims=True) + jnp.sum(p_n, axis=1, keepdims=True)
        acc = (jnp.dot(p_p.astype(BF16), cv_ref[head_rows, :].astype(BF16), preferred_element_type=F32)
               + jnp.dot(p_n.astype(BF16), vn_ref[:, sl], preferred_element_type=F32))
        o_ref[:, sl] = acc / l


def _attn_sample(qb, kb, vb, cache_k, cache_v, c_all, *, n_seq, s_new, past):
    kern = functools.partial(_attn_sample_kernel, past=past, s_new=s_new)
    new = pl.BlockSpec((s_new, D_ATTN), lambda b: (b, 0))
    cache_k = cache_k.reshape(n_seq, past * N_HEADS, HEAD_DIM)
    cache_v = cache_v.reshape(n_seq, past * N_HEADS, HEAD_DIM)
    cache = pl.BlockSpec((None, past * N_HEADS, HEAD_DIM), lambda b: (b, 0, 0))
    return pl.pallas_call(
        kern,
        grid=(n_seq,),
        in_specs=[new, new, new, cache, cache,
                  pl.BlockSpec((N_HEADS, c_all.shape[1]), lambda b: (b, 0))],
        out_specs=new,
        out_shape=jax.ShapeDtypeStruct((n_seq * s_new, D_ATTN), F32),
        compiler_params=_params(("arbitrary",)),
        name="attn_sample",
    )(qb, kb, vb, cache_k, cache_v, c_all)


def _gelu_tanh(x):
    return 0.5 * x * (1.0 + jnp.tanh(math.sqrt(2.0 / math.pi) * (x + 0.044715 * (x * x * x))))


def _mix_kernel(x_ref, y_ref, a_ref, unperm_ref, wglu_ref, gs_ref, ga_ref, wout_ref, gm_ref,
                x1_ref, hm_ref):
    gy = _gelu_tanh(y_ref[...])
    gate = jax.nn.sigmoid(jnp.dot(gy.astype(BF16), wglu_ref[...], preferred_element_type=F32))
    ssm = _rms(gy * gate, gs_ref[...]).astype(BF16)
    ssm = jnp.dot(unperm_ref[...], ssm, preferred_element_type=F32).astype(BF16)
    att = _rms(a_ref[...], ga_ref[...]).astype(BF16)
    x1 = (x_ref[...]
          + jnp.dot(ssm, wout_ref[0:D_SSM, :], preferred_element_type=F32)
          + jnp.dot(att, wout_ref[D_SSM:D_SSM + D_ATTN, :], preferred_element_type=F32))
    x1_ref[...] = x1
    hm_ref[...] = _rms(x1, gm_ref[...]).astype(BF16)


def _mix(x, y, attn, unperm, w_glu, g_ssm, g_attn, w_out, g_mlp, *, tt):
    t = x.shape[0]
    assert t % tt == 0 and unperm.shape == (tt, tt)
    row = lambda i: (i, 0)
    const = lambda i: (0, 0)
    once = pl.Buffered(1)
    return pl.pallas_call(
        _mix_kernel,
        grid=(t // tt,),
        in_specs=[
            pl.BlockSpec((tt, D_MODEL), row),
            pl.BlockSpec((tt, D_SSM), row),
            pl.BlockSpec((tt, D_ATTN), row),
            pl.BlockSpec((tt, tt), const, pipeline_mode=once),
            pl.BlockSpec((D_SSM, D_SSM), const, pipeline_mode=once),
            pl.BlockSpec((1, D_SSM), const),
            pl.BlockSpec((1, D_ATTN), const),
            pl.BlockSpec((D_SSM + D_ATTN, D_MODEL), const, pipeline_mode=once),
            pl.BlockSpec((1, D_MODEL), const),
        ],
        out_specs=[pl.BlockSpec((tt, D_MODEL), row), pl.BlockSpec((tt, D_MODEL), row)],
        out_shape=[jax.ShapeDtypeStruct((t, D_MODEL), F32), jax.ShapeDtypeStruct((t, D_MODEL), BF16)],
        compiler_params=_params(("arbitrary",)),
        name="mix",
    )(x, y, attn, unperm, w_glu, g_ssm, g_attn, w_out, g_mlp)


def _mlp_kernel(x1_ref, hm_ref, wup_ref, wdn_ref, o_ref):
    j = pl.program_id(1)

    @pl.when(j == 0)
    def _():
        o_ref[...] = x1_ref[...]

    a = jnp.maximum(jnp.dot(hm_ref[...], wup_ref[...], preferred_element_type=F32), 0.0)
    o_ref[...] += jnp.dot((a * a).astype(BF16), wdn_ref[...], preferred_element_type=F32)


def _mlp(x1, hm, w_up, w_down, *, tt, tf):
    t = x1.shape[0]
    assert t % tt == 0 and D_FF % tf == 0
    return pl.pallas_call(
        _mlp_kernel,
        grid=(t // tt, D_FF // tf),
        in_specs=[
            pl.BlockSpec((tt, D_MODEL), lambda i, j: (i, 0)),
            pl.BlockSpec((tt, D_MODEL), lambda i, j: (i, 0)),
            pl.BlockSpec((D_MODEL, tf), lambda i, j: (0, j)),
            pl.BlockSpec((tf, D_MODEL), lambda i, j: (j, 0)),
        ],
        out_specs=pl.BlockSpec((tt, D_MODEL), lambda i, j: (i, 0)),
        out_shape=jax.ShapeDtypeStruct((t, D_MODEL), F32),
        compiler_params=_params(("arbitrary", "arbitrary")),
        name="mlp",
    )(x1, hm, w_up, w_down)


def _layer(x, weights, *, n_seq, seq_len, cache=None, h0=None, tiles):
    (g_mix, w_main, w_f, b_f, g_q, g_k, s5_raw, w_glu, g_ssm, g_attn, w_out, g_mlp,
     w_up, w_down) = weights
    t = n_seq * seq_len
    x2 = x.reshape(t, D_MODEL)
    u, qb, k, kb, v, vb, lf = _in_proj(x2, g_mix, w_main, w_f, b_f, g_q, g_k, tt=tiles["proj"])
    logf = lf[:, :N_HEADS]
    lf_rows = logf.reshape(n_seq, seq_len, N_HEADS).transpose(0, 2, 1).reshape(n_seq * N_HEADS, seq_len)

    tc = tiles["chunk"] if cache is None else seq_len
    perm = _step_major_perm(tiles["s5"], tc)
    if cache is None:
        tables = _s5_tables(*s5_raw, tc=tc)
        y, hfin = _s5(u, tables, perm, None, tt=tiles["s5"], tc=tc, seq_len=seq_len,
                      carry_mode=True)
        c_rows = _cumsum_lanes(lf_rows).reshape(n_seq * N_HEADS, 1, seq_len)
        attend = functools.partial(_attn_prompt, qb, kb, vb, n_seq=n_seq, seq_len=seq_len,
                                   bq=tiles["bq"])
        bound = (HEAD_DIM ** -0.5 * LOG2E) * HEAD_DIM * jnp.max(jnp.abs(g_q)) * jnp.max(jnp.abs(g_k))
        attn = lax.cond(bound <= MAX_FIXED_SHIFT,
                        lambda: attend(c_rows - bound, c_rows, online=False),
                        lambda: attend(c_rows, c_rows, online=True))
    else:
        cache_k, cache_v, cache_logf = cache
        past = cache_k.shape[1]
        tables = _s5_tables(*s5_raw, tc=tc)
        y, hfin = _s5(u, tables, perm, h0, tt=tiles["s5"], tc=tc, seq_len=seq_len,
                      carry_mode=False)
        past_rows = cache_logf.transpose(0, 2, 1).reshape(n_seq * N_HEADS, past)
        total = past + seq_len
        padded = -(-total // LANES) * LANES
        lf_all = jnp.concatenate(
            [past_rows, lf_rows, jnp.zeros((n_seq * N_HEADS, padded - total), F32)], axis=1)
        c_all = _cumsum_lanes(lf_all)
        attn = _attn_sample(qb, kb, vb, cache_k, cache_v, c_all,
                            n_seq=n_seq, s_new=seq_len, past=past)

    x1, hm = _mix(x2, y, attn, perm.T, w_glu, g_ssm, g_attn, w_out, g_mlp, tt=tiles["s5"])
    out = _mlp(x1, hm, w_up, w_down, tt=tiles["mlp"], tf=tiles["tf"])
    h_re, h_im = _state_to_gp(hfin)
    return (out.reshape(n_seq, seq_len, D_MODEL),
            k.reshape(n_seq, seq_len, N_HEADS, HEAD_DIM),
            v.reshape(n_seq, seq_len, N_HEADS, HEAD_DIM),
            logf.reshape(n_seq, seq_len, N_HEADS), h_re, h_im)


def _prep_weights(l, g_norm_mix, w_in, b_f, ssm_a_re, ssm_a_im, ssm_log_step, ssm_b_re, ssm_b_im,
                  ssm_c_re, ssm_c_im, ssm_d, w_glu, g_q, g_k, g_out_ssm, g_out_attn, w_out,
                  g_norm_mlp, w_up, w_down):
    n_main = D_SSM + 3 * D_ATTN
    w = w_in[l]
    w_main = w[:, :n_main].astype(BF16)
    w_f = jnp.pad(w[:, n_main:], ((0, 0), (0, LANES - N_HEADS))).astype(BF16)
    b = jnp.pad(b_f[l], (0, LANES - N_HEADS)).reshape(1, LANES)
    s5_raw = (ssm_a_re[l], ssm_a_im[l], ssm_log_step[l], ssm_b_re[l], ssm_b_im[l],
              ssm_c_re[l], ssm_c_im[l], ssm_d[l])
    return (g_norm_mix[l].reshape(1, D_MODEL), w_main, w_f, b,
            g_q[l].reshape(1, HEAD_DIM), g_k[l].reshape(1, HEAD_DIM), s5_raw,
            w_glu[l].astype(BF16), g_out_ssm[l].reshape(1, D_SSM), g_out_attn[l].reshape(1, D_ATTN),
            w_out[l].astype(BF16), g_norm_mlp[l].reshape(1, D_MODEL),
            w_up[l].astype(BF16), w_down[l].astype(BF16))


PROMPT_TILES = dict(proj=512, s5=512, chunk=64, bq=1024, mlp=512, tf=1024)
SAMPLE_TILES = dict(proj=512, s5=512, mlp=512, tf=1024)


def kernel(x_prompt, x_sample, cache_k, cache_v, cache_logf, state_ssm_re, state_ssm_im,
           g_norm_mix, w_in, b_f, ssm_a_re, ssm_a_im, ssm_log_step, ssm_b_re, ssm_b_im,
           ssm_c_re, ssm_c_im, ssm_d, w_glu, g_q, g_k, g_out_ssm, g_out_attn, w_out,
           g_norm_mlp, w_up, w_down):
    depth = w_in.shape[0]
    y_p, y_s = x_prompt, x_sample
    outs_p, outs_s = [], []
    for l in range(depth):
        weights = _prep_weights(l, g_norm_mix, w_in, b_f, ssm_a_re, ssm_a_im, ssm_log_step,
                                ssm_b_re, ssm_b_im, ssm_c_re, ssm_c_im, ssm_d, w_glu, g_q, g_k,
                                g_out_ssm, g_out_attn, w_out, g_norm_mlp, w_up, w_down)
        n_p, l_p = y_p.shape[0], y_p.shape[1]
        y_p, *rest_p = _layer(y_p, weights, n_seq=n_p, seq_len=l_p, tiles=PROMPT_TILES)
        n_s, l_s = y_s.shape[0], y_s.shape[1]
        h0 = _state_from_gp(state_ssm_re[l], state_ssm_im[l])
        y_s, *rest_s = _layer(y_s, weights, n_seq=n_s, seq_len=l_s,
                              cache=(cache_k[l], cache_v[l], cache_logf[l]), h0=h0,
                              tiles=SAMPLE_TILES)
        outs_p.append(rest_p)
        outs_s.append(rest_s)
    stack = lambda outs, idx: jnp.stack([o[idx] for o in outs])
    return (y_p, y_s,
            stack(outs_p, 0), stack(outs_p, 1), stack(outs_p, 2), stack(outs_p, 3), stack(outs_p, 4),
            stack(outs_s, 0), stack(outs_s, 1), stack(outs_s, 2), stack(outs_s, 3), stack(outs_s, 4))
```

```python
import functools
import math

import jax
import jax.numpy as jnp
from jax import lax
from jax.experimental import pallas as pl
from jax.experimental.pallas import tpu as pltpu

D_MODEL = 2048
D_SSM = 1024
SSM_GROUP = 16
N_SSM_GROUPS = 64
SSM_STATE = 64
D_ATTN = 1024
HEAD_DIM = 128
N_HEADS = 8
D_FF = 8192
EPS = 1e-6
NEG_INF = -1e30
LOG2E = math.log2(math.e)
MAX_FIXED_SHIFT = 45.0
EXP2_ZERO = -150.0

LANES = 128
SLAB = 256
N_SLABS = D_SSM // SLAB
SLAB_GROUPS = SLAB // SSM_GROUP
SLAB_STATE = SLAB_GROUPS * SSM_STATE
VMEM_LIMIT = 56 * 1024 * 1024

F32 = jnp.float32
BF16 = jnp.bfloat16


def _params(sem, vmem=VMEM_LIMIT):
    return pltpu.CompilerParams(dimension_semantics=sem, vmem_limit_bytes=vmem)


def _rms(x, g):
    return x * lax.rsqrt(jnp.mean(x * x, axis=-1, keepdims=True) + EPS) * g


def _in_proj_kernel(x_ref, g_ref, w_ref, wf_ref, bf_ref, gq_ref, gk_ref,
                    u_ref, qb_ref, k_ref, kb_ref, v_ref, vb_ref, lf_ref):
    hb = _rms(x_ref[...], g_ref[...]).astype(BF16)

    u_ref[...] = jnp.dot(hb, w_ref[:, 0:D_SSM], preferred_element_type=F32)

    q = jnp.dot(hb, w_ref[:, D_SSM:D_SSM + D_ATTN], preferred_element_type=F32)
    gq = gq_ref[...]
    for h in range(N_HEADS):
        sl = slice(h * HEAD_DIM, (h + 1) * HEAD_DIM)
        qb_ref[:, sl] = (_rms(q[:, sl], gq) * (HEAD_DIM ** -0.5 * LOG2E)).astype(BF16)

    k = jnp.dot(hb, w_ref[:, D_SSM + D_ATTN:D_SSM + 2 * D_ATTN], preferred_element_type=F32)
    gk = gk_ref[...]
    for h in range(N_HEADS):
        sl = slice(h * HEAD_DIM, (h + 1) * HEAD_DIM)
        kn = _rms(k[:, sl], gk)
        k_ref[:, sl] = kn
        kb_ref[:, sl] = kn.astype(BF16)

    v = jnp.dot(hb, w_ref[:, D_SSM + 2 * D_ATTN:D_SSM + 3 * D_ATTN], preferred_element_type=F32)
    v_ref[...] = v
    vb_ref[...] = v.astype(BF16)

    zf = jnp.dot(hb, wf_ref[...], preferred_element_type=F32) + bf_ref[...]
    lf_ref[...] = jnp.minimum(zf, 0.0) - jnp.log1p(jnp.exp(-jnp.abs(zf)))


def _in_proj(x, g, w_main, w_f, b_f, g_q, g_k, *, tt):
    t = x.shape[0]
    assert t % tt == 0
    row = lambda i: (i, 0)
    const = lambda i: (0, 0)
    once = pl.Buffered(1)
    wide = lambda dt: jax.ShapeDtypeStruct((t, D_ATTN), dt)
    return pl.pallas_call(
        _in_proj_kernel,
        grid=(t // tt,),
        in_specs=[
            pl.BlockSpec((tt, D_MODEL), row),
            pl.BlockSpec((1, D_MODEL), const),
            pl.BlockSpec((D_MODEL, D_SSM + 3 * D_ATTN), const, pipeline_mode=once),
            pl.BlockSpec((D_MODEL, LANES), const, pipeline_mode=once),
            pl.BlockSpec((1, LANES), const),
            pl.BlockSpec((1, HEAD_DIM), const),
            pl.BlockSpec((1, HEAD_DIM), const),
        ],
        out_specs=[
            pl.BlockSpec((tt, D_SSM), row),
            pl.BlockSpec((tt, D_ATTN), row),
            pl.BlockSpec((tt, D_ATTN), row),
            pl.BlockSpec((tt, D_ATTN), row),
            pl.BlockSpec((tt, D_ATTN), row),
            pl.BlockSpec((tt, D_ATTN), row),
            pl.BlockSpec((tt, LANES), row),
        ],
        out_shape=[wide(F32), wide(BF16), wide(F32), wide(BF16), wide(F32), wide(BF16),
                   jax.ShapeDtypeStruct((t, LANES), F32)],
        compiler_params=_params(("arbitrary",)),
        name="in_proj",
    )(x, g, w_main, w_f, b_f, g_q, g_k)


def _cumsum_kernel(x_ref, o_ref):
    rows, length = x_ref.shape
    lane = lax.broadcasted_iota(jnp.int32, (rows, LANES), 1)
    carry = jnp.zeros((rows, 1), F32)
    for b in range(length // LANES):
        sl = slice(b * LANES, (b + 1) * LANES)
        x = x_ref[:, sl]
        d = 1
        while d < LANES:
            x = x + jnp.where(lane >= d, pltpu.roll(x, d, axis=1), 0.0)
            d *= 2
        x = x + carry
        o_ref[:, sl] = x * LOG2E
        carry = x[:, LANES - 1:LANES]


def _cumsum_lanes(x):
    rows, length = x.shape
    assert length % LANES == 0 and rows % 8 == 0
    return pl.pallas_call(
        _cumsum_kernel,
        out_shape=jax.ShapeDtypeStruct((rows, length), F32),
        name="cumsum",
    )(x)


def _s5_tables(a_re, a_im, log_step, b_re, b_im, c_re, c_im, d, tc):
    step = jnp.exp(log_step)[:, None]
    mag = jnp.exp(a_re * step)
    abar_re = mag * jnp.cos(a_im * step)
    abar_im = mag * jnp.sin(a_im * step)
    den = a_re * a_re + a_im * a_im
    nr = abar_re - 1.0
    ni = abar_im
    fr = (nr * a_re + ni * a_im) / den
    fi = (ni * a_re - nr * a_im) / den
    bbar_re = fr[..., None] * b_re - fi[..., None] * b_im
    bbar_im = fr[..., None] * b_im + fi[..., None] * b_re
    eye = jnp.eye(SLAB_GROUPS, dtype=F32)

    def in_blockdiag(b):
        b = b.reshape(N_SLABS, SLAB_GROUPS, SSM_STATE, SSM_GROUP)
        return jnp.einsum('sgph,gk->sghkp', b, eye).reshape(N_SLABS, SLAB, SLAB_STATE)

    def out_blockdiag(c):
        c = c.reshape(N_SLABS, SLAB_GROUPS, SSM_GROUP, SSM_STATE)
        return jnp.einsum('sghp,gk->sgpkh', c, eye).reshape(N_SLABS, SLAB_STATE, SLAB)

    bb = jnp.concatenate([in_blockdiag(bbar_re), in_blockdiag(bbar_im)], axis=-1).astype(BF16)
    cc = jnp.concatenate([out_blockdiag(c_re), out_blockdiag(-c_im)], axis=1).astype(BF16)
    n = jnp.arange(1, tc + 1, dtype=F32)[:, None, None]
    pmag = jnp.exp(a_re[None] * step[None] * n)
    ang = a_im[None] * step[None] * n
    p_re = (pmag * jnp.cos(ang)).reshape(tc, N_SLABS, SLAB_STATE).transpose(1, 0, 2)
    p_im = (pmag * jnp.sin(ang)).reshape(tc, N_SLABS, SLAB_STATE).transpose(1, 0, 2)
    pw = jnp.concatenate([p_re, p_im], axis=-1)
    dd = d.reshape(N_SLABS, 1, SLAB)
    return bb, cc, pw, dd


def _s5_kernel(u_ref, perm_ref, bb_ref, cc_ref, pw_ref, d_ref, h0_ref, y_ref, hfin_ref,
               st_s, hb_s, g_s, carry_s, *, tc, nc, tiles_per_seq, carry_mode):
    i = pl.program_id(1)
    u = u_ref[...]
    u_hi = u.astype(BF16)
    u_lo = (u - u_hi.astype(F32)).astype(BF16)
    up = jnp.dot(perm_ref[...], jnp.concatenate([u_hi, u_lo], axis=1),
                 preferred_element_type=F32)
    u_hi = up[:, :SLAB]
    u = u_hi + up[:, SLAB:]
    uh = u_hi.astype(BF16)
    y_ref[...] = d_ref[...] * u
    blk = 2 * LANES
    nblk = SLAB_STATE // blk
    grp = max(1, 16 // nc)
    lanes = lambda b: slice(b * blk, (b + 1) * blk)
    step_rows = lambda t: slice(t * nc, (t + 1) * nc)

    if carry_mode:
        @pl.when(i % tiles_per_seq == 0)
        def _():
            carry_s[...] = jnp.zeros_like(carry_s)
    else:
        g_s[...] = h0_ref[...]

    for b in range(nblk):
        re, im = lanes(b), lanes(nblk + b)
        st_s[:, re] = jnp.dot(uh, bb_ref[:, re], preferred_element_type=F32)
        st_s[:, im] = jnp.dot(uh, bb_ref[:, im], preferred_element_type=F32)
        ar = pw_ref[0:1, re]
        ai = pw_ref[0:1, im]
        hr = jnp.zeros((nc, blk), F32)
        hi = jnp.zeros((nc, blk), F32)
        for t in range(tc):
            rows = step_rows(t)
            hr, hi = (ar * hr - ai * hi + st_s[rows, re],
                      ar * hi + ai * hr + st_s[rows, im])
            st_s[rows, re] = hr
            st_s[rows, im] = hi

        ar = pw_ref[tc - 1:tc, re]
        ai = pw_ref[tc - 1:tc, im]
        if carry_mode:
            gr = carry_s[0:1, re]
            gi = carry_s[0:1, im]
            for c in range(nc):
                g_s[c:c + 1, re] = gr
                g_s[c:c + 1, im] = gi
                gr, gi = (ar * gr - ai * gi + hr[c:c + 1, :],
                          ar * gi + ai * gr + hi[c:c + 1, :])
            carry_s[0:1, re] = gr
            carry_s[0:1, im] = gi
            hfin_ref[0:1, re] = gr
            hfin_ref[0:1, im] = gi
            gr = g_s[:, re]
            gi = g_s[:, im]
        else:
            gr = g_s[:, re]
            gi = g_s[:, im]
            hfin_ref[:, re] = ar * gr - ai * gi + hr
            hfin_ref[:, im] = ar * gi + ai * gr + hi

        for t0 in range(0, tc, grp):
            full_r, full_i = [], []
            for t in range(t0, t0 + grp):
                rows = step_rows(t)
                pr = pw_ref[t:t + 1, re]
                pi = pw_ref[t:t + 1, im]
                full_r.append(st_s[rows, re] + (pr * gr - pi * gi))
                full_i.append(st_s[rows, im] + (pr * gi + pi * gr))
            rows = slice(t0 * nc, (t0 + grp) * nc)
            hb_s[rows, re] = jnp.concatenate(full_r, axis=0).astype(BF16)
            hb_s[rows, im] = jnp.concatenate(full_i, axis=0).astype(BF16)

        y_ref[...] += (jnp.dot(hb_s[:, re], cc_ref[re, :], preferred_element_type=F32)
                       + jnp.dot(hb_s[:, im], cc_ref[im, :], preferred_element_type=F32))


def _step_major_perm(tt, tc):
    nc = tt // tc
    r_out = jnp.arange(tt)
    r_in = (r_out % nc) * tc + r_out // nc
    return (r_in[:, None] == jnp.arange(tt)[None, :]).astype(BF16)


def _s5(u, tables, perm, h0, *, tt, tc, seq_len, carry_mode):
    bb, cc, pw, dd = tables
    t = u.shape[0]
    nc = tt // tc
    n_tiles = t // tt
    assert t % tt == 0 and tt % tc == 0
    if carry_mode:
        assert seq_len % tt == 0
        tiles_per_seq = seq_len // tt
        n_seq = t // seq_len
        hfin_shape = (N_SLABS, n_seq, 1, 2 * SLAB_STATE)
        hfin_spec = pl.BlockSpec((None, None, 1, 2 * SLAB_STATE),
                                 lambda s, i: (s, i // tiles_per_seq, 0, 0))
        h0 = jnp.zeros((N_SLABS, 8, 2 * SLAB_STATE), F32)
        h0_spec = pl.BlockSpec((None, 8, 2 * SLAB_STATE), lambda s, i: (s, 0, 0))
    else:
        assert seq_len == tc
        tiles_per_seq = 1
        hfin_shape = (N_SLABS, n_tiles, nc, 2 * SLAB_STATE)
        hfin_spec = pl.BlockSpec((None, None, nc, 2 * SLAB_STATE), lambda s, i: (s, i, 0, 0))
        h0_spec = pl.BlockSpec((None, nc, 2 * SLAB_STATE), lambda s, i: (s, i, 0))
    kern = functools.partial(_s5_kernel, tc=tc, nc=nc, tiles_per_seq=tiles_per_seq,
                             carry_mode=carry_mode)
    y, hfin = pl.pallas_call(
        kern,
        grid=(N_SLABS, n_tiles),
        in_specs=[
            pl.BlockSpec((tt, SLAB), lambda s, i: (i, s)),
            pl.BlockSpec((tt, tt), lambda s, i: (0, 0), pipeline_mode=pl.Buffered(1)),
            pl.BlockSpec((None, SLAB, 2 * SLAB_STATE), lambda s, i: (s, 0, 0)),
            pl.BlockSpec((None, 2 * SLAB_STATE, SLAB), lambda s, i: (s, 0, 0)),
            pl.BlockSpec((None, tc, 2 * SLAB_STATE), lambda s, i: (s, 0, 0)),
            pl.BlockSpec((None, 1, SLAB), lambda s, i: (s, 0, 0)),
            h0_spec,
        ],
        out_specs=[pl.BlockSpec((tt, SLAB), lambda s, i: (i, s)), hfin_spec],
        out_shape=[jax.ShapeDtypeStruct((t, D_SSM), F32),
                   jax.ShapeDtypeStruct(hfin_shape, F32)],
        scratch_shapes=[pltpu.VMEM((tt, 2 * SLAB_STATE), F32),
                        pltpu.VMEM((tt, 2 * SLAB_STATE), BF16),
                        pltpu.VMEM((nc, 2 * SLAB_STATE), F32),
                        pltpu.VMEM((8, 2 * SLAB_STATE), F32)],
        compiler_params=_params(("arbitrary", "arbitrary")),
        name="s5",
    )(u, perm, bb, cc, pw, dd, h0)
    return y, hfin.reshape(N_SLABS, -1, 2 * SLAB_STATE)


def _state_to_gp(hfin):
    n = hfin.shape[1]
    h = hfin.transpose(1, 0, 2)
    re = h[..., :SLAB_STATE].reshape(n, N_SSM_GROUPS, SSM_STATE)
    im = h[..., SLAB_STATE:].reshape(n, N_SSM_GROUPS, SSM_STATE)
    return re, im


def _state_from_gp(re, im):
    n = re.shape[0]
    h = jnp.concatenate([re.reshape(n, N_SLABS, SLAB_STATE), im.reshape(n, N_SLABS, SLAB_STATE)],
                        axis=-1)
    return h.transpose(1, 0, 2)


def _col_from_row(row):
    n = row.shape[1]
    r = lax.broadcasted_iota(jnp.int32, (n, n), 0)
    c = lax.broadcasted_iota(jnp.int32, (n, n), 1)
    return jnp.sum(jnp.where(r == c, jnp.broadcast_to(row, (n, n)), 0.0), axis=1, keepdims=True)


def _attn_prompt_kernel(first_ref, q_ref, k_ref, v_ref, cq_ref, ck_ref, o_ref,
                        m_s, l_s, acc_s, cq_s, s_s, *, bq, online):
    i = pl.program_id(1)
    bk = bq // 2
    q = q_ref[...]
    cq_s[...] = jnp.broadcast_to(_col_from_row(cq_ref[...]), (bq, LANES))
    if online:
        m_s[...] = jnp.full_like(m_s, NEG_INF)
    l_s[...] = jnp.zeros_like(l_s)
    acc_s[...] = jnp.zeros_like(acc_s)
    nchunk = bk // LANES

    def scores(j, slot):
        start = pl.multiple_of(j * bk, bk)
        s_s[slot] = lax.dot_general(q, k_ref[pl.ds(start, bk), :], (((1,), (1,)), ((), ())),
                                    preferred_element_type=F32)

    def reduce_block(j, slot, diag_offset=None):
        start = pl.multiple_of(j * bk, bk)
        v = v_ref[pl.ds(start, bk), :]

        def logits(c):
            col0 = pl.multiple_of(start + c * LANES, LANES)
            t = s_s[slot, :, c * LANES:(c + 1) * LANES] - ck_ref[:, pl.ds(col0, LANES)]
            if diag_offset is not None:
                r = lax.broadcasted_iota(jnp.int32, (bq, LANES), 0)
                col = lax.broadcasted_iota(jnp.int32, (bq, LANES), 1) + (diag_offset + c * LANES)
                t = jnp.where(col <= r, t, NEG_INF)
            return t

        if online:
            m_loc = functools.reduce(jnp.maximum, [logits(c) for c in range(nchunk)])
            m_old = m_s[...]
            m_new = jnp.maximum(m_old, jnp.max(m_loc, axis=1, keepdims=True) + cq_s[...])
            m_s[...] = m_new
            alpha = jnp.exp2(m_old - m_new)
            shift = cq_s[...] - m_new
        else:
            shift = cq_s[...]
        ps = [jnp.exp2(logits(c) + shift) for c in range(nchunk)]
        p = jnp.concatenate([pc.astype(BF16) for pc in ps], axis=1)
        l_new = functools.reduce(jnp.add, ps)
        acc_new = jnp.dot(p, v, preferred_element_type=F32)
        if online:
            l_s[...] = alpha * l_s[...] + l_new
            acc_s[...] = alpha * acc_s[...] + acc_new
        else:
            l_s[...] += l_new
            acc_s[...] += acc_new

    first = first_ref[pl.program_id(0) * pl.num_programs(1) + i]
    scores(2 * first, 0)

    def body(p, carry):
        j = 2 * p
        scores(j + 1, 1)
        reduce_block(j, 0)
        scores(j + 2, 0)
        reduce_block(j + 1, 1)
        return carry

    lax.fori_loop(first, i, body, 0)
    scores(2 * i + 1, 1)
    reduce_block(2 * i, 0, diag_offset=0)
    reduce_block(2 * i + 1, 1, diag_offset=bk)

    o_ref[...] = acc_s[...] / jnp.sum(l_s[...], axis=1, keepdims=True)


def _attn_prompt(qb, kb, vb, first, cq_rows, ck_rows, *, n_seq, seq_len, bq, online):
    nq = seq_len // bq
    assert seq_len % bq == 0
    kern = functools.partial(_attn_prompt_kernel, bq=bq, online=online)
    q_map = lambda g, i, first: ((g // N_HEADS) * nq + i, g % N_HEADS)
    kv_map = lambda g, i, first: (g // N_HEADS, g % N_HEADS)
    grid_spec = pltpu.PrefetchScalarGridSpec(
        num_scalar_prefetch=1,
        grid=(n_seq * N_HEADS, nq),
        in_specs=[
            pl.BlockSpec((bq, HEAD_DIM), q_map),
            pl.BlockSpec((seq_len, HEAD_DIM), kv_map),
            pl.BlockSpec((seq_len, HEAD_DIM), kv_map),
            pl.BlockSpec((None, 1, bq), lambda g, i, first: (g, 0, i)),
            pl.BlockSpec((None, 1, seq_len), lambda g, i, first: (g, 0, 0)),
        ],
        out_specs=pl.BlockSpec((bq, HEAD_DIM), q_map),
        scratch_shapes=[pltpu.VMEM((bq, LANES), F32), pltpu.VMEM((bq, LANES), F32),
                        pltpu.VMEM((bq, HEAD_DIM), F32), pltpu.VMEM((bq, LANES), F32),
                        pltpu.VMEM((2, bq, bq // 2), F32)],
    )
    return pl.pallas_call(
        kern,
        grid_spec=grid_spec,
        out_shape=jax.ShapeDtypeStruct((n_seq * seq_len, D_ATTN), F32),
        compiler_params=_params(("arbitrary", "arbitrary")),
        name="attn_prompt",
    )(first, qb, kb, vb, cq_rows, ck_rows)


def _first_live_pair(c_rows, bq):
    g, _, length = c_rows.shape
    nq = length // bq
    blocks = c_rows.reshape(g, nq, bq)
    hi = jnp.max(blocks, axis=-1)
    lo = jnp.min(blocks, axis=-1)
    dead = (hi[:, :, None] - lo[:, None, :]) < EXP2_ZERO - 1.0
    dead = dead & (jnp.arange(nq)[None, None, :] < jnp.arange(nq)[None, :, None])
    first = jnp.sum(jnp.cumprod(dead.astype(jnp.int32), axis=-1), axis=-1)
    return first.reshape(g * nq)


def _attn_sample_kernel(q_ref, kn_ref, vn_ref, ck_ref, cv_ref, c_ref, o_ref, *, past, s_new):
    r = lax.broadcasted_iota(jnp.int32, (s_new, s_new), 0)
    col = lax.broadcasted_iota(jnp.int32, (s_new, s_new), 1)
    causal = col <= r
    nt = (((1,), (1,)), ((), ()))
    for h in range(N_HEADS):
        sl = slice(h * HEAD_DIM, (h + 1) * HEAD_DIM)
        q = q_ref[:, sl]
        c_past = c_ref[h:h + 1, 0:past]
        c_new = c_ref[h:h + 1, past:past + s_new]
        cq = _col_from_row(c_new)
        head_rows = pl.ds(h, past, stride=N_HEADS)
        s_p = lax.dot_general(q, ck_ref[head_rows, :].astype(BF16), nt, preferred_element_type=F32)
        s_p = s_p + cq - c_past
        s_n = lax.dot_general(q, kn_ref[:, sl], nt, preferred_element_type=F32)
        s_n = jnp.where(causal, s_n + cq - c_new, NEG_INF)
        m = jnp.maximum(jnp.max(s_p, axis=1, keepdims=True), jnp.max(s_n, axis=1, keepdims=True))
        p_p = jnp.exp2(s_p - m)
        p_n = jnp.exp2(s_n - m)
        l = jnp.sum(p_p, axis=1, keepdims=True) + jnp.sum(p_n, axis=1, keepdims=True)
        acc = (jnp.dot(p_p.astype(BF16), cv_ref[head_rows, :].astype(BF16), preferred_element_type=F32)
               + jnp.dot(p_n.astype(BF16), vn_ref[:, sl], preferred_element_type=F32))
        o_ref[:, sl] = acc / l


def _attn_sample(qb, kb, vb, cache_k, cache_v, c_all, *, n_seq, s_new, past):
    kern = functools.partial(_attn_sample_kernel, past=past, s_new=s_new)
    new = pl.BlockSpec((s_new, D_ATTN), lambda b: (b, 0))
    cache_k = cache_k.reshape(n_seq, past * N_HEADS, HEAD_DIM)
    cache_v = cache_v.reshape(n_seq, past * N_HEADS, HEAD_DIM)
    cache = pl.BlockSpec((None, past * N_HEADS, HEAD_DIM), lambda b: (b, 0, 0))
    return pl.pallas_call(
        kern,
        grid=(n_seq,),
        in_specs=[new, new, new, cache, cache,
                  pl.BlockSpec((N_HEADS, c_all.shape[1]), lambda b: (b, 0))],
        out_specs=new,
        out_shape=jax.ShapeDtypeStruct((n_seq * s_new, D_ATTN), F32),
        compiler_params=_params(("arbitrary",)),
        name="attn_sample",
    )(qb, kb, vb, cache_k, cache_v, c_all)


def _gelu_tanh(x):
    return 0.5 * x * (1.0 + jnp.tanh(math.sqrt(2.0 / math.pi) * (x + 0.044715 * (x * x * x))))


def _mix_kernel(x_ref, y_ref, a_ref, unperm_ref, wglu_ref, gs_ref, ga_ref, wout_ref, gm_ref,
                x1_ref, hm_ref):
    gy = _gelu_tanh(y_ref[...])
    gate = jax.nn.sigmoid(jnp.dot(gy.astype(BF16), wglu_ref[...], preferred_element_type=F32))
    ssm = _rms(gy * gate, gs_ref[...]).astype(BF16)
    ssm = jnp.dot(unperm_ref[...], ssm, preferred_element_type=F32).astype(BF16)
    att = _rms(a_ref[...], ga_ref[...]).astype(BF16)
    x1 = (x_ref[...]
          + jnp.dot(ssm, wout_ref[0:D_SSM, :], preferred_element_type=F32)
          + jnp.dot(att, wout_ref[D_SSM:D_SSM + D_ATTN, :], preferred_element_type=F32))
    x1_ref[...] = x1
    hm_ref[...] = _rms(x1, gm_ref[...]).astype(BF16)


def _mix(x, y, attn, unperm, w_glu, g_ssm, g_attn, w_out, g_mlp, *, tt):
    t = x.shape[0]
    assert t % tt == 0 and unperm.shape == (tt, tt)
    row = lambda i: (i, 0)
    const = lambda i: (0, 0)
    once = pl.Buffered(1)
    return pl.pallas_call(
        _mix_kernel,
        grid=(t // tt,),
        in_specs=[
            pl.BlockSpec((tt, D_MODEL), row),
            pl.BlockSpec((tt, D_SSM), row),
            pl.BlockSpec((tt, D_ATTN), row),
            pl.BlockSpec((tt, tt), const, pipeline_mode=once),
            pl.BlockSpec((D_SSM, D_SSM), const, pipeline_mode=once),
            pl.BlockSpec((1, D_SSM), const),
            pl.BlockSpec((1, D_ATTN), const),
            pl.BlockSpec((D_SSM + D_ATTN, D_MODEL), const, pipeline_mode=once),
            pl.BlockSpec((1, D_MODEL), const),
        ],
        out_specs=[pl.BlockSpec((tt, D_MODEL), row), pl.BlockSpec((tt, D_MODEL), row)],
        out_shape=[jax.ShapeDtypeStruct((t, D_MODEL), F32), jax.ShapeDtypeStruct((t, D_MODEL), BF16)],
        compiler_params=_params(("arbitrary",)),
        name="mix",
    )(x, y, attn, unperm, w_glu, g_ssm, g_attn, w_out, g_mlp)


def _mlp_kernel(x1_ref, hm_ref, wup_ref, wdn_ref, o_ref):
    j = pl.program_id(1)

    @pl.when(j == 0)
    def _():
        o_ref[...] = x1_ref[...]

    a = jnp.maximum(jnp.dot(hm_ref[...], wup_ref[...], preferred_element_type=F32), 0.0)
    o_ref[...] += jnp.dot((a * a).astype(BF16), wdn_ref[...], preferred_element_type=F32)


def _mlp(x1, hm, w_up, w_down, *, tt, tf):
    t = x1.shape[0]
    assert t % tt == 0 and D_FF % tf == 0
    return pl.pallas_call(
        _mlp_kernel,
        grid=(t // tt, D_FF // tf),
        in_specs=[
            pl.BlockSpec((tt, D_MODEL), lambda i, j: (i, 0)),
            pl.BlockSpec((tt, D_MODEL), lambda i, j: (i, 0)),
            pl.BlockSpec((D_MODEL, tf), lambda i, j: (0, j)),
            pl.BlockSpec((tf, D_MODEL), lambda i, j: (j, 0)),
        ],
        out_specs=pl.BlockSpec((tt, D_MODEL), lambda i, j: (i, 0)),
        out_shape=jax.ShapeDtypeStruct((t, D_MODEL), F32),
        compiler_params=_params(("arbitrary", "arbitrary")),
        name="mlp",
    )(x1, hm, w_up, w_down)


def _layer(x, weights, *, n_seq, seq_len, cache=None, h0=None, tiles):
    (g_mix, w_main, w_f, b_f, g_q, g_k, s5_raw, w_glu, g_ssm, g_attn, w_out, g_mlp,
     w_up, w_down) = weights
    t = n_seq * seq_len
    x2 = x.reshape(t, D_MODEL)
    u, qb, k, kb, v, vb, lf = _in_proj(x2, g_mix, w_main, w_f, b_f, g_q, g_k, tt=tiles["proj"])
    logf = lf[:, :N_HEADS]
    lf_rows = logf.reshape(n_seq, seq_len, N_HEADS).transpose(0, 2, 1).reshape(n_seq * N_HEADS, seq_len)

    tc = tiles["chunk"] if cache is None else seq_len
    perm = _step_major_perm(tiles["s5"], tc)
    if cache is None:
        tables = _s5_tables(*s5_raw, tc=tc)
        y, hfin = _s5(u, tables, perm, None, tt=tiles["s5"], tc=tc, seq_len=seq_len,
                      carry_mode=True)
        c_rows = _cumsum_lanes(lf_rows).reshape(n_seq * N_HEADS, 1, seq_len)
        attend = functools.partial(_attn_prompt, qb, kb, vb, n_seq=n_seq, seq_len=seq_len,
                                   bq=tiles["bq"])
        bound = (HEAD_DIM ** -0.5 * LOG2E) * HEAD_DIM * jnp.max(jnp.abs(g_q)) * jnp.max(jnp.abs(g_k))
        attn = lax.cond(
            bound <= MAX_FIXED_SHIFT,
            lambda: attend(_first_live_pair(c_rows, tiles["bq"]), c_rows - bound, c_rows,
                           online=False),
            lambda: attend(jnp.zeros((c_rows.shape[0] * (seq_len // tiles["bq"]),), jnp.int32),
                           c_rows, c_rows, online=True))
    else:
        cache_k, cache_v, cache_logf = cache
        past = cache_k.shape[1]
        tables = _s5_tables(*s5_raw, tc=tc)
        y, hfin = _s5(u, tables, perm, h0, tt=tiles["s5"], tc=tc, seq_len=seq_len,
                      carry_mode=False)
        past_rows = cache_logf.transpose(0, 2, 1).reshape(n_seq * N_HEADS, past)
        total = past + seq_len
        padded = -(-total // LANES) * LANES
        lf_all = jnp.concatenate(
            [past_rows, lf_rows, jnp.zeros((n_seq * N_HEADS, padded - total), F32)], axis=1)
        c_all = _cumsum_lanes(lf_all)
        attn = _attn_sample(qb, kb, vb, cache_k, cache_v, c_all,
                            n_seq=n_seq, s_new=seq_len, past=past)

    x1, hm = _mix(x2, y, attn, perm.T, w_glu, g_ssm, g_attn, w_out, g_mlp, tt=tiles["s5"])
    out = _mlp(x1, hm, w_up, w_down, tt=tiles["mlp"], tf=tiles["tf"])
    h_re, h_im = _state_to_gp(hfin)
    return (out.reshape(n_seq, seq_len, D_MODEL),
            k.reshape(n_seq, seq_len, N_HEADS, HEAD_DIM),
            v.reshape(n_seq, seq_len, N_HEADS, HEAD_DIM),
            logf.reshape(n_seq, seq_len, N_HEADS), h_re, h_im)


def _prep_weights(l, g_norm_mix, w_in, b_f, ssm_a_re, ssm_a_im, ssm_log_step, ssm_b_re, ssm_b_im,
                  ssm_c_re, ssm_c_im, ssm_d, w_glu, g_q, g_k, g_out_ssm, g_out_attn, w_out,
                  g_norm_mlp, w_up, w_down):
    n_main = D_SSM + 3 * D_ATTN
    w = w_in[l]
    w_main = w[:, :n_main].astype(BF16)
    w_f = jnp.pad(w[:, n_main:], ((0, 0), (0, LANES - N_HEADS))).astype(BF16)
    b = jnp.pad(b_f[l], (0, LANES - N_HEADS)).reshape(1, LANES)
    s5_raw = (ssm_a_re[l], ssm_a_im[l], ssm_log_step[l], ssm_b_re[l], ssm_b_im[l],
              ssm_c_re[l], ssm_c_im[l], ssm_d[l])
    return (g_norm_mix[l].reshape(1, D_MODEL), w_main, w_f, b,
            g_q[l].reshape(1, HEAD_DIM), g_k[l].reshape(1, HEAD_DIM), s5_raw,
            w_glu[l].astype(BF16), g_out_ssm[l].reshape(1, D_SSM), g_out_attn[l].reshape(1, D_ATTN),
            w_out[l].astype(BF16), g_norm_mlp[l].reshape(1, D_MODEL),
            w_up[l].astype(BF16), w_down[l].astype(BF16))


PROMPT_TILES = dict(proj=512, s5=512, chunk=64, bq=1024, mlp=512, tf=1024)
SAMPLE_TILES = dict(proj=512, s5=512, mlp=512, tf=1024)


def kernel(x_prompt, x_sample, cache_k, cache_v, cache_logf, state_ssm_re, state_ssm_im,
           g_norm_mix, w_in, b_f, ssm_a_re, ssm_a_im, ssm_log_step, ssm_b_re, ssm_b_im,
           ssm_c_re, ssm_c_im, ssm_d, w_glu, g_q, g_k, g_out_ssm, g_out_attn, w_out,
           g_norm_mlp, w_up, w_down):
    depth = w_in.shape[0]
    y_p, y_s = x_prompt, x_sample
    outs_p, outs_s = [], []
    for l in range(depth):
        weights = _prep_weights(l, g_norm_mix, w_in, b_f, ssm_a_re, ssm_a_im, ssm_log_step,
                                ssm_b_re, ssm_b_im, ssm_c_re, ssm_c_im, ssm_d, w_glu, g_q, g_k,
                                g_out_ssm, g_out_attn, w_out, g_norm_mlp, w_up, w_down)
        n_p, l_p = y_p.shape[0], y_p.shape[1]
        y_p, *rest_p = _layer(y_p, weights, n_seq=n_p, seq_len=l_p, tiles=PROMPT_TILES)
        n_s, l_s = y_s.shape[0], y_s.shape[1]
        h0 = _state_from_gp(state_ssm_re[l], state_ssm_im[l])
        y_s, *rest_s = _layer(y_s, weights, n_seq=n_s, seq_len=l_s,
                              cache=(cache_k[l], cache_v[l], cache_logf[l]), h0=h0,
                              tiles=SAMPLE_TILES)
        outs_p.append(rest_p)
        outs_s.append(rest_s)
    stack = lambda outs, idx: jnp.stack([o[idx] for o in outs])
    return (y_p, y_s,
            stack(outs_p, 0), stack(outs_p, 1), stack(outs_p, 2), stack(outs_p, 3), stack(outs_p, 4),
            stack(outs_s, 0), stack(outs_s, 1), stack(outs_s, 2), stack(outs_s, 3), stack(outs_s, 4))
```

```python
import functools
import math

import jax
import jax.numpy as jnp
from jax import lax
from jax.experimental import pallas as pl
from jax.experimental.pallas import tpu as pltpu

D_MODEL = 2048
D_SSM = 1024
SSM_GROUP = 16
N_SSM_GROUPS = 64
SSM_STATE = 64
D_ATTN = 1024
HEAD_DIM = 128
N_HEADS = 8
D_FF = 8192
EPS = 1e-6
NEG_INF = -1e30
LOG2E = math.log2(math.e)
MAX_FIXED_SHIFT = 45.0
EXP2_ZERO = -150.0

LANES = 128
SLAB = 256
N_SLABS = D_SSM // SLAB
SLAB_GROUPS = SLAB // SSM_GROUP
SLAB_STATE = SLAB_GROUPS * SSM_STATE
VMEM_LIMIT = 56 * 1024 * 1024

F32 = jnp.float32
BF16 = jnp.bfloat16


def _params(sem, vmem=VMEM_LIMIT):
    return pltpu.CompilerParams(dimension_semantics=sem, vmem_limit_bytes=vmem)


def _rms(x, g):
    return x * lax.rsqrt(jnp.mean(x * x, axis=-1, keepdims=True) + EPS) * g


def _in_proj_kernel(x_ref, g_ref, w_ref, wf_ref, bf_ref, gq_ref, gk_ref,
                    u_ref, qb_ref, k_ref, kb_ref, v_ref, vb_ref, lf_ref, lft_ref):
    hb = _rms(x_ref[...], g_ref[...]).astype(BF16)

    u_ref[...] = jnp.dot(hb, w_ref[:, 0:D_SSM], preferred_element_type=F32)

    q = jnp.dot(hb, w_ref[:, D_SSM:D_SSM + D_ATTN], preferred_element_type=F32)
    gq = gq_ref[...]
    for h in range(N_HEADS):
        sl = slice(h * HEAD_DIM, (h + 1) * HEAD_DIM)
        qb_ref[:, sl] = (_rms(q[:, sl], gq) * (HEAD_DIM ** -0.5 * LOG2E)).astype(BF16)

    k = jnp.dot(hb, w_ref[:, D_SSM + D_ATTN:D_SSM + 2 * D_ATTN], preferred_element_type=F32)
    gk = gk_ref[...]
    for h in range(N_HEADS):
        sl = slice(h * HEAD_DIM, (h + 1) * HEAD_DIM)
        kn = _rms(k[:, sl], gk)
        k_ref[:, sl] = kn
        kb_ref[:, sl] = kn.astype(BF16)

    v = jnp.dot(hb, w_ref[:, D_SSM + 2 * D_ATTN:D_SSM + 3 * D_ATTN], preferred_element_type=F32)
    v_ref[...] = v
    vb_ref[...] = v.astype(BF16)

    zf = jnp.dot(hb, wf_ref[...], preferred_element_type=F32) + bf_ref[...]
    lf = jnp.minimum(zf, 0.0) - jnp.log1p(jnp.exp(-jnp.abs(zf)))
    lf_ref[...] = lf[:, :N_HEADS]
    lft_ref[...] = jnp.transpose(lf)[:N_HEADS, :]


def _in_proj(x, g, w_main, w_f, b_f, g_q, g_k, *, tt):
    t = x.shape[0]
    assert t % tt == 0
    row = lambda i: (i, 0)
    const = lambda i: (0, 0)
    once = pl.Buffered(1)
    wide = lambda dt: jax.ShapeDtypeStruct((t, D_ATTN), dt)
    return pl.pallas_call(
        _in_proj_kernel,
        grid=(t // tt,),
        in_specs=[
            pl.BlockSpec((tt, D_MODEL), row),
            pl.BlockSpec((1, D_MODEL), const),
            pl.BlockSpec((D_MODEL, D_SSM + 3 * D_ATTN), const, pipeline_mode=once),
            pl.BlockSpec((D_MODEL, LANES), const, pipeline_mode=once),
            pl.BlockSpec((1, LANES), const),
            pl.BlockSpec((1, HEAD_DIM), const),
            pl.BlockSpec((1, HEAD_DIM), const),
        ],
        out_specs=[
            pl.BlockSpec((tt, D_SSM), row),
            pl.BlockSpec((tt, D_ATTN), row),
            pl.BlockSpec((tt, D_ATTN), row),
            pl.BlockSpec((tt, D_ATTN), row),
            pl.BlockSpec((tt, D_ATTN), row),
            pl.BlockSpec((tt, D_ATTN), row),
            pl.BlockSpec((tt, N_HEADS), row),
            pl.BlockSpec((N_HEADS, tt), lambda i: (0, i)),
        ],
        out_shape=[wide(F32), wide(BF16), wide(F32), wide(BF16), wide(F32), wide(BF16),
                   jax.ShapeDtypeStruct((t, N_HEADS), F32),
                   jax.ShapeDtypeStruct((N_HEADS, t), F32)],
        compiler_params=_params(("arbitrary",)),
        name="in_proj",
    )(x, g, w_main, w_f, b_f, g_q, g_k)


def _cumsum_kernel(x_ref, o_ref):
    rows, length = x_ref.shape
    lane = lax.broadcasted_iota(jnp.int32, (rows, LANES), 1)
    carry = jnp.zeros((rows, 1), F32)
    for b in range(length // LANES):
        sl = slice(b * LANES, (b + 1) * LANES)
        x = x_ref[:, sl]
        d = 1
        while d < LANES:
            x = x + jnp.where(lane >= d, pltpu.roll(x, d, axis=1), 0.0)
            d *= 2
        x = x + carry
        o_ref[:, sl] = x * LOG2E
        carry = x[:, LANES - 1:LANES]


def _cumsum_lanes(x):
    rows, length = x.shape
    assert length % LANES == 0 and rows % 8 == 0
    return pl.pallas_call(
        _cumsum_kernel,
        out_shape=jax.ShapeDtypeStruct((rows, length), F32),
        name="cumsum",
    )(x)


def _s5_tables(a_re, a_im, log_step, b_re, b_im, c_re, c_im, d, tc):
    step = jnp.exp(log_step)[:, None]
    mag = jnp.exp(a_re * step)
    abar_re = mag * jnp.cos(a_im * step)
    abar_im = mag * jnp.sin(a_im * step)
    den = a_re * a_re + a_im * a_im
    nr = abar_re - 1.0
    ni = abar_im
    fr = (nr * a_re + ni * a_im) / den
    fi = (ni * a_re - nr * a_im) / den
    bbar_re = fr[..., None] * b_re - fi[..., None] * b_im
    bbar_im = fr[..., None] * b_im + fi[..., None] * b_re
    eye = jnp.eye(SLAB_GROUPS, dtype=F32)

    def in_blockdiag(b):
        b = b.reshape(N_SLABS, SLAB_GROUPS, SSM_STATE, SSM_GROUP)
        return jnp.einsum('sgph,gk->sghkp', b, eye).reshape(N_SLABS, SLAB, SLAB_STATE)

    def out_blockdiag(c):
        c = c.reshape(N_SLABS, SLAB_GROUPS, SSM_GROUP, SSM_STATE)
        return jnp.einsum('sghp,gk->sgpkh', c, eye).reshape(N_SLABS, SLAB_STATE, SLAB)

    bb = jnp.concatenate([in_blockdiag(bbar_re), in_blockdiag(bbar_im)], axis=-1).astype(BF16)
    cc = jnp.concatenate([out_blockdiag(c_re), out_blockdiag(-c_im)], axis=1).astype(BF16)
    n = jnp.arange(1, tc + 1, dtype=F32)[:, None, None]
    pmag = jnp.exp(a_re[None] * step[None] * n)
    ang = a_im[None] * step[None] * n
    p_re = (pmag * jnp.cos(ang)).reshape(tc, N_SLABS, SLAB_STATE).transpose(1, 0, 2)
    p_im = (pmag * jnp.sin(ang)).reshape(tc, N_SLABS, SLAB_STATE).transpose(1, 0, 2)
    pw = jnp.concatenate([p_re, p_im], axis=-1)
    dd = d.reshape(N_SLABS, 1, SLAB)
    return bb, cc, pw, dd


def _s5_kernel(u_ref, perm_ref, bb_ref, cc_ref, pw_ref, d_ref, h0_ref, y_ref, hfin_ref,
               st_s, hb_s, g_s, carry_s, *, tc, nc, tiles_per_seq, carry_mode):
    i = pl.program_id(1)
    u = u_ref[...]
    u_hi = u.astype(BF16)
    u_lo = (u - u_hi.astype(F32)).astype(BF16)
    up = jnp.dot(perm_ref[...], jnp.concatenate([u_hi, u_lo], axis=1),
                 preferred_element_type=F32)
    u_hi = up[:, :SLAB]
    u = u_hi + up[:, SLAB:]
    uh = u_hi.astype(BF16)
    y_ref[...] = d_ref[...] * u
    blk = 2 * LANES
    nblk = SLAB_STATE // blk
    grp = max(1, 16 // nc)
    lanes = lambda b: slice(b * blk, (b + 1) * blk)
    step_rows = lambda t: slice(t * nc, (t + 1) * nc)

    if carry_mode:
        @pl.when(i % tiles_per_seq == 0)
        def _():
            carry_s[...] = jnp.zeros_like(carry_s)
    else:
        g_s[...] = h0_ref[...]

    for b in range(nblk):
        re, im = lanes(b), lanes(nblk + b)
        st_s[:, re] = jnp.dot(uh, bb_ref[:, re], preferred_element_type=F32)
        st_s[:, im] = jnp.dot(uh, bb_ref[:, im], preferred_element_type=F32)
        ar = pw_ref[0:1, re]
        ai = pw_ref[0:1, im]
        hr = jnp.zeros((nc, blk), F32)
        hi = jnp.zeros((nc, blk), F32)
        for t in range(tc):
            rows = step_rows(t)
            hr, hi = (ar * hr - ai * hi + st_s[rows, re],
                      ar * hi + ai * hr + st_s[rows, im])
            st_s[rows, re] = hr
            st_s[rows, im] = hi

        ar = pw_ref[tc - 1:tc, re]
        ai = pw_ref[tc - 1:tc, im]
        if carry_mode:
            gr = carry_s[0:1, re]
            gi = carry_s[0:1, im]
            for c in range(nc):
                g_s[c:c + 1, re] = gr
                g_s[c:c + 1, im] = gi
                gr, gi = (ar * gr - ai * gi + hr[c:c + 1, :],
                          ar * gi + ai * gr + hi[c:c + 1, :])
            carry_s[0:1, re] = gr
            carry_s[0:1, im] = gi
            hfin_ref[0:1, re] = gr
            hfin_ref[0:1, im] = gi
            gr = g_s[:, re]
            gi = g_s[:, im]
        else:
            gr = g_s[:, re]
            gi = g_s[:, im]
            hfin_ref[:, re] = ar * gr - ai * gi + hr
            hfin_ref[:, im] = ar * gi + ai * gr + hi

        for t0 in range(0, tc, grp):
            full_r, full_i = [], []
            for t in range(t0, t0 + grp):
                rows = step_rows(t)
                pr = pw_ref[t:t + 1, re]
                pi = pw_ref[t:t + 1, im]
                full_r.append(st_s[rows, re] + (pr * gr - pi * gi))
                full_i.append(st_s[rows, im] + (pr * gi + pi * gr))
            rows = slice(t0 * nc, (t0 + grp) * nc)
            hb_s[rows, re] = jnp.concatenate(full_r, axis=0).astype(BF16)
            hb_s[rows, im] = jnp.concatenate(full_i, axis=0).astype(BF16)

        y_ref[...] += (jnp.dot(hb_s[:, re], cc_ref[re, :], preferred_element_type=F32)
                       + jnp.dot(hb_s[:, im], cc_ref[im, :], preferred_element_type=F32))


def _step_major_perm(tt, tc):
    nc = tt // tc
    r_out = jnp.arange(tt)
    r_in = (r_out % nc) * tc + r_out // nc
    return (r_in[:, None] == jnp.arange(tt)[None, :]).astype(BF16)


def _s5(u, tables, perm, h0, *, tt, tc, seq_len, carry_mode):
    bb, cc, pw, dd = tables
    t = u.shape[0]
    nc = tt // tc
    n_tiles = t // tt
    assert t % tt == 0 and tt % tc == 0
    if carry_mode:
        assert seq_len % tt == 0
        tiles_per_seq = seq_len // tt
        n_seq = t // seq_len
        hfin_shape = (N_SLABS, n_seq, 1, 2 * SLAB_STATE)
        hfin_spec = pl.BlockSpec((None, None, 1, 2 * SLAB_STATE),
                                 lambda s, i: (s, i // tiles_per_seq, 0, 0))
        h0 = jnp.zeros((N_SLABS, 8, 2 * SLAB_STATE), F32)
        h0_spec = pl.BlockSpec((None, 8, 2 * SLAB_STATE), lambda s, i: (s, 0, 0))
    else:
        assert seq_len == tc
        tiles_per_seq = 1
        hfin_shape = (N_SLABS, n_tiles, nc, 2 * SLAB_STATE)
        hfin_spec = pl.BlockSpec((None, None, nc, 2 * SLAB_STATE), lambda s, i: (s, i, 0, 0))
        h0_spec = pl.BlockSpec((None, nc, 2 * SLAB_STATE), lambda s, i: (s, i, 0))
    kern = functools.partial(_s5_kernel, tc=tc, nc=nc, tiles_per_seq=tiles_per_seq,
                             carry_mode=carry_mode)
    y, hfin = pl.pallas_call(
        kern,
        grid=(N_SLABS, n_tiles),
        in_specs=[
            pl.BlockSpec((tt, SLAB), lambda s, i: (i, s)),
            pl.BlockSpec((tt, tt), lambda s, i: (0, 0), pipeline_mode=pl.Buffered(1)),
            pl.BlockSpec((None, SLAB, 2 * SLAB_STATE), lambda s, i: (s, 0, 0)),
            pl.BlockSpec((None, 2 * SLAB_STATE, SLAB), lambda s, i: (s, 0, 0)),
            pl.BlockSpec((None, tc, 2 * SLAB_STATE), lambda s, i: (s, 0, 0)),
            pl.BlockSpec((None, 1, SLAB), lambda s, i: (s, 0, 0)),
            h0_spec,
        ],
        out_specs=[pl.BlockSpec((tt, SLAB), lambda s, i: (i, s)), hfin_spec],
        out_shape=[jax.ShapeDtypeStruct((t, D_SSM), F32),
                   jax.ShapeDtypeStruct(hfin_shape, F32)],
        scratch_shapes=[pltpu.VMEM((tt, 2 * SLAB_STATE), F32),
                        pltpu.VMEM((tt, 2 * SLAB_STATE), BF16),
                        pltpu.VMEM((nc, 2 * SLAB_STATE), F32),
                        pltpu.VMEM((8, 2 * SLAB_STATE), F32)],
        compiler_params=_params(("arbitrary", "arbitrary")),
        name="s5",
    )(u, perm, bb, cc, pw, dd, h0)
    return y, hfin.reshape(N_SLABS, -1, 2 * SLAB_STATE)


def _state_to_gp(hfin):
    n = hfin.shape[1]
    h = hfin.transpose(1, 0, 2)
    re = h[..., :SLAB_STATE].reshape(n, N_SSM_GROUPS, SSM_STATE)
    im = h[..., SLAB_STATE:].reshape(n, N_SSM_GROUPS, SSM_STATE)
    return re, im


def _state_from_gp(re, im):
    n = re.shape[0]
    h = jnp.concatenate([re.reshape(n, N_SLABS, SLAB_STATE), im.reshape(n, N_SLABS, SLAB_STATE)],
                        axis=-1)
    return h.transpose(1, 0, 2)


def _col_from_row(row):
    n = row.shape[1]
    r = lax.broadcasted_iota(jnp.int32, (n, n), 0)
    c = lax.broadcasted_iota(jnp.int32, (n, n), 1)
    return jnp.sum(jnp.where(r == c, jnp.broadcast_to(row, (n, n)), 0.0), axis=1, keepdims=True)


def _attn_prompt_kernel(first_ref, q_ref, k_ref, v_ref, cq_ref, ck_ref, o_ref,
                        m_s, l_s, acc_s, cq_s, s_s, *, bq, online):
    i = pl.program_id(1)
    bk = bq // 2
    cq_s[...] = jnp.transpose(jnp.broadcast_to(cq_ref[...], (LANES, bq)))
    if online:
        m_s[...] = jnp.full_like(m_s, NEG_INF)
    l_s[...] = jnp.zeros_like(l_s)
    acc_s[...] = jnp.zeros_like(acc_s)
    nchunk = bk // LANES
    all_rows = slice(0, bq)

    def scores(j, slot, rows=all_rows):
        start = pl.multiple_of(j * bk, bk)
        s_s[slot, rows, :] = lax.dot_general(q_ref[rows, :], k_ref[pl.ds(start, bk), :],
                                             (((1,), (1,)), ((), ())), preferred_element_type=F32)

    def reduce_block(j, slot, rows=all_rows, causal=False):
        start = pl.multiple_of(j * bk, bk)
        v = v_ref[pl.ds(start, bk), :]
        nrows = rows.stop - rows.start

        def logits(c):
            col0 = pl.multiple_of(start + c * LANES, LANES)
            t = s_s[slot, rows, c * LANES:(c + 1) * LANES] - ck_ref[:, pl.ds(col0, LANES)]
            if causal:
                r = lax.broadcasted_iota(jnp.int32, (nrows, LANES), 0)
                col = lax.broadcasted_iota(jnp.int32, (nrows, LANES), 1) + c * LANES
                t = jnp.where(col <= r, t, NEG_INF)
            return t

        if online:
            m_loc = functools.reduce(jnp.maximum, [logits(c) for c in range(nchunk)])
            m_old = m_s[rows, :]
            m_new = jnp.maximum(m_old, jnp.max(m_loc, axis=1, keepdims=True) + cq_s[rows, :])
            m_s[rows, :] = m_new
            alpha = jnp.exp2(m_old - m_new)
            shift = cq_s[rows, :] - m_new
        else:
            shift = cq_s[rows, :]
        ps = [jnp.exp2(logits(c) + shift) for c in range(nchunk)]
        p = jnp.concatenate([pc.astype(BF16) for pc in ps], axis=1)
        l_new = functools.reduce(jnp.add, ps)
        acc_new = jnp.dot(p, v, preferred_element_type=F32)
        if online:
            l_s[rows, :] = alpha * l_s[rows, :] + l_new
            acc_s[rows, :] = alpha * acc_s[rows, :] + acc_new
        else:
            l_s[rows, :] += l_new
            acc_s[rows, :] += acc_new

    first = first_ref[pl.program_id(0) * pl.num_programs(1) + i]
    scores(2 * first, 0)

    def body(p, carry):
        j = 2 * p
        scores(j + 1, 1)
        reduce_block(j, 0)
        scores(j + 2, 0)
        reduce_block(j + 1, 1)
        return carry

    lax.fori_loop(first, i, body, 0)
    late_rows = slice(bk, bq)
    scores(2 * i + 1, 1, late_rows)
    reduce_block(2 * i, 0, causal=True)
    reduce_block(2 * i + 1, 1, late_rows, causal=True)

    o_ref[...] = acc_s[...] / jnp.sum(l_s[...], axis=1, keepdims=True)


def _attn_prompt(qb, kb, vb, first, cq_rows, ck_rows, *, n_seq, seq_len, bq, online):
    nq = seq_len // bq
    assert seq_len % bq == 0
    kern = functools.partial(_attn_prompt_kernel, bq=bq, online=online)
    q_map = lambda g, i, first: ((g // N_HEADS) * nq + i, g % N_HEADS)
    kv_map = lambda g, i, first: (g // N_HEADS, g % N_HEADS)
    grid_spec = pltpu.PrefetchScalarGridSpec(
        num_scalar_prefetch=1,
        grid=(n_seq * N_HEADS, nq),
        in_specs=[
            pl.BlockSpec((bq, HEAD_DIM), q_map),
            pl.BlockSpec((seq_len, HEAD_DIM), kv_map),
            pl.BlockSpec((seq_len, HEAD_DIM), kv_map),
            pl.BlockSpec((None, 1, bq), lambda g, i, first: (g, 0, i)),
            pl.BlockSpec((None, 1, seq_len), lambda g, i, first: (g, 0, 0)),
        ],
        out_specs=pl.BlockSpec((bq, HEAD_DIM), q_map),
        scratch_shapes=[pltpu.VMEM((bq, LANES), F32), pltpu.VMEM((bq, LANES), F32),
                        pltpu.VMEM((bq, HEAD_DIM), F32), pltpu.VMEM((bq, LANES), F32),
                        pltpu.VMEM((2, bq, bq // 2), F32)],
    )
    return pl.pallas_call(
        kern,
        grid_spec=grid_spec,
        out_shape=jax.ShapeDtypeStruct((n_seq * seq_len, D_ATTN), F32),
        compiler_params=_params(("arbitrary", "arbitrary")),
        name="attn_prompt",
    )(first, qb, kb, vb, cq_rows, ck_rows)


def _first_live_pair(c_rows, bq):
    g, _, length = c_rows.shape
    nq = length // bq
    blocks = c_rows.reshape(g, nq, bq)
    hi = jnp.max(blocks, axis=-1)
    lo = jnp.min(blocks, axis=-1)
    dead = (hi[:, :, None] - lo[:, None, :]) < EXP2_ZERO - 1.0
    pair = jnp.arange(nq, dtype=jnp.int32)
    dead = dead & (pair[None, None, :] < pair[None, :, None])
    first = jnp.min(jnp.where(dead, nq, pair[None, None, :]), axis=-1)
    return first.reshape(g * nq).astype(jnp.int32)


def _attn_sample_kernel(q_ref, kn_ref, vn_ref, ck_ref, cv_ref, c_ref, o_ref, *, past, s_new):
    r = lax.broadcasted_iota(jnp.int32, (s_new, s_new), 0)
    col = lax.broadcasted_iota(jnp.int32, (s_new, s_new), 1)
    causal = col <= r
    nt = (((1,), (1,)), ((), ()))
    for h in range(N_HEADS):
        sl = slice(h * HEAD_DIM, (h + 1) * HEAD_DIM)
        q = q_ref[:, sl]
        c_past = c_ref[h:h + 1, 0:past]
        c_new = c_ref[h:h + 1, past:past + s_new]
        cq = _col_from_row(c_new)
        head_rows = pl.ds(h, past, stride=N_HEADS)
        s_p = lax.dot_general(q, ck_ref[head_rows, :].astype(BF16), nt, preferred_element_type=F32)
        s_p = s_p + cq - c_past
        s_n = lax.dot_general(q, kn_ref[:, sl], nt, preferred_element_type=F32)
        s_n = jnp.where(causal, s_n + cq - c_new, NEG_INF)
        m = jnp.maximum(jnp.max(s_p, axis=1, keepdims=True), jnp.max(s_n, axis=1, keepdims=True))
        p_p = jnp.exp2(s_p - m)
        p_n = jnp.exp2(s_n - m)
        l = jnp.sum(p_p, axis=1, keepdims=True) + jnp.sum(p_n, axis=1, keepdims=True)
        acc = (jnp.dot(p_p.astype(BF16), cv_ref[head_rows, :].astype(BF16), preferred_element_type=F32)
               + jnp.dot(p_n.astype(BF16), vn_ref[:, sl], preferred_element_type=F32))
        o_ref[:, sl] = acc / l


def _attn_sample(qb, kb, vb, cache_k, cache_v, c_all, *, n_seq, s_new, past):
    kern = functools.partial(_attn_sample_kernel, past=past, s_new=s_new)
    new = pl.BlockSpec((s_new, D_ATTN), lambda b: (b, 0))
    cache_k = cache_k.reshape(n_seq, past * N_HEADS, HEAD_DIM)
    cache_v = cache_v.reshape(n_seq, past * N_HEADS, HEAD_DIM)
    cache = pl.BlockSpec((None, past * N_HEADS, HEAD_DIM), lambda b: (b, 0, 0))
    return pl.pallas_call(
        kern,
        grid=(n_seq,),
        in_specs=[new, new, new, cache, cache,
                  pl.BlockSpec((N_HEADS, c_all.shape[1]), lambda b: (b, 0))],
        out_specs=new,
        out_shape=jax.ShapeDtypeStruct((n_seq * s_new, D_ATTN), F32),
        compiler_params=_params(("arbitrary",)),
        name="attn_sample",
    )(qb, kb, vb, cache_k, cache_v, c_all)


def _gelu_tanh(x):
    return 0.5 * x * (1.0 + jnp.tanh(math.sqrt(2.0 / math.pi) * (x + 0.044715 * (x * x * x))))


def _mix_kernel(x_ref, y_ref, a_ref, unperm_ref, wglu_ref, gs_ref, ga_ref, wout_ref, gm_ref,
                x1_ref, hm_ref):
    gy = _gelu_tanh(y_ref[...])
    gate = jax.nn.sigmoid(jnp.dot(gy.astype(BF16), wglu_ref[...], preferred_element_type=F32))
    ssm = _rms(gy * gate, gs_ref[...]).astype(BF16)
    ssm = jnp.dot(unperm_ref[...], ssm, preferred_element_type=F32).astype(BF16)
    att = _rms(a_ref[...], ga_ref[...]).astype(BF16)
    x1 = (x_ref[...]
          + jnp.dot(ssm, wout_ref[0:D_SSM, :], preferred_element_type=F32)
          + jnp.dot(att, wout_ref[D_SSM:D_SSM + D_ATTN, :], preferred_element_type=F32))
    x1_ref[...] = x1
    hm_ref[...] = _rms(x1, gm_ref[...]).astype(BF16)


def _mix(x, y, attn, unperm, w_glu, g_ssm, g_attn, w_out, g_mlp, *, tt):
    t = x.shape[0]
    assert t % tt == 0 and unperm.shape == (tt, tt)
    row = lambda i: (i, 0)
    const = lambda i: (0, 0)
    once = pl.Buffered(1)
    return pl.pallas_call(
        _mix_kernel,
        grid=(t // tt,),
        in_specs=[
            pl.BlockSpec((tt, D_MODEL), row),
            pl.BlockSpec((tt, D_SSM), row),
            pl.BlockSpec((tt, D_ATTN), row),
            pl.BlockSpec((tt, tt), const, pipeline_mode=once),
            pl.BlockSpec((D_SSM, D_SSM), const, pipeline_mode=once),
            pl.BlockSpec((1, D_SSM), const),
            pl.BlockSpec((1, D_ATTN), const),
            pl.BlockSpec((D_SSM + D_ATTN, D_MODEL), const, pipeline_mode=once),
            pl.BlockSpec((1, D_MODEL), const),
        ],
        out_specs=[pl.BlockSpec((tt, D_MODEL), row), pl.BlockSpec((tt, D_MODEL), row)],
        out_shape=[jax.ShapeDtypeStruct((t, D_MODEL), F32), jax.ShapeDtypeStruct((t, D_MODEL), BF16)],
        compiler_params=_params(("arbitrary",)),
        name="mix",
    )(x, y, attn, unperm, w_glu, g_ssm, g_attn, w_out, g_mlp)


def _mlp_kernel(x1_ref, hm_ref, wup_ref, wdn_ref, o_ref):
    j = pl.program_id(1)

    @pl.when(j == 0)
    def _():
        o_ref[...] = x1_ref[...]

    a = jnp.maximum(jnp.dot(hm_ref[...], wup_ref[...], preferred_element_type=F32), 0.0)
    o_ref[...] += jnp.dot((a * a).astype(BF16), wdn_ref[...], preferred_element_type=F32)


def _mlp(x1, hm, w_up, w_down, *, tt, tf):
    t = x1.shape[0]
    assert t % tt == 0 and D_FF % tf == 0
    return pl.pallas_call(
        _mlp_kernel,
        grid=(t // tt, D_FF // tf),
        in_specs=[
            pl.BlockSpec((tt, D_MODEL), lambda i, j: (i, 0)),
            pl.BlockSpec((tt, D_MODEL), lambda i, j: (i, 0)),
            pl.BlockSpec((D_MODEL, tf), lambda i, j: (0, j)),
            pl.BlockSpec((tf, D_MODEL), lambda i, j: (j, 0)),
        ],
        out_specs=pl.BlockSpec((tt, D_MODEL), lambda i, j: (i, 0)),
        out_shape=jax.ShapeDtypeStruct((t, D_MODEL), F32),
        compiler_params=_params(("arbitrary", "arbitrary")),
        name="mlp",
    )(x1, hm, w_up, w_down)


def _layer(x, weights, *, n_seq, seq_len, cache=None, h0=None, tiles):
    (g_mix, w_main, w_f, b_f, g_q, g_k, s5_raw, w_glu, g_ssm, g_attn, w_out, g_mlp,
     w_up, w_down) = weights
    t = n_seq * seq_len
    x2 = x.reshape(t, D_MODEL)
    u, qb, k, kb, v, vb, logf, lf_t = _in_proj(x2, g_mix, w_main, w_f, b_f, g_q, g_k,
                                               tt=tiles["proj"])
    lf_rows = lf_t.reshape(N_HEADS, n_seq, seq_len).transpose(1, 0, 2).reshape(n_seq * N_HEADS,
                                                                               seq_len)

    tc = tiles["chunk"] if cache is None else seq_len
    perm = _step_major_perm(tiles["s5"], tc)
    if cache is None:
        tables = _s5_tables(*s5_raw, tc=tc)
        y, hfin = _s5(u, tables, perm, None, tt=tiles["s5"], tc=tc, seq_len=seq_len,
                      carry_mode=True)
        c_rows = _cumsum_lanes(lf_rows).reshape(n_seq * N_HEADS, 1, seq_len)
        attend = functools.partial(_attn_prompt, qb, kb, vb, n_seq=n_seq, seq_len=seq_len,
                                   bq=tiles["bq"])
        bound = (HEAD_DIM ** -0.5 * LOG2E) * HEAD_DIM * jnp.max(jnp.abs(g_q)) * jnp.max(jnp.abs(g_k))
        attn = lax.cond(
            bound <= MAX_FIXED_SHIFT,
            lambda: attend(_first_live_pair(c_rows, tiles["bq"]), c_rows - bound, c_rows,
                           online=False),
            lambda: attend(jnp.zeros((c_rows.shape[0] * (seq_len // tiles["bq"]),), jnp.int32),
                           c_rows, c_rows, online=True))
    else:
        cache_k, cache_v, cache_logf = cache
        past = cache_k.shape[1]
        tables = _s5_tables(*s5_raw, tc=tc)
        y, hfin = _s5(u, tables, perm, h0, tt=tiles["s5"], tc=tc, seq_len=seq_len,
                      carry_mode=False)
        past_rows = cache_logf.transpose(0, 2, 1).reshape(n_seq * N_HEADS, past)
        total = past + seq_len
        padded = -(-total // LANES) * LANES
        lf_all = jnp.concatenate(
            [past_rows, lf_rows, jnp.zeros((n_seq * N_HEADS, padded - total), F32)], axis=1)
        c_all = _cumsum_lanes(lf_all)
        attn = _attn_sample(qb, kb, vb, cache_k, cache_v, c_all,
                            n_seq=n_seq, s_new=seq_len, past=past)

    x1, hm = _mix(x2, y, attn, perm.T, w_glu, g_ssm, g_attn, w_out, g_mlp, tt=tiles["s5"])
    out = _mlp(x1, hm, w_up, w_down, tt=tiles["mlp"], tf=tiles["tf"])
    h_re, h_im = _state_to_gp(hfin)
    return (out.reshape(n_seq, seq_len, D_MODEL),
            k.reshape(n_seq, seq_len, N_HEADS, HEAD_DIM),
            v.reshape(n_seq, seq_len, N_HEADS, HEAD_DIM),
            logf.reshape(n_seq, seq_len, N_HEADS), h_re, h_im)


def _prep_weights(l, g_norm_mix, w_in, b_f, ssm_a_re, ssm_a_im, ssm_log_step, ssm_b_re, ssm_b_im,
                  ssm_c_re, ssm_c_im, ssm_d, w_glu, g_q, g_k, g_out_ssm, g_out_attn, w_out,
                  g_norm_mlp, w_up, w_down):
    n_main = D_SSM + 3 * D_ATTN
    w = w_in[l]
    w_main = w[:, :n_main].astype(BF16)
    w_f = jnp.pad(w[:, n_main:], ((0, 0), (0, LANES - N_HEADS))).astype(BF16)
    b = jnp.pad(b_f[l], (0, LANES - N_HEADS)).reshape(1, LANES)
    s5_raw = (ssm_a_re[l], ssm_a_im[l], ssm_log_step[l], ssm_b_re[l], ssm_b_im[l],
              ssm_c_re[l], ssm_c_im[l], ssm_d[l])
    return (g_norm_mix[l].reshape(1, D_MODEL), w_main, w_f, b,
            g_q[l].reshape(1, HEAD_DIM), g_k[l].reshape(1, HEAD_DIM), s5_raw,
            w_glu[l].astype(BF16), g_out_ssm[l].reshape(1, D_SSM), g_out_attn[l].reshape(1, D_ATTN),
            w_out[l].astype(BF16), g_norm_mlp[l].reshape(1, D_MODEL),
            w_up[l].astype(BF16), w_down[l].astype(BF16))


PROMPT_TILES = dict(proj=512, s5=512, chunk=32, bq=1024, mlp=512, tf=1024)
SAMPLE_TILES = dict(proj=512, s5=512, mlp=512, tf=1024)


def kernel(x_prompt, x_sample, cache_k, cache_v, cache_logf, state_ssm_re, state_ssm_im,
           g_norm_mix, w_in, b_f, ssm_a_re, ssm_a_im, ssm_log_step, ssm_b_re, ssm_b_im,
           ssm_c_re, ssm_c_im, ssm_d, w_glu, g_q, g_k, g_out_ssm, g_out_attn, w_out,
           g_norm_mlp, w_up, w_down):
    depth = w_in.shape[0]
    y_p, y_s = x_prompt, x_sample
    outs_p, outs_s = [], []
    for l in range(depth):
        weights = _prep_weights(l, g_norm_mix, w_in, b_f, ssm_a_re, ssm_a_im, ssm_log_step,
                                ssm_b_re, ssm_b_im, ssm_c_re, ssm_c_im, ssm_d, w_glu, g_q, g_k,
                                g_out_ssm, g_out_attn, w_out, g_norm_mlp, w_up, w_down)
        n_p, l_p = y_p.shape[0], y_p.shape[1]
        y_p, *rest_p = _layer(y_p, weights, n_seq=n_p, seq_len=l_p, tiles=PROMPT_TILES)
        n_s, l_s = y_s.shape[0], y_s.shape[1]
        h0 = _state_from_gp(state_ssm_re[l], state_ssm_im[l])
        y_s, *rest_s = _layer(y_s, weights, n_seq=n_s, seq_len=l_s,
                              cache=(cache_k[l], cache_v[l], cache_logf[l]), h0=h0,
                              tiles=SAMPLE_TILES)
        outs_p.append(rest_p)
        outs_s.append(rest_s)
    stack = lambda outs, idx: jnp.stack([o[idx] for o in outs])
    return (y_p, y_s,
            stack(outs_p, 0), stack(outs_p, 1), stack(outs_p, 2), stack(outs_p, 3), stack(outs_p, 4),
            stack(outs_s, 0), stack(outs_s, 1), stack(outs_s, 2), stack(outs_s, 3), stack(outs_s, 4))
```

```python
import functools
import math

import jax
import jax.numpy as jnp
from jax import lax
from jax.experimental import pallas as pl
from jax.experimental.pallas import tpu as pltpu

D_MODEL = 2048
D_SSM = 1024
SSM_GROUP = 16
N_SSM_GROUPS = 64
SSM_STATE = 64
D_ATTN = 1024
HEAD_DIM = 128
N_HEADS = 8
D_FF = 8192
EPS = 1e-6
NEG_INF = -1e30
LOG2E = math.log2(math.e)
MAX_FIXED_SHIFT = 45.0
EXP2_ZERO = -150.0

LANES = 128
SLAB = 256
N_SLABS = D_SSM // SLAB
SLAB_GROUPS = SLAB // SSM_GROUP
SLAB_STATE = SLAB_GROUPS * SSM_STATE
V7X_VMEM_BYTES = 64 * 1024 * 1024
VMEM_LIMIT = V7X_VMEM_BYTES - 2 * 1024 * 1024

F32 = jnp.float32
BF16 = jnp.bfloat16


def _params(sem, vmem=VMEM_LIMIT):
    return pltpu.CompilerParams(dimension_semantics=sem, vmem_limit_bytes=vmem)


def _rms(x, g):
    return x * lax.rsqrt(jnp.mean(x * x, axis=-1, keepdims=True) + EPS) * g


def _in_proj_kernel(x_ref, g_ref, w_ref, wf_ref, bf_ref, gq_ref, gk_ref,
                    u_ref, qb_ref, k_ref, kb_ref, v_ref, vb_ref, lf_ref, lft_ref):
    hb = _rms(x_ref[...], g_ref[...]).astype(BF16)

    u_ref[...] = jnp.dot(hb, w_ref[:, 0:D_SSM], preferred_element_type=F32)

    q = jnp.dot(hb, w_ref[:, D_SSM:D_SSM + D_ATTN], preferred_element_type=F32)
    gq = gq_ref[...]
    for h in range(N_HEADS):
        sl = slice(h * HEAD_DIM, (h + 1) * HEAD_DIM)
        qb_ref[:, sl] = (_rms(q[:, sl], gq) * (HEAD_DIM ** -0.5 * LOG2E)).astype(BF16)

    k = jnp.dot(hb, w_ref[:, D_SSM + D_ATTN:D_SSM + 2 * D_ATTN], preferred_element_type=F32)
    gk = gk_ref[...]
    for h in range(N_HEADS):
        sl = slice(h * HEAD_DIM, (h + 1) * HEAD_DIM)
        kn = _rms(k[:, sl], gk)
        k_ref[:, sl] = kn
        kb_ref[:, sl] = kn.astype(BF16)

    v = jnp.dot(hb, w_ref[:, D_SSM + 2 * D_ATTN:D_SSM + 3 * D_ATTN], preferred_element_type=F32)
    v_ref[...] = v
    vb_ref[...] = v.astype(BF16)

    zf = jnp.dot(hb, wf_ref[...], preferred_element_type=F32) + bf_ref[...]
    lf = jnp.minimum(zf, 0.0) - jnp.log1p(jnp.exp(-jnp.abs(zf)))
    lf_ref[...] = lf[:, :N_HEADS]
    lft_ref[...] = jnp.transpose(lf)[:N_HEADS, :]


def _in_proj(x, g, w_main, w_f, b_f, g_q, g_k, *, tt):
    t = x.shape[0]
    assert t % tt == 0
    row = lambda i: (i, 0)
    const = lambda i: (0, 0)
    once = pl.Buffered(1)
    wide = lambda dt: jax.ShapeDtypeStruct((t, D_ATTN), dt)
    return pl.pallas_call(
        _in_proj_kernel,
        grid=(t // tt,),
        in_specs=[
            pl.BlockSpec((tt, D_MODEL), row),
            pl.BlockSpec((1, D_MODEL), const),
            pl.BlockSpec((D_MODEL, D_SSM + 3 * D_ATTN), const, pipeline_mode=once),
            pl.BlockSpec((D_MODEL, LANES), const, pipeline_mode=once),
            pl.BlockSpec((1, LANES), const),
            pl.BlockSpec((1, HEAD_DIM), const),
            pl.BlockSpec((1, HEAD_DIM), const),
        ],
        out_specs=[
            pl.BlockSpec((tt, D_SSM), row),
            pl.BlockSpec((tt, D_ATTN), row),
            pl.BlockSpec((tt, D_ATTN), row),
            pl.BlockSpec((tt, D_ATTN), row),
            pl.BlockSpec((tt, D_ATTN), row),
            pl.BlockSpec((tt, D_ATTN), row),
            pl.BlockSpec((tt, N_HEADS), row),
            pl.BlockSpec((N_HEADS, tt), lambda i: (0, i)),
        ],
        out_shape=[wide(F32), wide(BF16), wide(F32), wide(BF16), wide(F32), wide(BF16),
                   jax.ShapeDtypeStruct((t, N_HEADS), F32),
                   jax.ShapeDtypeStruct((N_HEADS, t), F32)],
        compiler_params=_params(("arbitrary",)),
        name="in_proj",
    )(x, g, w_main, w_f, b_f, g_q, g_k)


def _cumsum_kernel(x_ref, o_ref):
    rows, length = x_ref.shape
    lane = lax.broadcasted_iota(jnp.int32, (rows, LANES), 1)
    carry = jnp.zeros((rows, 1), F32)
    for b in range(length // LANES):
        sl = slice(b * LANES, (b + 1) * LANES)
        x = x_ref[:, sl]
        d = 1
        while d < LANES:
            x = x + jnp.where(lane >= d, pltpu.roll(x, d, axis=1), 0.0)
            d *= 2
        x = x + carry
        o_ref[:, sl] = x * LOG2E
        carry = x[:, LANES - 1:LANES]


def _cumsum_lanes(x):
    rows, length = x.shape
    assert length % LANES == 0 and rows % 8 == 0
    return pl.pallas_call(
        _cumsum_kernel,
        out_shape=jax.ShapeDtypeStruct((rows, length), F32),
        name="cumsum",
    )(x)


def _s5_tables(a_re, a_im, log_step, b_re, b_im, c_re, c_im, d, tc):
    step = jnp.exp(log_step)[:, None]
    mag = jnp.exp(a_re * step)
    abar_re = mag * jnp.cos(a_im * step)
    abar_im = mag * jnp.sin(a_im * step)
    den = a_re * a_re + a_im * a_im
    nr = abar_re - 1.0
    ni = abar_im
    fr = (nr * a_re + ni * a_im) / den
    fi = (ni * a_re - nr * a_im) / den
    bbar_re = fr[..., None] * b_re - fi[..., None] * b_im
    bbar_im = fr[..., None] * b_im + fi[..., None] * b_re
    row_group = lambda n, per: (lax.broadcasted_iota(jnp.int32, (n, 1), 0) // per)
    col_group = lambda n, per: (lax.broadcasted_iota(jnp.int32, (1, n), 1) // per)

    def in_blockdiag(b):
        b = b.reshape(N_SLABS, SLAB_GROUPS, SSM_STATE, SSM_GROUP).transpose(0, 1, 3, 2)
        b = jnp.tile(b.reshape(N_SLABS, SLAB, SSM_STATE), (1, 1, SLAB_GROUPS))
        keep = row_group(SLAB, SSM_GROUP) == col_group(SLAB_STATE, SSM_STATE)
        return jnp.where(keep[None], b, 0.0)

    def out_blockdiag(c):
        c = c.reshape(N_SLABS, SLAB_GROUPS, SSM_GROUP, SSM_STATE).transpose(0, 1, 3, 2)
        c = jnp.tile(c.reshape(N_SLABS, SLAB_STATE, SSM_GROUP), (1, 1, SLAB_GROUPS))
        keep = row_group(SLAB_STATE, SSM_STATE) == col_group(SLAB, SSM_GROUP)
        return jnp.where(keep[None], c, 0.0)

    bb = jnp.concatenate([in_blockdiag(bbar_re), in_blockdiag(bbar_im)], axis=-1).astype(BF16)
    cc = jnp.concatenate([out_blockdiag(c_re), out_blockdiag(-c_im)], axis=1).astype(BF16)
    n = jnp.arange(1, tc + 1, dtype=F32)[:, None, None]
    pmag = jnp.exp(a_re[None] * step[None] * n)
    ang = a_im[None] * step[None] * n
    p_re = (pmag * jnp.cos(ang)).reshape(tc, N_SLABS, SLAB_STATE).transpose(1, 0, 2)
    p_im = (pmag * jnp.sin(ang)).reshape(tc, N_SLABS, SLAB_STATE).transpose(1, 0, 2)
    pw = jnp.concatenate([p_re, p_im], axis=-1)
    dd = d.reshape(N_SLABS, 1, SLAB)
    return bb, cc, pw, dd


def _s5_kernel(u_ref, perm_ref, bb_ref, cc_ref, pw_ref, d_ref, h0_ref, y_ref, hfin_ref,
               st_s, hb_s, g_s, carry_s, *, tc, nc, tiles_per_seq, carry_mode):
    i = pl.program_id(1)
    u = u_ref[...]
    u_hi = u.astype(BF16)
    u_lo = (u - u_hi.astype(F32)).astype(BF16)
    up = jnp.dot(perm_ref[...], jnp.concatenate([u_hi, u_lo], axis=1),
                 preferred_element_type=F32)
    u_hi = up[:, :SLAB]
    u = u_hi + up[:, SLAB:]
    uh = u_hi.astype(BF16)
    y_ref[...] = d_ref[...] * u
    blk = 2 * LANES
    nblk = SLAB_STATE // blk
    grp = max(1, 16 // nc)
    lanes = lambda b: slice(b * blk, (b + 1) * blk)
    step_rows = lambda t: slice(t * nc, (t + 1) * nc)

    if carry_mode:
        @pl.when(i % tiles_per_seq == 0)
        def _():
            carry_s[...] = jnp.zeros_like(carry_s)
    else:
        g_s[...] = h0_ref[...]

    for b in range(nblk):
        re, im = lanes(b), lanes(nblk + b)
        st_s[:, re] = jnp.dot(uh, bb_ref[:, re], preferred_element_type=F32)
        st_s[:, im] = jnp.dot(uh, bb_ref[:, im], preferred_element_type=F32)
        ar = pw_ref[0:1, re]
        ai = pw_ref[0:1, im]
        hr = jnp.zeros((nc, blk), F32)
        hi = jnp.zeros((nc, blk), F32)
        for t in range(tc):
            rows = step_rows(t)
            hr, hi = (ar * hr - ai * hi + st_s[rows, re],
                      ar * hi + ai * hr + st_s[rows, im])
            st_s[rows, re] = hr
            st_s[rows, im] = hi

        ar = pw_ref[tc - 1:tc, re]
        ai = pw_ref[tc - 1:tc, im]
        if carry_mode:
            gr = carry_s[0:1, re]
            gi = carry_s[0:1, im]
            for c in range(nc):
                g_s[c:c + 1, re] = gr
                g_s[c:c + 1, im] = gi
                gr, gi = (ar * gr - ai * gi + hr[c:c + 1, :],
                          ar * gi + ai * gr + hi[c:c + 1, :])
            carry_s[0:1, re] = gr
            carry_s[0:1, im] = gi
            hfin_ref[0:1, re] = gr
            hfin_ref[0:1, im] = gi
            gr = g_s[:, re]
            gi = g_s[:, im]
        else:
            gr = g_s[:, re]
            gi = g_s[:, im]
            hfin_ref[:, re] = ar * gr - ai * gi + hr
            hfin_ref[:, im] = ar * gi + ai * gr + hi

        for t0 in range(0, tc, grp):
            full_r, full_i = [], []
            for t in range(t0, t0 + grp):
                rows = step_rows(t)
                pr = pw_ref[t:t + 1, re]
                pi = pw_ref[t:t + 1, im]
                full_r.append(st_s[rows, re] + (pr * gr - pi * gi))
                full_i.append(st_s[rows, im] + (pr * gi + pi * gr))
            rows = slice(t0 * nc, (t0 + grp) * nc)
            hb_s[rows, re] = jnp.concatenate(full_r, axis=0).astype(BF16)
            hb_s[rows, im] = jnp.concatenate(full_i, axis=0).astype(BF16)

        y_ref[...] += (jnp.dot(hb_s[:, re], cc_ref[re, :], preferred_element_type=F32)
                       + jnp.dot(hb_s[:, im], cc_ref[im, :], preferred_element_type=F32))


def _step_major_perm(tt, tc):
    nc = tt // tc
    r_out = jnp.arange(tt)
    r_in = (r_out % nc) * tc + r_out // nc
    return (r_in[:, None] == jnp.arange(tt)[None, :]).astype(BF16)


def _s5(u, tables, perm, h0, *, tt, tc, seq_len, carry_mode):
    bb, cc, pw, dd = tables
    t = u.shape[0]
    nc = tt // tc
    n_tiles = t // tt
    assert t % tt == 0 and tt % tc == 0
    if carry_mode:
        assert seq_len % tt == 0
        tiles_per_seq = seq_len // tt
        n_seq = t // seq_len
        hfin_shape = (N_SLABS, n_seq, 1, 2 * SLAB_STATE)
        hfin_spec = pl.BlockSpec((None, None, 1, 2 * SLAB_STATE),
                                 lambda s, i: (s, i // tiles_per_seq, 0, 0))
        h0 = jnp.zeros((N_SLABS, 8, 2 * SLAB_STATE), F32)
        h0_spec = pl.BlockSpec((None, 8, 2 * SLAB_STATE), lambda s, i: (s, 0, 0))
    else:
        assert seq_len == tc
        tiles_per_seq = 1
        hfin_shape = (N_SLABS, n_tiles, nc, 2 * SLAB_STATE)
        hfin_spec = pl.BlockSpec((None, None, nc, 2 * SLAB_STATE), lambda s, i: (s, i, 0, 0))
        h0_spec = pl.BlockSpec((None, nc, 2 * SLAB_STATE), lambda s, i: (s, i, 0))
    kern = functools.partial(_s5_kernel, tc=tc, nc=nc, tiles_per_seq=tiles_per_seq,
                             carry_mode=carry_mode)
    y, hfin = pl.pallas_call(
        kern,
        grid=(N_SLABS, n_tiles),
        in_specs=[
            pl.BlockSpec((tt, SLAB), lambda s, i: (i, s)),
            pl.BlockSpec((tt, tt), lambda s, i: (0, 0), pipeline_mode=pl.Buffered(1)),
            pl.BlockSpec((None, SLAB, 2 * SLAB_STATE), lambda s, i: (s, 0, 0)),
            pl.BlockSpec((None, 2 * SLAB_STATE, SLAB), lambda s, i: (s, 0, 0)),
            pl.BlockSpec((None, tc, 2 * SLAB_STATE), lambda s, i: (s, 0, 0)),
            pl.BlockSpec((None, 1, SLAB), lambda s, i: (s, 0, 0)),
            h0_spec,
        ],
        out_specs=[pl.BlockSpec((tt, SLAB), lambda s, i: (i, s)), hfin_spec],
        out_shape=[jax.ShapeDtypeStruct((t, D_SSM), F32),
                   jax.ShapeDtypeStruct(hfin_shape, F32)],
        scratch_shapes=[pltpu.VMEM((tt, 2 * SLAB_STATE), F32),
                        pltpu.VMEM((tt, 2 * SLAB_STATE), BF16),
                        pltpu.VMEM((nc, 2 * SLAB_STATE), F32),
                        pltpu.VMEM((8, 2 * SLAB_STATE), F32)],
        compiler_params=_params(("arbitrary", "arbitrary")),
        name="s5",
    )(u, perm, bb, cc, pw, dd, h0)
    return y, hfin.reshape(N_SLABS, -1, 2 * SLAB_STATE)


def _state_to_gp(hfin):
    n = hfin.shape[1]
    h = hfin.transpose(1, 0, 2)
    re = h[..., :SLAB_STATE].reshape(n, N_SSM_GROUPS, SSM_STATE)
    im = h[..., SLAB_STATE:].reshape(n, N_SSM_GROUPS, SSM_STATE)
    return re, im


def _state_from_gp(re, im):
    n = re.shape[0]
    h = jnp.concatenate([re.reshape(n, N_SLABS, SLAB_STATE), im.reshape(n, N_SLABS, SLAB_STATE)],
                        axis=-1)
    return h.transpose(1, 0, 2)


def _col_from_row(row):
    n = row.shape[1]
    r = lax.broadcasted_iota(jnp.int32, (n, n), 0)
    c = lax.broadcasted_iota(jnp.int32, (n, n), 1)
    return jnp.sum(jnp.where(r == c, jnp.broadcast_to(row, (n, n)), 0.0), axis=1, keepdims=True)


def _attn_prompt_kernel(first_ref, q_ref, k_ref, v_ref, cq_ref, ck_ref, o_ref,
                        m_s, l_s, acc_s, cq_s, s_s, *, bq, online):
    i = pl.program_id(1)
    bk = bq // 2
    cq_s[...] = jnp.transpose(jnp.broadcast_to(cq_ref[...], (LANES, bq)))
    if online:
        m_s[...] = jnp.full_like(m_s, NEG_INF)
    l_s[...] = jnp.zeros_like(l_s)
    acc_s[...] = jnp.zeros_like(acc_s)
    nchunk = bk // LANES
    all_rows = slice(0, bq)

    def scores(j, slot, rows=all_rows):
        start = pl.multiple_of(j * bk, bk)
        s_s[slot, rows, :] = lax.dot_general(q_ref[rows, :], k_ref[pl.ds(start, bk), :],
                                             (((1,), (1,)), ((), ())), preferred_element_type=F32)

    def reduce_block(j, slot, rows=all_rows, causal=False):
        start = pl.multiple_of(j * bk, bk)
        v = v_ref[pl.ds(start, bk), :]
        nrows = rows.stop - rows.start

        def logits(c):
            col0 = pl.multiple_of(start + c * LANES, LANES)
            t = s_s[slot, rows, c * LANES:(c + 1) * LANES] - ck_ref[:, pl.ds(col0, LANES)]
            if causal:
                r = lax.broadcasted_iota(jnp.int32, (nrows, LANES), 0)
                col = lax.broadcasted_iota(jnp.int32, (nrows, LANES), 1) + c * LANES
                t = jnp.where(col <= r, t, NEG_INF)
            return t

        if online:
            m_loc = functools.reduce(jnp.maximum, [logits(c) for c in range(nchunk)])
            m_old = m_s[rows, :]
            m_new = jnp.maximum(m_old, jnp.max(m_loc, axis=1, keepdims=True) + cq_s[rows, :])
            m_s[rows, :] = m_new
            alpha = jnp.exp2(m_old - m_new)
            shift = cq_s[rows, :] - m_new
        else:
            shift = cq_s[rows, :]
        ps = [jnp.exp2(logits(c) + shift) for c in range(nchunk)]
        p = jnp.concatenate([pc.astype(BF16) for pc in ps], axis=1)
        l_new = functools.reduce(jnp.add, ps)
        acc_new = jnp.dot(p, v, preferred_element_type=F32)
        if online:
            l_s[rows, :] = alpha * l_s[rows, :] + l_new
            acc_s[rows, :] = alpha * acc_s[rows, :] + acc_new
        else:
            l_s[rows, :] += l_new
            acc_s[rows, :] += acc_new

    first = first_ref[pl.program_id(0) * pl.num_programs(1) + i]
    scores(2 * first, 0)

    def body(p, carry):
        j = 2 * p
        scores(j + 1, 1)
        reduce_block(j, 0)
        scores(j + 2, 0)
        reduce_block(j + 1, 1)
        return carry

    lax.fori_loop(first, i, body, 0)
    late_rows = slice(bk, bq)
    scores(2 * i + 1, 1, late_rows)
    reduce_block(2 * i, 0, causal=True)
    reduce_block(2 * i + 1, 1, late_rows, causal=True)

    o_ref[...] = acc_s[...] / jnp.sum(l_s[...], axis=1, keepdims=True)


def _attn_prompt(qb, kb, vb, first, cq_rows, ck_rows, *, n_seq, seq_len, bq, online):
    nq = seq_len // bq
    assert seq_len % bq == 0
    kern = functools.partial(_attn_prompt_kernel, bq=bq, online=online)
    q_map = lambda g, i, first: ((g // N_HEADS) * nq + i, g % N_HEADS)
    kv_map = lambda g, i, first: (g // N_HEADS, g % N_HEADS)
    grid_spec = pltpu.PrefetchScalarGridSpec(
        num_scalar_prefetch=1,
        grid=(n_seq * N_HEADS, nq),
        in_specs=[
            pl.BlockSpec((bq, HEAD_DIM), q_map),
            pl.BlockSpec((seq_len, HEAD_DIM), kv_map),
            pl.BlockSpec((seq_len, HEAD_DIM), kv_map),
            pl.BlockSpec((None, 1, bq), lambda g, i, first: (g, 0, i)),
            pl.BlockSpec((None, 1, seq_len), lambda g, i, first: (g, 0, 0)),
        ],
        out_specs=pl.BlockSpec((bq, HEAD_DIM), q_map),
        scratch_shapes=[pltpu.VMEM((bq, LANES), F32), pltpu.VMEM((bq, LANES), F32),
                        pltpu.VMEM((bq, HEAD_DIM), F32), pltpu.VMEM((bq, LANES), F32),
                        pltpu.VMEM((2, bq, bq // 2), F32)],
    )
    return pl.pallas_call(
        kern,
        grid_spec=grid_spec,
        out_shape=jax.ShapeDtypeStruct((n_seq * seq_len, D_ATTN), F32),
        compiler_params=_params(("arbitrary", "arbitrary")),
        name="attn_prompt",
    )(first, qb, kb, vb, cq_rows, ck_rows)


def _first_live_pair(c_rows, bq):
    g, _, length = c_rows.shape
    nq = length // bq
    blocks = c_rows.reshape(g, nq, bq)
    hi = jnp.max(blocks, axis=-1)
    lo = jnp.min(blocks, axis=-1)
    dead = (hi[:, :, None] - lo[:, None, :]) < EXP2_ZERO - 1.0
    pair = jnp.arange(nq, dtype=jnp.int32)
    dead = dead & (pair[None, None, :] < pair[None, :, None])
    first = jnp.min(jnp.where(dead, nq, pair[None, None, :]), axis=-1)
    return first.reshape(g * nq).astype(jnp.int32)


def _attn_sample_kernel(q_ref, kn_ref, vn_ref, ck_ref, cv_ref, c_ref, o_ref, *, past, s_new):
    r = lax.broadcasted_iota(jnp.int32, (s_new, s_new), 0)
    col = lax.broadcasted_iota(jnp.int32, (s_new, s_new), 1)
    causal = col <= r
    nt = (((1,), (1,)), ((), ()))
    for h in range(N_HEADS):
        sl = slice(h * HEAD_DIM, (h + 1) * HEAD_DIM)
        q = q_ref[:, sl]
        c_past = c_ref[h:h + 1, 0:past]
        c_new = c_ref[h:h + 1, past:past + s_new]
        cq = _col_from_row(c_new)
        head_rows = pl.ds(h, past, stride=N_HEADS)
        s_p = lax.dot_general(q, ck_ref[head_rows, :].astype(BF16), nt, preferred_element_type=F32)
        s_p = s_p + cq - c_past
        s_n = lax.dot_general(q, kn_ref[:, sl], nt, preferred_element_type=F32)
        s_n = jnp.where(causal, s_n + cq - c_new, NEG_INF)
        m = jnp.maximum(jnp.max(s_p, axis=1, keepdims=True), jnp.max(s_n, axis=1, keepdims=True))
        p_p = jnp.exp2(s_p - m)
        p_n = jnp.exp2(s_n - m)
        l = jnp.sum(p_p, axis=1, keepdims=True) + jnp.sum(p_n, axis=1, keepdims=True)
        acc = (jnp.dot(p_p.astype(BF16), cv_ref[head_rows, :].astype(BF16), preferred_element_type=F32)
               + jnp.dot(p_n.astype(BF16), vn_ref[:, sl], preferred_element_type=F32))
        o_ref[:, sl] = acc / l


def _attn_sample(qb, kb, vb, cache_k, cache_v, c_all, *, n_seq, s_new, past):
    kern = functools.partial(_attn_sample_kernel, past=past, s_new=s_new)
    new = pl.BlockSpec((s_new, D_ATTN), lambda b: (b, 0))
    cache_k = cache_k.reshape(n_seq, past * N_HEADS, HEAD_DIM)
    cache_v = cache_v.reshape(n_seq, past * N_HEADS, HEAD_DIM)
    cache = pl.BlockSpec((None, past * N_HEADS, HEAD_DIM), lambda b: (b, 0, 0))
    return pl.pallas_call(
        kern,
        grid=(n_seq,),
        in_specs=[new, new, new, cache, cache,
                  pl.BlockSpec((N_HEADS, c_all.shape[1]), lambda b: (b, 0))],
        out_specs=new,
        out_shape=jax.ShapeDtypeStruct((n_seq * s_new, D_ATTN), F32),
        compiler_params=_params(("arbitrary",)),
        name="attn_sample",
    )(qb, kb, vb, cache_k, cache_v, c_all)


def _gelu_tanh(x):
    return 0.5 * x * (1.0 + jnp.tanh(math.sqrt(2.0 / math.pi) * (x + 0.044715 * (x * x * x))))


def _mix_kernel(x_ref, y_ref, a_ref, unperm_ref, wglu_ref, gs_ref, ga_ref, wout_ref, gm_ref,
                x1_ref, hm_ref):
    gy = _gelu_tanh(y_ref[...])
    gate = jax.nn.sigmoid(jnp.dot(gy.astype(BF16), wglu_ref[...], preferred_element_type=F32))
    ssm = _rms(gy * gate, gs_ref[...]).astype(BF16)
    ssm = jnp.dot(unperm_ref[...], ssm, preferred_element_type=F32).astype(BF16)
    att = _rms(a_ref[...], ga_ref[...]).astype(BF16)
    x1 = (x_ref[...]
          + jnp.dot(ssm, wout_ref[0:D_SSM, :], preferred_element_type=F32)
          + jnp.dot(att, wout_ref[D_SSM:D_SSM + D_ATTN, :], preferred_element_type=F32))
    x1_ref[...] = x1
    hm_ref[...] = _rms(x1, gm_ref[...]).astype(BF16)


def _mix(x, y, attn, unperm, w_glu, g_ssm, g_attn, w_out, g_mlp, *, tt):
    t = x.shape[0]
    assert t % tt == 0 and unperm.shape == (tt, tt)
    row = lambda i: (i, 0)
    const = lambda i: (0, 0)
    once = pl.Buffered(1)
    return pl.pallas_call(
        _mix_kernel,
        grid=(t // tt,),
        in_specs=[
            pl.BlockSpec((tt, D_MODEL), row),
            pl.BlockSpec((tt, D_SSM), row),
            pl.BlockSpec((tt, D_ATTN), row),
            pl.BlockSpec((tt, tt), const, pipeline_mode=once),
            pl.BlockSpec((D_SSM, D_SSM), const, pipeline_mode=once),
            pl.BlockSpec((1, D_SSM), const),
            pl.BlockSpec((1, D_ATTN), const),
            pl.BlockSpec((D_SSM + D_ATTN, D_MODEL), const, pipeline_mode=once),
            pl.BlockSpec((1, D_MODEL), const),
        ],
        out_specs=[pl.BlockSpec((tt, D_MODEL), row), pl.BlockSpec((tt, D_MODEL), row)],
        out_shape=[jax.ShapeDtypeStruct((t, D_MODEL), F32), jax.ShapeDtypeStruct((t, D_MODEL), BF16)],
        compiler_params=_params(("arbitrary",)),
        name="mix",
    )(x, y, attn, unperm, w_glu, g_ssm, g_attn, w_out, g_mlp)


def _mlp_kernel(x1_ref, hm_ref, wup_ref, wdn_ref, o_ref):
    j = pl.program_id(1)

    @pl.when(j == 0)
    def _():
        o_ref[...] = x1_ref[...]

    a = jnp.maximum(jnp.dot(hm_ref[...], wup_ref[...], preferred_element_type=F32), 0.0)
    o_ref[...] += jnp.dot((a * a).astype(BF16), wdn_ref[...], preferred_element_type=F32)


def _mlp(x1, hm, w_up, w_down, *, tt, tf):
    t = x1.shape[0]
    assert t % tt == 0 and D_FF % tf == 0
    return pl.pallas_call(
        _mlp_kernel,
        grid=(t // tt, D_FF // tf),
        in_specs=[
            pl.BlockSpec((tt, D_MODEL), lambda i, j: (i, 0)),
            pl.BlockSpec((tt, D_MODEL), lambda i, j: (i, 0)),
            pl.BlockSpec((D_MODEL, tf), lambda i, j: (0, j)),
            pl.BlockSpec((tf, D_MODEL), lambda i, j: (j, 0)),
        ],
        out_specs=pl.BlockSpec((tt, D_MODEL), lambda i, j: (i, 0)),
        out_shape=jax.ShapeDtypeStruct((t, D_MODEL), F32),
        compiler_params=_params(("arbitrary", "arbitrary")),
        name="mlp",
    )(x1, hm, w_up, w_down)


def _layer(x, weights, *, n_seq, seq_len, cache=None, h0=None, tiles):
    (g_mix, w_main, w_f, b_f, g_q, g_k, s5_raw, w_glu, g_ssm, g_attn, w_out, g_mlp,
     w_up, w_down) = weights
    t = n_seq * seq_len
    x2 = x.reshape(t, D_MODEL)
    u, qb, k, kb, v, vb, logf, lf_t = _in_proj(x2, g_mix, w_main, w_f, b_f, g_q, g_k,
                                               tt=tiles["proj"])
    lf_rows = lf_t.reshape(N_HEADS, n_seq, seq_len).transpose(1, 0, 2).reshape(n_seq * N_HEADS,
                                                                               seq_len)

    tc = tiles["chunk"] if cache is None else seq_len
    perm = _step_major_perm(tiles["s5"], tc)
    if cache is None:
        tables = _s5_tables(*s5_raw, tc=tc)
        y, hfin = _s5(u, tables, perm, None, tt=tiles["s5"], tc=tc, seq_len=seq_len,
                      carry_mode=True)
        c_rows = _cumsum_lanes(lf_rows).reshape(n_seq * N_HEADS, 1, seq_len)
        attend = functools.partial(_attn_prompt, qb, kb, vb, n_seq=n_seq, seq_len=seq_len,
                                   bq=tiles["bq"])
        bound = (HEAD_DIM ** -0.5 * LOG2E) * HEAD_DIM * jnp.max(jnp.abs(g_q)) * jnp.max(jnp.abs(g_k))
        attn = lax.cond(
            bound <= MAX_FIXED_SHIFT,
            lambda: attend(_first_live_pair(c_rows, tiles["bq"]), c_rows - bound, c_rows,
                           online=False),
            lambda: attend(jnp.zeros((c_rows.shape[0] * (seq_len // tiles["bq"]),), jnp.int32),
                           c_rows, c_rows, online=True))
    else:
        cache_k, cache_v, cache_logf = cache
        past = cache_k.shape[1]
        tables = _s5_tables(*s5_raw, tc=tc)
        y, hfin = _s5(u, tables, perm, h0, tt=tiles["s5"], tc=tc, seq_len=seq_len,
                      carry_mode=False)
        past_rows = cache_logf.transpose(0, 2, 1).reshape(n_seq * N_HEADS, past)
        total = past + seq_len
        padded = -(-total // LANES) * LANES
        lf_all = jnp.concatenate(
            [past_rows, lf_rows, jnp.zeros((n_seq * N_HEADS, padded - total), F32)], axis=1)
        c_all = _cumsum_lanes(lf_all)
        attn = _attn_sample(qb, kb, vb, cache_k, cache_v, c_all,
                            n_seq=n_seq, s_new=seq_len, past=past)

    x1, hm = _mix(x2, y, attn, perm.T, w_glu, g_ssm, g_attn, w_out, g_mlp, tt=tiles["s5"])
    out = _mlp(x1, hm, w_up, w_down, tt=tiles["mlp"], tf=tiles["tf"])
    h_re, h_im = _state_to_gp(hfin)
    return (out.reshape(n_seq, seq_len, D_MODEL),
            k.reshape(n_seq, seq_len, N_HEADS, HEAD_DIM),
            v.reshape(n_seq, seq_len, N_HEADS, HEAD_DIM),
            logf.reshape(n_seq, seq_len, N_HEADS), h_re, h_im)


def _cast_kernel(w_ref, o_ref):
    o_ref[...] = w_ref[...].astype(o_ref.dtype)


def _cast_leading_cols(w, l, n_cols, rows_per_step=256):
    _, rows, _ = w.shape
    assert rows % rows_per_step == 0 and n_cols % LANES == 0
    return pl.pallas_call(
        _cast_kernel,
        grid=(rows // rows_per_step,),
        in_specs=[pl.BlockSpec((None, rows_per_step, n_cols), lambda i: (l, i, 0))],
        out_specs=pl.BlockSpec((rows_per_step, n_cols), lambda i: (i, 0)),
        out_shape=jax.ShapeDtypeStruct((rows, n_cols), BF16),
        compiler_params=_params(("arbitrary",)),
        name="cast_w_in",
    )(w)


def _prep_weights(l, g_norm_mix, w_in, b_f, ssm_a_re, ssm_a_im, ssm_log_step, ssm_b_re, ssm_b_im,
                  ssm_c_re, ssm_c_im, ssm_d, w_glu, g_q, g_k, g_out_ssm, g_out_attn, w_out,
                  g_norm_mlp, w_up, w_down):
    n_main = D_SSM + 3 * D_ATTN
    w_main = _cast_leading_cols(w_in, l, n_main)
    w_f = jnp.pad(w_in[l][:, n_main:], ((0, 0), (0, LANES - N_HEADS))).astype(BF16)
    b = jnp.pad(b_f[l], (0, LANES - N_HEADS)).reshape(1, LANES)
    s5_raw = (ssm_a_re[l], ssm_a_im[l], ssm_log_step[l], ssm_b_re[l], ssm_b_im[l],
              ssm_c_re[l], ssm_c_im[l], ssm_d[l])
    return (g_norm_mix[l].reshape(1, D_MODEL), w_main, w_f, b,
            g_q[l].reshape(1, HEAD_DIM), g_k[l].reshape(1, HEAD_DIM), s5_raw,
            w_glu[l].astype(BF16), g_out_ssm[l].reshape(1, D_SSM), g_out_attn[l].reshape(1, D_ATTN),
            w_out[l].astype(BF16), g_norm_mlp[l].reshape(1, D_MODEL),
            w_up[l].astype(BF16), w_down[l].astype(BF16))


PROMPT_TILES = dict(proj=512, s5=512, chunk=32, bq=1024, mlp=512, tf=2048)
SAMPLE_TILES = dict(proj=512, s5=512, mlp=512, tf=2048)


def kernel(x_prompt, x_sample, cache_k, cache_v, cache_logf, state_ssm_re, state_ssm_im,
           g_norm_mix, w_in, b_f, ssm_a_re, ssm_a_im, ssm_log_step, ssm_b_re, ssm_b_im,
           ssm_c_re, ssm_c_im, ssm_d, w_glu, g_q, g_k, g_out_ssm, g_out_attn, w_out,
           g_norm_mlp, w_up, w_down):
    depth = w_in.shape[0]
    y_p, y_s = x_prompt, x_sample
    outs_p, outs_s = [], []
    for l in range(depth):
        weights = _prep_weights(l, g_norm_mix, w_in, b_f, ssm_a_re, ssm_a_im, ssm_log_step,
                                ssm_b_re, ssm_b_im, ssm_c_re, ssm_c_im, ssm_d, w_glu, g_q, g_k,
                                g_out_ssm, g_out_attn, w_out, g_norm_mlp, w_up, w_down)
        n_p, l_p = y_p.shape[0], y_p.shape[1]
        y_p, *rest_p = _layer(y_p, weights, n_seq=n_p, seq_len=l_p, tiles=PROMPT_TILES)
        n_s, l_s = y_s.shape[0], y_s.shape[1]
        h0 = _state_from_gp(state_ssm_re[l], state_ssm_im[l])
        y_s, *rest_s = _layer(y_s, weights, n_seq=n_s, seq_len=l_s,
                              cache=(cache_k[l], cache_v[l], cache_logf[l]), h0=h0,
                              tiles=SAMPLE_TILES)
        outs_p.append(rest_p)
        outs_s.append(rest_s)
    stack = lambda outs, idx: jnp.stack([o[idx] for o in outs])
    return (y_p, y_s,
            stack(outs_p, 0), stack(outs_p, 1), stack(outs_p, 2), stack(outs_p, 3), stack(outs_p, 4),
            stack(outs_s, 0), stack(outs_s, 1), stack(outs_s, 2), stack(outs_s, 3), stack(outs_s, 4))
```

```python
import functools
import math

import jax
import jax.numpy as jnp
from jax import lax
from jax.experimental import pallas as pl
from jax.experimental.pallas import tpu as pltpu

D_MODEL = 2048
D_SSM = 1024
SSM_GROUP = 16
N_SSM_GROUPS = 64
SSM_STATE = 64
D_ATTN = 1024
HEAD_DIM = 128
N_HEADS = 8
D_FF = 8192
EPS = 1e-6
NEG_INF = -1e30
LOG2E = math.log2(math.e)
MAX_FIXED_SHIFT = 45.0
EXP2_ZERO = -150.0

LANES = 128
SLAB = 256
N_SLABS = D_SSM // SLAB
SLAB_GROUPS = SLAB // SSM_GROUP
SLAB_STATE = SLAB_GROUPS * SSM_STATE
V7X_VMEM_BYTES = 64 * 1024 * 1024
VMEM_LIMIT = V7X_VMEM_BYTES - 2 * 1024 * 1024

F32 = jnp.float32
BF16 = jnp.bfloat16


def _params(sem, vmem=VMEM_LIMIT):
    return pltpu.CompilerParams(dimension_semantics=sem, vmem_limit_bytes=vmem)


def _rms(x, g):
    return x * lax.rsqrt(jnp.mean(x * x, axis=-1, keepdims=True) + EPS) * g


def _in_proj_kernel(x_ref, g_ref, w_ref, wf_ref, bf_ref, gq_ref, gk_ref,
                    u_ref, qb_ref, k_ref, kb_ref, v_ref, vb_ref, lft_ref):
    hb = _rms(x_ref[...], g_ref[...]).astype(BF16)

    def project(w_rows):
        return lax.dot_general(hb, w_rows, (((1,), (1,)), ((), ())), preferred_element_type=F32)

    u_ref[...] = project(w_ref[0:D_SSM, :])

    q = project(w_ref[D_SSM:D_SSM + D_ATTN, :])
    gq = gq_ref[...]
    for h in range(N_HEADS):
        sl = slice(h * HEAD_DIM, (h + 1) * HEAD_DIM)
        qb_ref[:, sl] = (_rms(q[:, sl], gq) * (HEAD_DIM ** -0.5 * LOG2E)).astype(BF16)

    k = project(w_ref[D_SSM + D_ATTN:D_SSM + 2 * D_ATTN, :])
    gk = gk_ref[...]
    for h in range(N_HEADS):
        sl = slice(h * HEAD_DIM, (h + 1) * HEAD_DIM)
        kn = _rms(k[:, sl], gk)
        k_ref[:, sl] = kn
        kb_ref[:, sl] = kn.astype(BF16)

    v = project(w_ref[D_SSM + 2 * D_ATTN:D_SSM + 3 * D_ATTN, :])
    v_ref[...] = v
    vb_ref[...] = v.astype(BF16)

    zf = project(wf_ref[...]) + bf_ref[...]
    lf = jnp.minimum(zf, 0.0) - jnp.log1p(jnp.exp(-jnp.abs(zf)))
    lft_ref[...] = jnp.transpose(lf)[:N_HEADS, :]


def _in_proj(x, g, w_main, w_f, b_f, g_q, g_k, *, tt):
    t = x.shape[0]
    assert t % tt == 0
    row = lambda i: (i, 0)
    const = lambda i: (0, 0)
    once = pl.Buffered(1)
    wide = lambda dt: jax.ShapeDtypeStruct((t, D_ATTN), dt)
    return pl.pallas_call(
        _in_proj_kernel,
        grid=(t // tt,),
        in_specs=[
            pl.BlockSpec((tt, D_MODEL), row),
            pl.BlockSpec((1, D_MODEL), const),
            pl.BlockSpec((D_SSM + 3 * D_ATTN, D_MODEL), const, pipeline_mode=once),
            pl.BlockSpec((LANES, D_MODEL), const, pipeline_mode=once),
            pl.BlockSpec((1, LANES), const),
            pl.BlockSpec((1, HEAD_DIM), const),
            pl.BlockSpec((1, HEAD_DIM), const),
        ],
        out_specs=[
            pl.BlockSpec((tt, D_SSM), row),
            pl.BlockSpec((tt, D_ATTN), row),
            pl.BlockSpec((tt, D_ATTN), row),
            pl.BlockSpec((tt, D_ATTN), row),
            pl.BlockSpec((tt, D_ATTN), row),
            pl.BlockSpec((tt, D_ATTN), row),
            pl.BlockSpec((N_HEADS, tt), lambda i: (0, i)),
        ],
        out_shape=[wide(F32), wide(BF16), wide(F32), wide(BF16), wide(F32), wide(BF16),
                   jax.ShapeDtypeStruct((N_HEADS, t), F32)],
        compiler_params=_params(("arbitrary",)),
        name="in_proj",
    )(x, g, w_main, w_f, b_f, g_q, g_k)


def _cumsum_kernel(x_ref, o_ref):
    rows, length = x_ref.shape
    lane = lax.broadcasted_iota(jnp.int32, (rows, LANES), 1)
    carry = jnp.zeros((rows, 1), F32)
    for b in range(length // LANES):
        sl = slice(b * LANES, (b + 1) * LANES)
        x = x_ref[:, sl]
        d = 1
        while d < LANES:
            x = x + jnp.where(lane >= d, pltpu.roll(x, d, axis=1), 0.0)
            d *= 2
        x = x + carry
        o_ref[:, sl] = x * LOG2E
        carry = x[:, LANES - 1:LANES]


def _cumsum_lanes(x):
    rows, length = x.shape
    assert length % LANES == 0 and rows % 8 == 0
    return pl.pallas_call(
        _cumsum_kernel,
        out_shape=jax.ShapeDtypeStruct((rows, length), F32),
        name="cumsum",
    )(x)


def _s5_tables(a_re, a_im, log_step, b_re, b_im, c_re, c_im, d, tc):
    step = jnp.exp(log_step)[:, None]
    mag = jnp.exp(a_re * step)
    abar_re = mag * jnp.cos(a_im * step)
    abar_im = mag * jnp.sin(a_im * step)
    den = a_re * a_re + a_im * a_im
    nr = abar_re - 1.0
    ni = abar_im
    fr = (nr * a_re + ni * a_im) / den
    fi = (ni * a_re - nr * a_im) / den
    bbar_re = fr[..., None] * b_re - fi[..., None] * b_im
    bbar_im = fr[..., None] * b_im + fi[..., None] * b_re
    row_group = lambda n, per: (lax.broadcasted_iota(jnp.int32, (n, 1), 0) // per)
    col_group = lambda n, per: (lax.broadcasted_iota(jnp.int32, (1, n), 1) // per)

    def in_blockdiag(b):
        b = b.reshape(N_SLABS, SLAB_GROUPS, SSM_STATE, SSM_GROUP).transpose(0, 1, 3, 2)
        b = jnp.tile(b.reshape(N_SLABS, SLAB, SSM_STATE), (1, 1, SLAB_GROUPS))
        keep = row_group(SLAB, SSM_GROUP) == col_group(SLAB_STATE, SSM_STATE)
        return jnp.where(keep[None], b, 0.0)

    def out_blockdiag(c):
        c = c.reshape(N_SLABS, SLAB_GROUPS, SSM_GROUP, SSM_STATE).transpose(0, 1, 3, 2)
        c = jnp.tile(c.reshape(N_SLABS, SLAB_STATE, SSM_GROUP), (1, 1, SLAB_GROUPS))
        keep = row_group(SLAB_STATE, SSM_STATE) == col_group(SLAB, SSM_GROUP)
        return jnp.where(keep[None], c, 0.0)

    bb = jnp.concatenate([in_blockdiag(bbar_re), in_blockdiag(bbar_im)], axis=-1).astype(BF16)
    cc = jnp.concatenate([out_blockdiag(c_re), out_blockdiag(-c_im)], axis=1).astype(BF16)
    n = jnp.arange(1, tc + 1, dtype=F32)[:, None, None]
    pmag = jnp.exp(a_re[None] * step[None] * n)
    ang = a_im[None] * step[None] * n
    p_re = (pmag * jnp.cos(ang)).reshape(tc, N_SLABS, SLAB_STATE).transpose(1, 0, 2)
    p_im = (pmag * jnp.sin(ang)).reshape(tc, N_SLABS, SLAB_STATE).transpose(1, 0, 2)
    pw = jnp.concatenate([p_re, p_im], axis=-1)
    dd = d.reshape(N_SLABS, 1, SLAB)
    return bb, cc, pw, dd


def _s5_kernel(u_ref, perm_ref, bb_ref, cc_ref, pw_ref, d_ref, h0_ref, y_ref, hfin_ref,
               st_s, hb_s, g_s, carry_s, *, tc, nc, tiles_per_seq, carry_mode):
    i = pl.program_id(1)
    u = u_ref[...]
    u_hi = u.astype(BF16)
    u_lo = (u - u_hi.astype(F32)).astype(BF16)
    up = jnp.dot(perm_ref[...], jnp.concatenate([u_hi, u_lo], axis=1),
                 preferred_element_type=F32)
    u_hi = up[:, :SLAB]
    u = u_hi + up[:, SLAB:]
    uh = u_hi.astype(BF16)
    y_ref[...] = d_ref[...] * u
    blk = 2 * LANES
    nblk = SLAB_STATE // blk
    grp = max(1, 16 // nc)
    lanes = lambda b: slice(b * blk, (b + 1) * blk)
    step_rows = lambda t: slice(t * nc, (t + 1) * nc)

    if carry_mode:
        @pl.when(i % tiles_per_seq == 0)
        def _():
            carry_s[...] = jnp.zeros_like(carry_s)
    else:
        g_s[...] = h0_ref[...]

    for b in range(nblk):
        re, im = lanes(b), lanes(nblk + b)
        st_s[:, re] = jnp.dot(uh, bb_ref[:, re], preferred_element_type=F32)
        st_s[:, im] = jnp.dot(uh, bb_ref[:, im], preferred_element_type=F32)
        ar = pw_ref[0:1, re]
        ai = pw_ref[0:1, im]
        hr = jnp.zeros((nc, blk), F32)
        hi = jnp.zeros((nc, blk), F32)
        for t in range(tc):
            rows = step_rows(t)
            hr, hi = (ar * hr - ai * hi + st_s[rows, re],
                      ar * hi + ai * hr + st_s[rows, im])
            st_s[rows, re] = hr
            st_s[rows, im] = hi

        ar = pw_ref[tc - 1:tc, re]
        ai = pw_ref[tc - 1:tc, im]
        if carry_mode:
            gr = carry_s[0:1, re]
            gi = carry_s[0:1, im]
            for c in range(nc):
                g_s[c:c + 1, re] = gr
                g_s[c:c + 1, im] = gi
                gr, gi = (ar * gr - ai * gi + hr[c:c + 1, :],
                          ar * gi + ai * gr + hi[c:c + 1, :])
            carry_s[0:1, re] = gr
            carry_s[0:1, im] = gi
            hfin_ref[0:1, re] = gr
            hfin_ref[0:1, im] = gi
            gr = g_s[:, re]
            gi = g_s[:, im]
        else:
            gr = g_s[:, re]
            gi = g_s[:, im]
            hfin_ref[:, re] = ar * gr - ai * gi + hr
            hfin_ref[:, im] = ar * gi + ai * gr + hi

        for t0 in range(0, tc, grp):
            full_r, full_i = [], []
            for t in range(t0, t0 + grp):
                rows = step_rows(t)
                pr = pw_ref[t:t + 1, re]
                pi = pw_ref[t:t + 1, im]
                full_r.append(st_s[rows, re] + (pr * gr - pi * gi))
                full_i.append(st_s[rows, im] + (pr * gi + pi * gr))
            rows = slice(t0 * nc, (t0 + grp) * nc)
            hb_s[rows, re] = jnp.concatenate(full_r, axis=0).astype(BF16)
            hb_s[rows, im] = jnp.concatenate(full_i, axis=0).astype(BF16)

        y_ref[...] += (jnp.dot(hb_s[:, re], cc_ref[re, :], preferred_element_type=F32)
                       + jnp.dot(hb_s[:, im], cc_ref[im, :], preferred_element_type=F32))


def _step_major_perm(tt, tc):
    nc = tt // tc
    r_out = jnp.arange(tt)
    r_in = (r_out % nc) * tc + r_out // nc
    return (r_in[:, None] == jnp.arange(tt)[None, :]).astype(BF16)


def _s5(u, tables, perm, h0, *, tt, tc, seq_len, carry_mode):
    bb, cc, pw, dd = tables
    t = u.shape[0]
    nc = tt // tc
    n_tiles = t // tt
    assert t % tt == 0 and tt % tc == 0
    if carry_mode:
        assert seq_len % tt == 0
        tiles_per_seq = seq_len // tt
        n_seq = t // seq_len
        hfin_shape = (N_SLABS, n_seq, 1, 2 * SLAB_STATE)
        hfin_spec = pl.BlockSpec((None, None, 1, 2 * SLAB_STATE),
                                 lambda s, i: (s, i // tiles_per_seq, 0, 0))
        h0 = jnp.zeros((N_SLABS, 8, 2 * SLAB_STATE), F32)
        h0_spec = pl.BlockSpec((None, 8, 2 * SLAB_STATE), lambda s, i: (s, 0, 0))
    else:
        assert seq_len == tc
        tiles_per_seq = 1
        hfin_shape = (N_SLABS, n_tiles, nc, 2 * SLAB_STATE)
        hfin_spec = pl.BlockSpec((None, None, nc, 2 * SLAB_STATE), lambda s, i: (s, i, 0, 0))
        h0_spec = pl.BlockSpec((None, nc, 2 * SLAB_STATE), lambda s, i: (s, i, 0))
    kern = functools.partial(_s5_kernel, tc=tc, nc=nc, tiles_per_seq=tiles_per_seq,
                             carry_mode=carry_mode)
    y, hfin = pl.pallas_call(
        kern,
        grid=(N_SLABS, n_tiles),
        in_specs=[
            pl.BlockSpec((tt, SLAB), lambda s, i: (i, s)),
            pl.BlockSpec((tt, tt), lambda s, i: (0, 0), pipeline_mode=pl.Buffered(1)),
            pl.BlockSpec((None, SLAB, 2 * SLAB_STATE), lambda s, i: (s, 0, 0)),
            pl.BlockSpec((None, 2 * SLAB_STATE, SLAB), lambda s, i: (s, 0, 0)),
            pl.BlockSpec((None, tc, 2 * SLAB_STATE), lambda s, i: (s, 0, 0)),
            pl.BlockSpec((None, 1, SLAB), lambda s, i: (s, 0, 0)),
            h0_spec,
        ],
        out_specs=[pl.BlockSpec((tt, SLAB), lambda s, i: (i, s)), hfin_spec],
        out_shape=[jax.ShapeDtypeStruct((t, D_SSM), F32),
                   jax.ShapeDtypeStruct(hfin_shape, F32)],
        scratch_shapes=[pltpu.VMEM((tt, 2 * SLAB_STATE), F32),
                        pltpu.VMEM((tt, 2 * SLAB_STATE), BF16),
                        pltpu.VMEM((nc, 2 * SLAB_STATE), F32),
                        pltpu.VMEM((8, 2 * SLAB_STATE), F32)],
        compiler_params=_params(("arbitrary", "arbitrary")),
        name="s5",
    )(u, perm, bb, cc, pw, dd, h0)
    return y, hfin.reshape(N_SLABS, -1, 2 * SLAB_STATE)


def _state_to_gp(hfin):
    n = hfin.shape[1]
    h = hfin.transpose(1, 0, 2)
    re = h[..., :SLAB_STATE].reshape(n, N_SSM_GROUPS, SSM_STATE)
    im = h[..., SLAB_STATE:].reshape(n, N_SSM_GROUPS, SSM_STATE)
    return re, im


def _state_from_gp(re, im):
    n = re.shape[0]
    h = jnp.concatenate([re.reshape(n, N_SLABS, SLAB_STATE), im.reshape(n, N_SLABS, SLAB_STATE)],
                        axis=-1)
    return h.transpose(1, 0, 2)


def _col_from_row(row):
    n = row.shape[1]
    r = lax.broadcasted_iota(jnp.int32, (n, n), 0)
    c = lax.broadcasted_iota(jnp.int32, (n, n), 1)
    return jnp.sum(jnp.where(r == c, jnp.broadcast_to(row, (n, n)), 0.0), axis=1, keepdims=True)


def _attn_prompt_kernel(first_ref, q_ref, k_ref, v_ref, cq_ref, ck_ref, o_ref,
                        m_s, l_s, acc_s, cq_s, s_s, *, bq, online):
    i = pl.program_id(1)
    bk = bq // 2
    cq_s[...] = jnp.transpose(jnp.broadcast_to(cq_ref[...], (LANES, bq)))
    if online:
        m_s[...] = jnp.full_like(m_s, NEG_INF)
    l_s[...] = jnp.zeros_like(l_s)
    acc_s[...] = jnp.zeros_like(acc_s)
    nchunk = bk // LANES
    all_rows = slice(0, bq)

    def scores(j, slot, rows=all_rows):
        start = pl.multiple_of(j * bk, bk)
        s_s[slot, rows, :] = lax.dot_general(q_ref[rows, :], k_ref[pl.ds(start, bk), :],
                                             (((1,), (1,)), ((), ())), preferred_element_type=F32)

    def reduce_block(j, slot, rows=all_rows, causal=False):
        start = pl.multiple_of(j * bk, bk)
        v = v_ref[pl.ds(start, bk), :]
        nrows = rows.stop - rows.start

        def logits(c):
            col0 = pl.multiple_of(start + c * LANES, LANES)
            t = s_s[slot, rows, c * LANES:(c + 1) * LANES] - ck_ref[:, pl.ds(col0, LANES)]
            if causal:
                r = lax.broadcasted_iota(jnp.int32, (nrows, LANES), 0)
                col = lax.broadcasted_iota(jnp.int32, (nrows, LANES), 1) + c * LANES
                t = jnp.where(col <= r, t, NEG_INF)
            return t

        if online:
            m_loc = functools.reduce(jnp.maximum, [logits(c) for c in range(nchunk)])
            m_old = m_s[rows, :]
            m_new = jnp.maximum(m_old, jnp.max(m_loc, axis=1, keepdims=True) + cq_s[rows, :])
            m_s[rows, :] = m_new
            alpha = jnp.exp2(m_old - m_new)
            shift = cq_s[rows, :] - m_new
        else:
            shift = cq_s[rows, :]
        ps = [jnp.exp2(logits(c) + shift) for c in range(nchunk)]
        p = jnp.concatenate([pc.astype(BF16) for pc in ps], axis=1)
        l_new = functools.reduce(jnp.add, ps)
        acc_new = jnp.dot(p, v, preferred_element_type=F32)
        if online:
            l_s[rows, :] = alpha * l_s[rows, :] + l_new
            acc_s[rows, :] = alpha * acc_s[rows, :] + acc_new
        else:
            l_s[rows, :] += l_new
            acc_s[rows, :] += acc_new

    first = first_ref[pl.program_id(0) * pl.num_programs(1) + i]
    scores(2 * first, 0)

    def body(p, carry):
        j = 2 * p
        scores(j + 1, 1)
        reduce_block(j, 0)
        scores(j + 2, 0)
        reduce_block(j + 1, 1)
        return carry

    lax.fori_loop(first, i, body, 0)
    late_rows = slice(bk, bq)
    scores(2 * i + 1, 1, late_rows)
    reduce_block(2 * i, 0, causal=True)
    reduce_block(2 * i + 1, 1, late_rows, causal=True)

    o_ref[...] = acc_s[...] / jnp.sum(l_s[...], axis=1, keepdims=True)


def _attn_prompt(qb, kb, vb, first, cq_rows, ck_rows, *, n_seq, seq_len, bq, online):
    nq = seq_len // bq
    assert seq_len % bq == 0
    kern = functools.partial(_attn_prompt_kernel, bq=bq, online=online)
    q_map = lambda g, i, first: ((g // N_HEADS) * nq + i, g % N_HEADS)
    kv_map = lambda g, i, first: (g // N_HEADS, g % N_HEADS)
    grid_spec = pltpu.PrefetchScalarGridSpec(
        num_scalar_prefetch=1,
        grid=(n_seq * N_HEADS, nq),
        in_specs=[
            pl.BlockSpec((bq, HEAD_DIM), q_map),
            pl.BlockSpec((seq_len, HEAD_DIM), kv_map),
            pl.BlockSpec((seq_len, HEAD_DIM), kv_map),
            pl.BlockSpec((None, 1, bq), lambda g, i, first: (g, 0, i)),
            pl.BlockSpec((None, 1, seq_len), lambda g, i, first: (g, 0, 0)),
        ],
        out_specs=pl.BlockSpec((bq, HEAD_DIM), q_map),
        scratch_shapes=[pltpu.VMEM((bq, LANES), F32), pltpu.VMEM((bq, LANES), F32),
                        pltpu.VMEM((bq, HEAD_DIM), F32), pltpu.VMEM((bq, LANES), F32),
                        pltpu.VMEM((2, bq, bq // 2), F32)],
    )
    return pl.pallas_call(
        kern,
        grid_spec=grid_spec,
        out_shape=jax.ShapeDtypeStruct((n_seq * seq_len, D_ATTN), F32),
        compiler_params=_params(("arbitrary", "arbitrary")),
        name="attn_prompt",
    )(first, qb, kb, vb, cq_rows, ck_rows)


def _first_live_pair(c_rows, bq):
    g, _, length = c_rows.shape
    nq = length // bq
    blocks = c_rows.reshape(g, nq, bq)
    hi = jnp.max(blocks, axis=-1)
    lo = jnp.min(blocks, axis=-1)
    dead = (hi[:, :, None] - lo[:, None, :]) < EXP2_ZERO - 1.0
    pair = jnp.arange(nq, dtype=jnp.int32)
    dead = dead & (pair[None, None, :] < pair[None, :, None])
    first = jnp.min(jnp.where(dead, nq, pair[None, None, :]), axis=-1)
    return first.reshape(g * nq).astype(jnp.int32)


def _attn_sample_kernel(q_ref, kn_ref, vn_ref, ck_ref, cv_ref, c_ref, o_ref, *, past, s_new):
    r = lax.broadcasted_iota(jnp.int32, (s_new, s_new), 0)
    col = lax.broadcasted_iota(jnp.int32, (s_new, s_new), 1)
    causal = col <= r
    nt = (((1,), (1,)), ((), ()))
    for h in range(N_HEADS):
        sl = slice(h * HEAD_DIM, (h + 1) * HEAD_DIM)
        q = q_ref[:, sl]
        c_past = c_ref[h:h + 1, 0:past]
        c_new = c_ref[h:h + 1, past:past + s_new]
        cq = _col_from_row(c_new)
        head_rows = pl.ds(h, past, stride=N_HEADS)
        s_p = lax.dot_general(q, ck_ref[head_rows, :].astype(BF16), nt, preferred_element_type=F32)
        s_p = s_p + cq - c_past
        s_n = lax.dot_general(q, kn_ref[:, sl], nt, preferred_element_type=F32)
        s_n = jnp.where(causal, s_n + cq - c_new, NEG_INF)
        m = jnp.maximum(jnp.max(s_p, axis=1, keepdims=True), jnp.max(s_n, axis=1, keepdims=True))
        p_p = jnp.exp2(s_p - m)
        p_n = jnp.exp2(s_n - m)
        l = jnp.sum(p_p, axis=1, keepdims=True) + jnp.sum(p_n, axis=1, keepdims=True)
        acc = (jnp.dot(p_p.astype(BF16), cv_ref[head_rows, :].astype(BF16), preferred_element_type=F32)
               + jnp.dot(p_n.astype(BF16), vn_ref[:, sl], preferred_element_type=F32))
        o_ref[:, sl] = acc / l


def _attn_sample(qb, kb, vb, cache_k, cache_v, c_all, *, n_seq, s_new, past):
    kern = functools.partial(_attn_sample_kernel, past=past, s_new=s_new)
    new = pl.BlockSpec((s_new, D_ATTN), lambda b: (b, 0))
    cache_k = cache_k.reshape(n_seq, past * N_HEADS, HEAD_DIM)
    cache_v = cache_v.reshape(n_seq, past * N_HEADS, HEAD_DIM)
    cache = pl.BlockSpec((None, past * N_HEADS, HEAD_DIM), lambda b: (b, 0, 0))
    return pl.pallas_call(
        kern,
        grid=(n_seq,),
        in_specs=[new, new, new, cache, cache,
                  pl.BlockSpec((N_HEADS, c_all.shape[1]), lambda b: (b, 0))],
        out_specs=new,
        out_shape=jax.ShapeDtypeStruct((n_seq * s_new, D_ATTN), F32),
        compiler_params=_params(("arbitrary",)),
        name="attn_sample",
    )(qb, kb, vb, cache_k, cache_v, c_all)


def _gelu_tanh(x):
    return 0.5 * x * (1.0 + jnp.tanh(math.sqrt(2.0 / math.pi) * (x + 0.044715 * (x * x * x))))


def _mix_kernel(x_ref, y_ref, a_ref, unperm_ref, wglu_ref, gs_ref, ga_ref, wout_ref, gm_ref,
                x1_ref, hm_ref):
    gy = _gelu_tanh(y_ref[...])
    gate = jax.nn.sigmoid(jnp.dot(gy.astype(BF16), wglu_ref[...], preferred_element_type=F32))
    ssm = _rms(gy * gate, gs_ref[...]).astype(BF16)
    ssm = jnp.dot(unperm_ref[...], ssm, preferred_element_type=F32).astype(BF16)
    att = _rms(a_ref[...], ga_ref[...]).astype(BF16)
    x1 = (x_ref[...]
          + jnp.dot(ssm, wout_ref[0:D_SSM, :], preferred_element_type=F32)
          + jnp.dot(att, wout_ref[D_SSM:D_SSM + D_ATTN, :], preferred_element_type=F32))
    x1_ref[...] = x1
    hm_ref[...] = _rms(x1, gm_ref[...]).astype(BF16)


def _mix(x, y, attn, unperm, w_glu, g_ssm, g_attn, w_out, g_mlp, *, tt):
    t = x.shape[0]
    assert t % tt == 0 and unperm.shape == (tt, tt)
    row = lambda i: (i, 0)
    const = lambda i: (0, 0)
    once = pl.Buffered(1)
    return pl.pallas_call(
        _mix_kernel,
        grid=(t // tt,),
        in_specs=[
            pl.BlockSpec((tt, D_MODEL), row),
            pl.BlockSpec((tt, D_SSM), row),
            pl.BlockSpec((tt, D_ATTN), row),
            pl.BlockSpec((tt, tt), const, pipeline_mode=once),
            pl.BlockSpec((D_SSM, D_SSM), const, pipeline_mode=once),
            pl.BlockSpec((1, D_SSM), const),
            pl.BlockSpec((1, D_ATTN), const),
            pl.BlockSpec((D_SSM + D_ATTN, D_MODEL), const, pipeline_mode=once),
            pl.BlockSpec((1, D_MODEL), const),
        ],
        out_specs=[pl.BlockSpec((tt, D_MODEL), row), pl.BlockSpec((tt, D_MODEL), row)],
        out_shape=[jax.ShapeDtypeStruct((t, D_MODEL), F32), jax.ShapeDtypeStruct((t, D_MODEL), BF16)],
        compiler_params=_params(("arbitrary",)),
        name="mix",
    )(x, y, attn, unperm, w_glu, g_ssm, g_attn, w_out, g_mlp)


def _mlp_kernel(x1_ref, hm_ref, wup_ref, wdn_ref, o_ref):
    j = pl.program_id(1)

    @pl.when(j == 0)
    def _():
        o_ref[...] = x1_ref[...]

    a = jnp.maximum(jnp.dot(hm_ref[...], wup_ref[...], preferred_element_type=F32), 0.0)
    o_ref[...] += jnp.dot((a * a).astype(BF16), wdn_ref[...], preferred_element_type=F32)


def _mlp(x1, hm, w_up, w_down, *, tt, tf):
    t = x1.shape[0]
    assert t % tt == 0 and D_FF % tf == 0
    return pl.pallas_call(
        _mlp_kernel,
        grid=(t // tt, D_FF // tf),
        in_specs=[
            pl.BlockSpec((tt, D_MODEL), lambda i, j: (i, 0)),
            pl.BlockSpec((tt, D_MODEL), lambda i, j: (i, 0)),
            pl.BlockSpec((D_MODEL, tf), lambda i, j: (0, j)),
            pl.BlockSpec((tf, D_MODEL), lambda i, j: (j, 0)),
        ],
        out_specs=pl.BlockSpec((tt, D_MODEL), lambda i, j: (i, 0)),
        out_shape=jax.ShapeDtypeStruct((t, D_MODEL), F32),
        compiler_params=_params(("arbitrary", "arbitrary")),
        name="mlp",
    )(x1, hm, w_up, w_down)


def _layer(x, weights, *, n_seq, seq_len, cache=None, h0=None, tiles):
    (g_mix, w_main, w_f, b_f, g_q, g_k, s5_raw, w_glu, g_ssm, g_attn, w_out, g_mlp,
     w_up, w_down) = weights
    t = n_seq * seq_len
    x2 = x.reshape(t, D_MODEL)
    u, qb, k, kb, v, vb, lf_t = _in_proj(x2, g_mix, w_main, w_f, b_f, g_q, g_k, tt=tiles["proj"])
    lf_t = lf_t.reshape(N_HEADS, n_seq, seq_len)
    logf = lf_t.transpose(1, 2, 0)
    lf_rows = lf_t.transpose(1, 0, 2).reshape(n_seq * N_HEADS, seq_len)

    tc = tiles["chunk"] if cache is None else seq_len
    perm = _step_major_perm(tiles["s5"], tc)
    if cache is None:
        tables = _s5_tables(*s5_raw, tc=tc)
        y, hfin = _s5(u, tables, perm, None, tt=tiles["s5"], tc=tc, seq_len=seq_len,
                      carry_mode=True)
        c_rows = _cumsum_lanes(lf_rows).reshape(n_seq * N_HEADS, 1, seq_len)
        attend = functools.partial(_attn_prompt, qb, kb, vb, n_seq=n_seq, seq_len=seq_len,
                                   bq=tiles["bq"])
        bound = (HEAD_DIM ** -0.5 * LOG2E) * HEAD_DIM * jnp.max(jnp.abs(g_q)) * jnp.max(jnp.abs(g_k))
        attn = lax.cond(
            bound <= MAX_FIXED_SHIFT,
            lambda: attend(_first_live_pair(c_rows, tiles["bq"]), c_rows - bound, c_rows,
                           online=False),
            lambda: attend(jnp.zeros((c_rows.shape[0] * (seq_len // tiles["bq"]),), jnp.int32),
                           c_rows, c_rows, online=True))
    else:
        cache_k, cache_v, cache_logf = cache
        past = cache_k.shape[1]
        tables = _s5_tables(*s5_raw, tc=tc)
        y, hfin = _s5(u, tables, perm, h0, tt=tiles["s5"], tc=tc, seq_len=seq_len,
                      carry_mode=False)
        past_rows = cache_logf.transpose(0, 2, 1).reshape(n_seq * N_HEADS, past)
        total = past + seq_len
        padded = -(-total // LANES) * LANES
        lf_all = jnp.concatenate(
            [past_rows, lf_rows, jnp.zeros((n_seq * N_HEADS, padded - total), F32)], axis=1)
        c_all = _cumsum_lanes(lf_all)
        attn = _attn_sample(qb, kb, vb, cache_k, cache_v, c_all,
                            n_seq=n_seq, s_new=seq_len, past=past)

    x1, hm = _mix(x2, y, attn, perm.T, w_glu, g_ssm, g_attn, w_out, g_mlp, tt=tiles["s5"])
    out = _mlp(x1, hm, w_up, w_down, tt=tiles["mlp"], tf=tiles["tf"])
    h_re, h_im = _state_to_gp(hfin)
    return (out.reshape(n_seq, seq_len, D_MODEL),
            k.reshape(n_seq, seq_len, N_HEADS, HEAD_DIM),
            v.reshape(n_seq, seq_len, N_HEADS, HEAD_DIM),
            logf, h_re, h_im)


def _cast_kernel(w_ref, wf_ref, o_ref, of_ref):
    o_ref[...] = w_ref[...].astype(o_ref.dtype)
    of_ref[...] = jnp.zeros_like(of_ref)
    of_ref[0:N_HEADS, :] = wf_ref[...].astype(of_ref.dtype)


def _cast_w_in(w_t, n_main, rows_per_step=512):
    n_rows, d = w_t.shape
    assert n_main % rows_per_step == 0 and n_rows - n_main == N_HEADS and n_main % N_HEADS == 0
    return pl.pallas_call(
        _cast_kernel,
        grid=(n_main // rows_per_step,),
        in_specs=[pl.BlockSpec((rows_per_step, d), lambda i: (i, 0)),
                  pl.BlockSpec((N_HEADS, d), lambda i: (n_main // N_HEADS, 0))],
        out_specs=[pl.BlockSpec((rows_per_step, d), lambda i: (i, 0)),
                   pl.BlockSpec((LANES, d), lambda i: (0, 0))],
        out_shape=[jax.ShapeDtypeStruct((n_main, d), BF16),
                   jax.ShapeDtypeStruct((LANES, d), BF16)],
        compiler_params=_params(("arbitrary",)),
        name="cast_w_in",
    )(w_t, w_t)


def _prep_weights(l, g_norm_mix, w_in, b_f, ssm_a_re, ssm_a_im, ssm_log_step, ssm_b_re, ssm_b_im,
                  ssm_c_re, ssm_c_im, ssm_d, w_glu, g_q, g_k, g_out_ssm, g_out_attn, w_out,
                  g_norm_mlp, w_up, w_down):
    n_main = D_SSM + 3 * D_ATTN
    w_main, w_f = _cast_w_in(jnp.swapaxes(w_in[l], 0, 1), n_main)
    b = jnp.pad(b_f[l], (0, LANES - N_HEADS)).reshape(1, LANES)
    s5_raw = (ssm_a_re[l], ssm_a_im[l], ssm_log_step[l], ssm_b_re[l], ssm_b_im[l],
              ssm_c_re[l], ssm_c_im[l], ssm_d[l])
    return (g_norm_mix[l].reshape(1, D_MODEL), w_main, w_f, b,
            g_q[l].reshape(1, HEAD_DIM), g_k[l].reshape(1, HEAD_DIM), s5_raw,
            w_glu[l].astype(BF16), g_out_ssm[l].reshape(1, D_SSM), g_out_attn[l].reshape(1, D_ATTN),
            w_out[l].astype(BF16), g_norm_mlp[l].reshape(1, D_MODEL),
            w_up[l].astype(BF16), w_down[l].astype(BF16))


PROMPT_TILES = dict(proj=512, s5=512, chunk=32, bq=1024, mlp=512, tf=2048)
SAMPLE_TILES = dict(proj=512, s5=512, mlp=512, tf=2048)


def kernel(x_prompt, x_sample, cache_k, cache_v, cache_logf, state_ssm_re, state_ssm_im,
           g_norm_mix, w_in, b_f, ssm_a_re, ssm_a_im, ssm_log_step, ssm_b_re, ssm_b_im,
           ssm_c_re, ssm_c_im, ssm_d, w_glu, g_q, g_k, g_out_ssm, g_out_attn, w_out,
           g_norm_mlp, w_up, w_down):
    depth = w_in.shape[0]
    y_p, y_s = x_prompt, x_sample
    outs_p, outs_s = [], []
    for l in range(depth):
        weights = _prep_weights(l, g_norm_mix, w_in, b_f, ssm_a_re, ssm_a_im, ssm_log_step,
                                ssm_b_re, ssm_b_im, ssm_c_re, ssm_c_im, ssm_d, w_glu, g_q, g_k,
                                g_out_ssm, g_out_attn, w_out, g_norm_mlp, w_up, w_down)
        n_p, l_p = y_p.shape[0], y_p.shape[1]
        y_p, *rest_p = _layer(y_p, weights, n_seq=n_p, seq_len=l_p, tiles=PROMPT_TILES)
        n_s, l_s = y_s.shape[0], y_s.shape[1]
        h0 = _state_from_gp(state_ssm_re[l], state_ssm_im[l])
        y_s, *rest_s = _layer(y_s, weights, n_seq=n_s, seq_len=l_s,
                              cache=(cache_k[l], cache_v[l], cache_logf[l]), h0=h0,
                              tiles=SAMPLE_TILES)
        outs_p.append(rest_p)
        outs_s.append(rest_s)
    stack = lambda outs, idx: jnp.stack([o[idx] for o in outs])
    return (y_p, y_s,
            stack(outs_p, 0), stack(outs_p, 1), stack(outs_p, 2), stack(outs_p, 3), stack(outs_p, 4),
            stack(outs_s, 0), stack(outs_s, 1), stack(outs_s, 2), stack(outs_s, 3), stack(outs_s, 4))
```

```python
import functools
import math

import jax
import jax.numpy as jnp
from jax import lax
from jax.experimental import pallas as pl
from jax.experimental.pallas import tpu as pltpu

D_MODEL = 2048
D_SSM = 1024
SSM_GROUP = 16
N_SSM_GROUPS = 64
SSM_STATE = 64
D_ATTN = 1024
HEAD_DIM = 128
N_HEADS = 8
D_FF = 8192
EPS = 1e-6
NEG_INF = -1e30
LOG2E = math.log2(math.e)
MAX_FIXED_SHIFT = 45.0
EXP2_ZERO = -150.0

LANES = 128
SLAB = 256
N_SLABS = D_SSM // SLAB
SLAB_GROUPS = SLAB // SSM_GROUP
SLAB_STATE = SLAB_GROUPS * SSM_STATE
V7X_VMEM_BYTES = 64 * 1024 * 1024
VMEM_LIMIT = V7X_VMEM_BYTES - 2 * 1024 * 1024

F32 = jnp.float32
BF16 = jnp.bfloat16


def _params(sem, vmem=VMEM_LIMIT):
    return pltpu.CompilerParams(dimension_semantics=sem, vmem_limit_bytes=vmem)


def _rms(x, g):
    return x * lax.rsqrt(jnp.mean(x * x, axis=-1, keepdims=True) + EPS) * g


def _in_proj_kernel(x_ref, g_ref, w_ref, wf_ref, bf_ref, gq_ref, gk_ref,
                    u_ref, qb_ref, k_ref, kb_ref, v_ref, vb_ref, lft_ref):
    hb = _rms(x_ref[...], g_ref[...]).astype(BF16)

    def project(w_rows):
        return lax.dot_general(hb, w_rows, (((1,), (1,)), ((), ())), preferred_element_type=F32)

    u_ref[...] = project(w_ref[0:D_SSM, :])

    q = project(w_ref[D_SSM:D_SSM + D_ATTN, :])
    gq = gq_ref[...]
    for h in range(N_HEADS):
        sl = slice(h * HEAD_DIM, (h + 1) * HEAD_DIM)
        qb_ref[:, sl] = (_rms(q[:, sl], gq) * (HEAD_DIM ** -0.5 * LOG2E)).astype(BF16)

    k = project(w_ref[D_SSM + D_ATTN:D_SSM + 2 * D_ATTN, :])
    gk = gk_ref[...]
    for h in range(N_HEADS):
        sl = slice(h * HEAD_DIM, (h + 1) * HEAD_DIM)
        kn = _rms(k[:, sl], gk)
        k_ref[:, sl] = kn
        kb_ref[:, sl] = kn.astype(BF16)

    v = project(w_ref[D_SSM + 2 * D_ATTN:D_SSM + 3 * D_ATTN, :])
    v_ref[...] = v
    vb_ref[...] = v.astype(BF16)

    zf = project(wf_ref[...]) + bf_ref[...]
    lf = jnp.minimum(zf, 0.0) - jnp.log1p(jnp.exp(-jnp.abs(zf)))
    lft_ref[...] = jnp.transpose(lf)[:N_HEADS, :]


def _in_proj(x, g, w_main, w_f, b_f, g_q, g_k, *, tt):
    t = x.shape[0]
    assert t % tt == 0
    row = lambda i: (i, 0)
    const = lambda i: (0, 0)
    once = pl.Buffered(1)
    wide = lambda dt: jax.ShapeDtypeStruct((t, D_ATTN), dt)
    return pl.pallas_call(
        _in_proj_kernel,
        grid=(t // tt,),
        in_specs=[
            pl.BlockSpec((tt, D_MODEL), row),
            pl.BlockSpec((1, D_MODEL), const),
            pl.BlockSpec((D_SSM + 3 * D_ATTN, D_MODEL), const, pipeline_mode=once),
            pl.BlockSpec((LANES, D_MODEL), const, pipeline_mode=once),
            pl.BlockSpec((1, LANES), const),
            pl.BlockSpec((1, HEAD_DIM), const),
            pl.BlockSpec((1, HEAD_DIM), const),
        ],
        out_specs=[
            pl.BlockSpec((tt, D_SSM), row),
            pl.BlockSpec((tt, D_ATTN), row),
            pl.BlockSpec((tt, D_ATTN), row),
            pl.BlockSpec((tt, D_ATTN), row),
            pl.BlockSpec((tt, D_ATTN), row),
            pl.BlockSpec((tt, D_ATTN), row),
            pl.BlockSpec((N_HEADS, tt), lambda i: (0, i)),
        ],
        out_shape=[wide(F32), wide(BF16), wide(F32), wide(BF16), wide(F32), wide(BF16),
                   jax.ShapeDtypeStruct((N_HEADS, t), F32)],
        compiler_params=_params(("arbitrary",)),
        name="in_proj",
    )(x, g, w_main, w_f, b_f, g_q, g_k)


def _cumsum_kernel(x_ref, o_ref):
    rows, length = x_ref.shape
    lane = lax.broadcasted_iota(jnp.int32, (rows, LANES), 1)
    carry = jnp.zeros((rows, 1), F32)
    for b in range(length // LANES):
        sl = slice(b * LANES, (b + 1) * LANES)
        x = x_ref[:, sl]
        d = 1
        while d < LANES:
            x = x + jnp.where(lane >= d, pltpu.roll(x, d, axis=1), 0.0)
            d *= 2
        x = x + carry
        o_ref[:, sl] = x * LOG2E
        carry = x[:, LANES - 1:LANES]


def _cumsum_lanes(x):
    rows, length = x.shape
    assert length % LANES == 0 and rows % 8 == 0
    return pl.pallas_call(
        _cumsum_kernel,
        out_shape=jax.ShapeDtypeStruct((rows, length), F32),
        name="cumsum",
    )(x)


def _s5_tables(a_re, a_im, log_step, b_re, b_im, c_re, c_im, d, tc):
    step = jnp.exp(log_step)[:, None]
    mag = jnp.exp(a_re * step)
    abar_re = mag * jnp.cos(a_im * step)
    abar_im = mag * jnp.sin(a_im * step)
    den = a_re * a_re + a_im * a_im
    nr = abar_re - 1.0
    ni = abar_im
    fr = (nr * a_re + ni * a_im) / den
    fi = (ni * a_re - nr * a_im) / den
    bbar_re = fr[..., None] * b_re - fi[..., None] * b_im
    bbar_im = fr[..., None] * b_im + fi[..., None] * b_re
    row_group = lambda n, per: (lax.broadcasted_iota(jnp.int32, (n, 1), 0) // per)
    col_group = lambda n, per: (lax.broadcasted_iota(jnp.int32, (1, n), 1) // per)

    def in_blockdiag(b):
        b = b.reshape(N_SLABS, SLAB_GROUPS, SSM_STATE, SSM_GROUP).transpose(0, 1, 3, 2)
        b = jnp.tile(b.reshape(N_SLABS, SLAB, SSM_STATE), (1, 1, SLAB_GROUPS))
        keep = row_group(SLAB, SSM_GROUP) == col_group(SLAB_STATE, SSM_STATE)
        return jnp.where(keep[None], b, 0.0)

    def out_blockdiag(c):
        c = c.reshape(N_SLABS, SLAB_GROUPS, SSM_GROUP, SSM_STATE).transpose(0, 1, 3, 2)
        c = jnp.tile(c.reshape(N_SLABS, SLAB_STATE, SSM_GROUP), (1, 1, SLAB_GROUPS))
        keep = row_group(SLAB_STATE, SSM_STATE) == col_group(SLAB, SSM_GROUP)
        return jnp.where(keep[None], c, 0.0)

    bb = jnp.concatenate([in_blockdiag(bbar_re), in_blockdiag(bbar_im)], axis=-1).astype(BF16)
    cc = jnp.concatenate([out_blockdiag(c_re), out_blockdiag(-c_im)], axis=1).astype(BF16)
    n = jnp.arange(1, tc + 1, dtype=F32)[:, None, None]
    pmag = jnp.exp(a_re[None] * step[None] * n)
    ang = a_im[None] * step[None] * n
    p_re = (pmag * jnp.cos(ang)).reshape(tc, N_SLABS, SLAB_STATE).transpose(1, 0, 2)
    p_im = (pmag * jnp.sin(ang)).reshape(tc, N_SLABS, SLAB_STATE).transpose(1, 0, 2)
    pw = jnp.concatenate([p_re, p_im], axis=-1)
    dd = d.reshape(N_SLABS, 1, SLAB)
    return bb, cc, pw, dd


def _s5_kernel(u_ref, perm_ref, bb_ref, cc_ref, pw_ref, d_ref, h0_ref, y_ref, hfin_ref,
               st_s, hb_s, g_s, carry_s, *, tc, nc, tiles_per_seq, carry_mode):
    i = pl.program_id(1)
    u = u_ref[...]
    u_hi = u.astype(BF16)
    u_lo = (u - u_hi.astype(F32)).astype(BF16)
    up = jnp.dot(perm_ref[...], jnp.concatenate([u_hi, u_lo], axis=1),
                 preferred_element_type=F32)
    u_hi = up[:, :SLAB]
    u = u_hi + up[:, SLAB:]
    uh = u_hi.astype(BF16)
    y_ref[...] = d_ref[...] * u
    blk = 2 * LANES
    nblk = SLAB_STATE // blk
    grp = max(1, 16 // nc)
    lanes = lambda b: slice(b * blk, (b + 1) * blk)
    step_rows = lambda t: slice(t * nc, (t + 1) * nc)

    if carry_mode:
        @pl.when(i % tiles_per_seq == 0)
        def _():
            carry_s[...] = jnp.zeros_like(carry_s)
    else:
        g_s[...] = h0_ref[...]

    for b in range(nblk):
        re, im = lanes(b), lanes(nblk + b)
        st_s[:, re] = jnp.dot(uh, bb_ref[:, re], preferred_element_type=F32)
        st_s[:, im] = jnp.dot(uh, bb_ref[:, im], preferred_element_type=F32)
        ar = pw_ref[0:1, re]
        ai = pw_ref[0:1, im]
        hr = jnp.zeros((nc, blk), F32)
        hi = jnp.zeros((nc, blk), F32)
        for t in range(tc):
            rows = step_rows(t)
            hr, hi = (ar * hr - ai * hi + st_s[rows, re],
                      ar * hi + ai * hr + st_s[rows, im])
            st_s[rows, re] = hr
            st_s[rows, im] = hi

        ar = pw_ref[tc - 1:tc, re]
        ai = pw_ref[tc - 1:tc, im]
        if carry_mode:
            gr = carry_s[0:1, re]
            gi = carry_s[0:1, im]
            for c in range(nc):
                g_s[c:c + 1, re] = gr
                g_s[c:c + 1, im] = gi
                gr, gi = (ar * gr - ai * gi + hr[c:c + 1, :],
                          ar * gi + ai * gr + hi[c:c + 1, :])
            carry_s[0:1, re] = gr
            carry_s[0:1, im] = gi
            hfin_ref[0:1, re] = gr
            hfin_ref[0:1, im] = gi
            gr = g_s[:, re]
            gi = g_s[:, im]
        else:
            gr = g_s[:, re]
            gi = g_s[:, im]
            hfin_ref[:, re] = ar * gr - ai * gi + hr
            hfin_ref[:, im] = ar * gi + ai * gr + hi

        for t0 in range(0, tc, grp):
            full_r, full_i = [], []
            for t in range(t0, t0 + grp):
                rows = step_rows(t)
                pr = pw_ref[t:t + 1, re]
                pi = pw_ref[t:t + 1, im]
                full_r.append(st_s[rows, re] + (pr * gr - pi * gi))
                full_i.append(st_s[rows, im] + (pr * gi + pi * gr))
            rows = slice(t0 * nc, (t0 + grp) * nc)
            hb_s[rows, re] = jnp.concatenate(full_r, axis=0).astype(BF16)
            hb_s[rows, im] = jnp.concatenate(full_i, axis=0).astype(BF16)

        y_ref[...] += (jnp.dot(hb_s[:, re], cc_ref[re, :], preferred_element_type=F32)
                       + jnp.dot(hb_s[:, im], cc_ref[im, :], preferred_element_type=F32))


def _step_major_perm(tt, tc):
    nc = tt // tc
    r_out = jnp.arange(tt)
    r_in = (r_out % nc) * tc + r_out // nc
    return (r_in[:, None] == jnp.arange(tt)[None, :]).astype(BF16)


def _s5(u, tables, perm, h0, *, tt, tc, seq_len, carry_mode):
    bb, cc, pw, dd = tables
    t = u.shape[0]
    nc = tt // tc
    n_tiles = t // tt
    assert t % tt == 0 and tt % tc == 0
    if carry_mode:
        assert seq_len % tt == 0
        tiles_per_seq = seq_len // tt
        n_seq = t // seq_len
        hfin_shape = (N_SLABS, n_seq, 1, 2 * SLAB_STATE)
        hfin_spec = pl.BlockSpec((None, None, 1, 2 * SLAB_STATE),
                                 lambda s, i: (s, i // tiles_per_seq, 0, 0))
        h0 = jnp.zeros((N_SLABS, 8, 2 * SLAB_STATE), F32)
        h0_spec = pl.BlockSpec((None, 8, 2 * SLAB_STATE), lambda s, i: (s, 0, 0))
    else:
        assert seq_len == tc
        tiles_per_seq = 1
        hfin_shape = (N_SLABS, n_tiles, nc, 2 * SLAB_STATE)
        hfin_spec = pl.BlockSpec((None, None, nc, 2 * SLAB_STATE), lambda s, i: (s, i, 0, 0))
        h0_spec = pl.BlockSpec((None, nc, 2 * SLAB_STATE), lambda s, i: (s, i, 0))
    kern = functools.partial(_s5_kernel, tc=tc, nc=nc, tiles_per_seq=tiles_per_seq,
                             carry_mode=carry_mode)
    y, hfin = pl.pallas_call(
        kern,
        grid=(N_SLABS, n_tiles),
        in_specs=[
            pl.BlockSpec((tt, SLAB), lambda s, i: (i, s)),
            pl.BlockSpec((tt, tt), lambda s, i: (0, 0), pipeline_mode=pl.Buffered(1)),
            pl.BlockSpec((None, SLAB, 2 * SLAB_STATE), lambda s, i: (s, 0, 0)),
            pl.BlockSpec((None, 2 * SLAB_STATE, SLAB), lambda s, i: (s, 0, 0)),
            pl.BlockSpec((None, tc, 2 * SLAB_STATE), lambda s, i: (s, 0, 0)),
            pl.BlockSpec((None, 1, SLAB), lambda s, i: (s, 0, 0)),
            h0_spec,
        ],
        out_specs=[pl.BlockSpec((tt, SLAB), lambda s, i: (i, s)), hfin_spec],
        out_shape=[jax.ShapeDtypeStruct((t, D_SSM), F32),
                   jax.ShapeDtypeStruct(hfin_shape, F32)],
        scratch_shapes=[pltpu.VMEM((tt, 2 * SLAB_STATE), F32),
                        pltpu.VMEM((tt, 2 * SLAB_STATE), BF16),
                        pltpu.VMEM((nc, 2 * SLAB_STATE), F32),
                        pltpu.VMEM((8, 2 * SLAB_STATE), F32)],
        compiler_params=_params(("arbitrary", "arbitrary")),
        name="s5",
    )(u, perm, bb, cc, pw, dd, h0)
    return y, hfin.reshape(N_SLABS, -1, 2 * SLAB_STATE)


def _state_to_gp(hfin):
    n = hfin.shape[1]
    h = hfin.transpose(1, 0, 2)
    re = h[..., :SLAB_STATE].reshape(n, N_SSM_GROUPS, SSM_STATE)
    im = h[..., SLAB_STATE:].reshape(n, N_SSM_GROUPS, SSM_STATE)
    return re, im


def _state_from_gp(re, im):
    n = re.shape[0]
    h = jnp.concatenate([re.reshape(n, N_SLABS, SLAB_STATE), im.reshape(n, N_SLABS, SLAB_STATE)],
                        axis=-1)
    return h.transpose(1, 0, 2)


def _col_from_row(row):
    n = row.shape[1]
    r = lax.broadcasted_iota(jnp.int32, (n, n), 0)
    c = lax.broadcasted_iota(jnp.int32, (n, n), 1)
    return jnp.sum(jnp.where(r == c, jnp.broadcast_to(row, (n, n)), 0.0), axis=1, keepdims=True)


def _attn_prompt_kernel(first_ref, q_ref, k_ref, v_ref, cq_ref, ck_ref, o_ref,
                        m_s, l_s, acc_s, cq_s, s_s, *, bq, online):
    i = pl.program_id(1)
    bk = bq // 2
    cq_s[...] = jnp.transpose(jnp.broadcast_to(cq_ref[...], (LANES, bq)))
    if online:
        m_s[...] = jnp.full_like(m_s, NEG_INF)
    l_s[...] = jnp.zeros_like(l_s)
    acc_s[...] = jnp.zeros_like(acc_s)
    nchunk = bk // LANES
    all_rows = slice(0, bq)

    def scores(j, slot, rows=all_rows):
        start = pl.multiple_of(j * bk, bk)
        s_s[slot, rows, :] = lax.dot_general(q_ref[rows, :], k_ref[pl.ds(start, bk), :],
                                             (((1,), (1,)), ((), ())), preferred_element_type=F32)

    def reduce_block(j, slot, rows=all_rows, causal=False):
        start = pl.multiple_of(j * bk, bk)
        v = v_ref[pl.ds(start, bk), :]
        nrows = rows.stop - rows.start

        def logits(c):
            col0 = pl.multiple_of(start + c * LANES, LANES)
            t = s_s[slot, rows, c * LANES:(c + 1) * LANES] - ck_ref[:, pl.ds(col0, LANES)]
            if causal:
                r = lax.broadcasted_iota(jnp.int32, (nrows, LANES), 0)
                col = lax.broadcasted_iota(jnp.int32, (nrows, LANES), 1) + c * LANES
                t = jnp.where(col <= r, t, NEG_INF)
            return t

        if online:
            m_loc = functools.reduce(jnp.maximum, [logits(c) for c in range(nchunk)])
            m_old = m_s[rows, :]
            m_new = jnp.maximum(m_old, jnp.max(m_loc, axis=1, keepdims=True) + cq_s[rows, :])
            m_s[rows, :] = m_new
            alpha = jnp.exp2(m_old - m_new)
            shift = cq_s[rows, :] - m_new
        else:
            shift = cq_s[rows, :]
        ps = [jnp.exp2(logits(c) + shift) for c in range(nchunk)]
        p = jnp.concatenate([pc.astype(BF16) for pc in ps], axis=1)
        l_new = functools.reduce(jnp.add, ps)
        acc_new = jnp.dot(p, v, preferred_element_type=F32)
        if online:
            l_s[rows, :] = alpha * l_s[rows, :] + l_new
            acc_s[rows, :] = alpha * acc_s[rows, :] + acc_new
        else:
            l_s[rows, :] += l_new
            acc_s[rows, :] += acc_new

    first = first_ref[pl.program_id(0) * pl.num_programs(1) + i]
    scores(2 * first, 0)

    def body(p, carry):
        j = 2 * p
        scores(j + 1, 1)
        reduce_block(j, 0)
        scores(j + 2, 0)
        reduce_block(j + 1, 1)
        return carry

    lax.fori_loop(first, i, body, 0)
    late_rows = slice(bk, bq)
    scores(2 * i + 1, 1, late_rows)
    reduce_block(2 * i, 0, causal=True)
    reduce_block(2 * i + 1, 1, late_rows, causal=True)

    o_ref[...] = acc_s[...] / jnp.sum(l_s[...], axis=1, keepdims=True)


def _attn_prompt(qb, kb, vb, first, cq_rows, ck_rows, *, n_seq, seq_len, bq, online):
    nq = seq_len // bq
    assert seq_len % bq == 0
    kern = functools.partial(_attn_prompt_kernel, bq=bq, online=online)
    q_map = lambda g, i, first: ((g // N_HEADS) * nq + i, g % N_HEADS)
    kv_map = lambda g, i, first: (g // N_HEADS, g % N_HEADS)
    grid_spec = pltpu.PrefetchScalarGridSpec(
        num_scalar_prefetch=1,
        grid=(n_seq * N_HEADS, nq),
        in_specs=[
            pl.BlockSpec((bq, HEAD_DIM), q_map),
            pl.BlockSpec((seq_len, HEAD_DIM), kv_map),
            pl.BlockSpec((seq_len, HEAD_DIM), kv_map),
            pl.BlockSpec((None, 1, bq), lambda g, i, first: (g, 0, i)),
            pl.BlockSpec((None, 1, seq_len), lambda g, i, first: (g, 0, 0)),
        ],
        out_specs=pl.BlockSpec((bq, HEAD_DIM), q_map),
        scratch_shapes=[pltpu.VMEM((bq, LANES), F32), pltpu.VMEM((bq, LANES), F32),
                        pltpu.VMEM((bq, HEAD_DIM), F32), pltpu.VMEM((bq, LANES), F32),
                        pltpu.VMEM((2, bq, bq // 2), F32)],
    )
    return pl.pallas_call(
        kern,
        grid_spec=grid_spec,
        out_shape=jax.ShapeDtypeStruct((n_seq * seq_len, D_ATTN), F32),
        compiler_params=_params(("arbitrary", "arbitrary")),
        name="attn_prompt",
    )(first, qb, kb, vb, cq_rows, ck_rows)


def _first_live_pair(c_rows, bq):
    g, _, length = c_rows.shape
    nq = length // bq
    blocks = c_rows.reshape(g, nq, bq)
    hi = jnp.max(blocks, axis=-1)
    lo = jnp.min(blocks, axis=-1)
    dead = (hi[:, :, None] - lo[:, None, :]) < EXP2_ZERO - 1.0
    pair = jnp.arange(nq, dtype=jnp.int32)
    dead = dead & (pair[None, None, :] < pair[None, :, None])
    first = jnp.min(jnp.where(dead, nq, pair[None, None, :]), axis=-1)
    return first.reshape(g * nq).astype(jnp.int32)


def _attn_sample_kernel(q_ref, kn_ref, vn_ref, ck_ref, cv_ref, c_ref, o_ref, *, past, s_new):
    r = lax.broadcasted_iota(jnp.int32, (s_new, s_new), 0)
    col = lax.broadcasted_iota(jnp.int32, (s_new, s_new), 1)
    causal = col <= r
    nt = (((1,), (1,)), ((), ()))
    heads = range(N_HEADS)
    lanes = lambda h: slice(h * HEAD_DIM, (h + 1) * HEAD_DIM)
    head_rows = lambda h: pl.ds(h, past, stride=N_HEADS)
    logits = []
    for h in heads:
        q = q_ref[:, lanes(h)]
        c_past = c_ref[h:h + 1, 0:past]
        c_new = c_ref[h:h + 1, past:past + s_new]
        cq = _col_from_row(c_new)
        s_p = lax.dot_general(q, ck_ref[head_rows(h), :].astype(BF16), nt,
                              preferred_element_type=F32)
        s_n = lax.dot_general(q, kn_ref[:, lanes(h)], nt, preferred_element_type=F32)
        logits.append((s_p + cq - c_past, jnp.where(causal, s_n + cq - c_new, NEG_INF)))
    probs = []
    for s_p, s_n in logits:
        m = jnp.maximum(jnp.max(s_p, axis=1, keepdims=True), jnp.max(s_n, axis=1, keepdims=True))
        p_p = jnp.exp2(s_p - m)
        p_n = jnp.exp2(s_n - m)
        l = jnp.sum(p_p, axis=1, keepdims=True) + jnp.sum(p_n, axis=1, keepdims=True)
        probs.append((p_p.astype(BF16), p_n.astype(BF16), l))
    for h, (p_p, p_n, l) in zip(heads, probs):
        acc = (jnp.dot(p_p, cv_ref[head_rows(h), :].astype(BF16), preferred_element_type=F32)
               + jnp.dot(p_n, vn_ref[:, lanes(h)], preferred_element_type=F32))
        o_ref[:, lanes(h)] = acc / l


def _attn_sample(qb, kb, vb, cache_k, cache_v, c_all, *, n_seq, s_new, past):
    kern = functools.partial(_attn_sample_kernel, past=past, s_new=s_new)
    new = pl.BlockSpec((s_new, D_ATTN), lambda b: (b, 0))
    cache_k = cache_k.reshape(n_seq, past * N_HEADS, HEAD_DIM)
    cache_v = cache_v.reshape(n_seq, past * N_HEADS, HEAD_DIM)
    cache = pl.BlockSpec((None, past * N_HEADS, HEAD_DIM), lambda b: (b, 0, 0))
    return pl.pallas_call(
        kern,
        grid=(n_seq,),
        in_specs=[new, new, new, cache, cache,
                  pl.BlockSpec((N_HEADS, c_all.shape[1]), lambda b: (b, 0))],
        out_specs=new,
        out_shape=jax.ShapeDtypeStruct((n_seq * s_new, D_ATTN), F32),
        compiler_params=_params(("arbitrary",)),
        name="attn_sample",
    )(qb, kb, vb, cache_k, cache_v, c_all)


def _gelu_tanh(x):
    return 0.5 * x * (1.0 + jnp.tanh(math.sqrt(2.0 / math.pi) * (x + 0.044715 * (x * x * x))))


def _mix_kernel(x_ref, y_ref, a_ref, unperm_ref, wglu_ref, gs_ref, ga_ref, wout_ref, gm_ref,
                x1_ref, hm_ref):
    gy = _gelu_tanh(y_ref[...])
    gate = jax.nn.sigmoid(jnp.dot(gy.astype(BF16), wglu_ref[...], preferred_element_type=F32))
    ssm = _rms(gy * gate, gs_ref[...]).astype(BF16)
    ssm = jnp.dot(unperm_ref[...], ssm, preferred_element_type=F32).astype(BF16)
    att = _rms(a_ref[...], ga_ref[...]).astype(BF16)
    x1 = (x_ref[...]
          + jnp.dot(ssm, wout_ref[0:D_SSM, :], preferred_element_type=F32)
          + jnp.dot(att, wout_ref[D_SSM:D_SSM + D_ATTN, :], preferred_element_type=F32))
    x1_ref[...] = x1
    hm_ref[...] = _rms(x1, gm_ref[...]).astype(BF16)


def _mix(x, y, attn, unperm, w_glu, g_ssm, g_attn, w_out, g_mlp, *, tt):
    t = x.shape[0]
    assert t % tt == 0 and unperm.shape == (tt, tt)
    row = lambda i: (i, 0)
    const = lambda i: (0, 0)
    once = pl.Buffered(1)
    return pl.pallas_call(
        _mix_kernel,
        grid=(t // tt,),
        in_specs=[
            pl.BlockSpec((tt, D_MODEL), row),
            pl.BlockSpec((tt, D_SSM), row),
            pl.BlockSpec((tt, D_ATTN), row),
            pl.BlockSpec((tt, tt), const, pipeline_mode=once),
            pl.BlockSpec((D_SSM, D_SSM), const, pipeline_mode=once),
            pl.BlockSpec((1, D_SSM), const),
            pl.BlockSpec((1, D_ATTN), const),
            pl.BlockSpec((D_SSM + D_ATTN, D_MODEL), const, pipeline_mode=once),
            pl.BlockSpec((1, D_MODEL), const),
        ],
        out_specs=[pl.BlockSpec((tt, D_MODEL), row), pl.BlockSpec((tt, D_MODEL), row)],
        out_shape=[jax.ShapeDtypeStruct((t, D_MODEL), F32), jax.ShapeDtypeStruct((t, D_MODEL), BF16)],
        compiler_params=_params(("arbitrary",)),
        name="mix",
    )(x, y, attn, unperm, w_glu, g_ssm, g_attn, w_out, g_mlp)


def _mlp_kernel(x1_ref, hm_ref, wup_ref, wdn_ref, o_ref):
    j = pl.program_id(1)

    @pl.when(j == 0)
    def _():
        o_ref[...] = x1_ref[...]

    a = jnp.maximum(jnp.dot(hm_ref[...], wup_ref[...], preferred_element_type=F32), 0.0)
    o_ref[...] += jnp.dot((a * a).astype(BF16), wdn_ref[...], preferred_element_type=F32)


def _mlp(x1, hm, w_up, w_down, *, tt, tf):
    t = x1.shape[0]
    assert t % tt == 0 and D_FF % tf == 0
    return pl.pallas_call(
        _mlp_kernel,
        grid=(t // tt, D_FF // tf),
        in_specs=[
            pl.BlockSpec((tt, D_MODEL), lambda i, j: (i, 0)),
            pl.BlockSpec((tt, D_MODEL), lambda i, j: (i, 0)),
            pl.BlockSpec((D_MODEL, tf), lambda i, j: (0, j)),
            pl.BlockSpec((tf, D_MODEL), lambda i, j: (j, 0)),
        ],
        out_specs=pl.BlockSpec((tt, D_MODEL), lambda i, j: (i, 0)),
        out_shape=jax.ShapeDtypeStruct((t, D_MODEL), F32),
        compiler_params=_params(("arbitrary", "arbitrary")),
        name="mlp",
    )(x1, hm, w_up, w_down)


def _layer(x, weights, *, n_seq, seq_len, cache=None, h0=None, tiles):
    (g_mix, w_main, w_f, b_f, g_q, g_k, s5_raw, w_glu, g_ssm, g_attn, w_out, g_mlp,
     w_up, w_down) = weights
    t = n_seq * seq_len
    x2 = x.reshape(t, D_MODEL)
    u, qb, k, kb, v, vb, lf_t = _in_proj(x2, g_mix, w_main, w_f, b_f, g_q, g_k, tt=tiles["proj"])
    lf_t = lf_t.reshape(N_HEADS, n_seq, seq_len)
    logf = lf_t.transpose(1, 2, 0)
    lf_rows = lf_t.transpose(1, 0, 2).reshape(n_seq * N_HEADS, seq_len)

    tc = tiles["chunk"] if cache is None else seq_len
    perm = _step_major_perm(tiles["s5"], tc)
    if cache is None:
        tables = _s5_tables(*s5_raw, tc=tc)
        y, hfin = _s5(u, tables, perm, None, tt=tiles["s5"], tc=tc, seq_len=seq_len,
                      carry_mode=True)
        c_rows = _cumsum_lanes(lf_rows).reshape(n_seq * N_HEADS, 1, seq_len)
        attend = functools.partial(_attn_prompt, qb, kb, vb, n_seq=n_seq, seq_len=seq_len,
                                   bq=tiles["bq"])
        bound = (HEAD_DIM ** -0.5 * LOG2E) * HEAD_DIM * jnp.max(jnp.abs(g_q)) * jnp.max(jnp.abs(g_k))
        attn = lax.cond(
            bound <= MAX_FIXED_SHIFT,
            lambda: attend(_first_live_pair(c_rows, tiles["bq"]), c_rows - bound, c_rows,
                           online=False),
            lambda: attend(jnp.zeros((c_rows.shape[0] * (seq_len // tiles["bq"]),), jnp.int32),
                           c_rows, c_rows, online=True))
    else:
        cache_k, cache_v, cache_logf = cache
        past = cache_k.shape[1]
        tables = _s5_tables(*s5_raw, tc=tc)
        y, hfin = _s5(u, tables, perm, h0, tt=tiles["s5"], tc=tc, seq_len=seq_len,
                      carry_mode=False)
        past_rows = cache_logf.transpose(0, 2, 1).reshape(n_seq * N_HEADS, past)
        total = past + seq_len
        padded = -(-total // LANES) * LANES
        lf_all = jnp.concatenate(
            [past_rows, lf_rows, jnp.zeros((n_seq * N_HEADS, padded - total), F32)], axis=1)
        c_all = _cumsum_lanes(lf_all)
        attn = _attn_sample(qb, kb, vb, cache_k, cache_v, c_all,
                            n_seq=n_seq, s_new=seq_len, past=past)

    x1, hm = _mix(x2, y, attn, perm.T, w_glu, g_ssm, g_attn, w_out, g_mlp, tt=tiles["s5"])
    out = _mlp(x1, hm, w_up, w_down, tt=tiles["mlp"], tf=tiles["tf"])
    h_re, h_im = _state_to_gp(hfin)
    return (out.reshape(n_seq, seq_len, D_MODEL),
            k.reshape(n_seq, seq_len, N_HEADS, HEAD_DIM),
            v.reshape(n_seq, seq_len, N_HEADS, HEAD_DIM),
            logf, h_re, h_im)


def _cast_kernel(w_ref, wf_ref, o_ref, of_ref):
    o_ref[...] = w_ref[...].astype(o_ref.dtype)
    of_ref[...] = jnp.zeros_like(of_ref)
    of_ref[0:N_HEADS, :] = wf_ref[...].astype(of_ref.dtype)


def _cast_w_in(w_t, n_main, rows_per_step=512):
    n_rows, d = w_t.shape
    assert n_main % rows_per_step == 0 and n_rows - n_main == N_HEADS and n_main % N_HEADS == 0
    return pl.pallas_call(
        _cast_kernel,
        grid=(n_main // rows_per_step,),
        in_specs=[pl.BlockSpec((rows_per_step, d), lambda i: (i, 0)),
                  pl.BlockSpec((N_HEADS, d), lambda i: (n_main // N_HEADS, 0))],
        out_specs=[pl.BlockSpec((rows_per_step, d), lambda i: (i, 0)),
                   pl.BlockSpec((LANES, d), lambda i: (0, 0))],
        out_shape=[jax.ShapeDtypeStruct((n_main, d), BF16),
                   jax.ShapeDtypeStruct((LANES, d), BF16)],
        compiler_params=_params(("arbitrary",)),
        name="cast_w_in",
    )(w_t, w_t)


def _prep_weights(l, g_norm_mix, w_in, b_f, ssm_a_re, ssm_a_im, ssm_log_step, ssm_b_re, ssm_b_im,
                  ssm_c_re, ssm_c_im, ssm_d, w_glu, g_q, g_k, g_out_ssm, g_out_attn, w_out,
                  g_norm_mlp, w_up, w_down):
    n_main = D_SSM + 3 * D_ATTN
    w_main, w_f = _cast_w_in(jnp.swapaxes(w_in[l], 0, 1), n_main)
    b = jnp.pad(b_f[l], (0, LANES - N_HEADS)).reshape(1, LANES)
    s5_raw = (ssm_a_re[l], ssm_a_im[l], ssm_log_step[l], ssm_b_re[l], ssm_b_im[l],
              ssm_c_re[l], ssm_c_im[l], ssm_d[l])
    return (g_norm_mix[l].reshape(1, D_MODEL), w_main, w_f, b,
            g_q[l].reshape(1, HEAD_DIM), g_k[l].reshape(1, HEAD_DIM), s5_raw,
            w_glu[l].astype(BF16), g_out_ssm[l].reshape(1, D_SSM), g_out_attn[l].reshape(1, D_ATTN),
            w_out[l].astype(BF16), g_norm_mlp[l].reshape(1, D_MODEL),
            w_up[l].astype(BF16), w_down[l].astype(BF16))


PROMPT_TILES = dict(proj=512, s5=512, chunk=32, bq=1024, mlp=512, tf=2048)
SAMPLE_TILES = dict(proj=512, s5=512, mlp=512, tf=2048)


def kernel(x_prompt, x_sample, cache_k, cache_v, cache_logf, state_ssm_re, state_ssm_im,
           g_norm_mix, w_in, b_f, ssm_a_re, ssm_a_im, ssm_log_step, ssm_b_re, ssm_b_im,
           ssm_c_re, ssm_c_im, ssm_d, w_glu, g_q, g_k, g_out_ssm, g_out_attn, w_out,
           g_norm_mlp, w_up, w_down):
    depth = w_in.shape[0]
    y_p, y_s = x_prompt, x_sample
    outs_p, outs_s = [], []
    for l in range(depth):
        weights = _prep_weights(l, g_norm_mix, w_in, b_f, ssm_a_re, ssm_a_im, ssm_log_step,
                                ssm_b_re, ssm_b_im, ssm_c_re, ssm_c_im, ssm_d, w_glu, g_q, g_k,
                                g_out_ssm, g_out_attn, w_out, g_norm_mlp, w_up, w_down)
        n_p, l_p = y_p.shape[0], y_p.shape[1]
        y_p, *rest_p = _layer(y_p, weights, n_seq=n_p, seq_len=l_p, tiles=PROMPT_TILES)
        n_s, l_s = y_s.shape[0], y_s.shape[1]
        h0 = _state_from_gp(state_ssm_re[l], state_ssm_im[l])
        y_s, *rest_s = _layer(y_s, weights, n_seq=n_s, seq_len=l_s,
                              cache=(cache_k[l], cache_v[l], cache_logf[l]), h0=h0,
                              tiles=SAMPLE_TILES)
        outs_p.append(rest_p)
        outs_s.append(rest_s)
    stack = lambda outs, idx: jnp.stack([o[idx] for o in outs])
    return (y_p, y_s,
            stack(outs_p, 0), stack(outs_p, 1), stack(outs_p, 2), stack(outs_p, 3), stack(outs_p, 4),
            stack(outs_s, 0), stack(outs_s, 1), stack(outs_s, 2), stack(outs_s, 3), stack(outs_s, 4))
```

```python
import functools
import math

import jax
import jax.numpy as jnp
from jax import lax
from jax.experimental import pallas as pl
from jax.experimental.pallas import tpu as pltpu

D_MODEL = 2048
D_SSM = 1024
SSM_GROUP = 16
N_SSM_GROUPS = 64
SSM_STATE = 64
D_ATTN = 1024
HEAD_DIM = 128
N_HEADS = 8
D_FF = 8192
EPS = 1e-6
NEG_INF = -1e30
LOG2E = math.log2(math.e)
MAX_FIXED_SHIFT = 45.0
EXP2_ZERO = -150.0

LANES = 128
SLAB = 256
N_SLABS = D_SSM // SLAB
SLAB_GROUPS = SLAB // SSM_GROUP
SLAB_STATE = SLAB_GROUPS * SSM_STATE
V7X_VMEM_BYTES = 64 * 1024 * 1024
VMEM_LIMIT = V7X_VMEM_BYTES - 2 * 1024 * 1024

F32 = jnp.float32
BF16 = jnp.bfloat16


def _params(sem, vmem=VMEM_LIMIT):
    return pltpu.CompilerParams(dimension_semantics=sem, vmem_limit_bytes=vmem)


def _rms(x, g):
    return x * lax.rsqrt(jnp.mean(x * x, axis=-1, keepdims=True) + EPS) * g


def _in_proj_kernel(x_ref, g_ref, w_ref, wf_ref, bf_ref, gq_ref, gk_ref, perm_ref,
                    u_ref, qb_ref, k_ref, kb_ref, v_ref, vb_ref, lft_ref):
    hb = _rms(x_ref[...], g_ref[...]).astype(BF16)

    def project(w_rows):
        return lax.dot_general(hb, w_rows, (((1,), (1,)), ((), ())), preferred_element_type=F32)

    u = project(w_ref[0:D_SSM, :])
    u_hi = u.astype(BF16)
    u_lo = (u - u_hi.astype(F32)).astype(BF16)
    up = jnp.dot(perm_ref[...], jnp.concatenate([u_hi, u_lo], axis=1),
                 preferred_element_type=F32)
    u_ref[...] = up[:, :D_SSM] + up[:, D_SSM:]

    q = project(w_ref[D_SSM:D_SSM + D_ATTN, :])
    gq = gq_ref[...]
    for h in range(N_HEADS):
        sl = slice(h * HEAD_DIM, (h + 1) * HEAD_DIM)
        qb_ref[:, sl] = (_rms(q[:, sl], gq) * (HEAD_DIM ** -0.5 * LOG2E)).astype(BF16)

    k = project(w_ref[D_SSM + D_ATTN:D_SSM + 2 * D_ATTN, :])
    gk = gk_ref[...]
    for h in range(N_HEADS):
        sl = slice(h * HEAD_DIM, (h + 1) * HEAD_DIM)
        kn = _rms(k[:, sl], gk)
        k_ref[:, sl] = kn
        kb_ref[:, sl] = kn.astype(BF16)

    v = project(w_ref[D_SSM + 2 * D_ATTN:D_SSM + 3 * D_ATTN, :])
    v_ref[...] = v
    vb_ref[...] = v.astype(BF16)

    zf = project(wf_ref[...]) + bf_ref[...]
    lf = jnp.minimum(zf, 0.0) - jnp.log1p(jnp.exp(-jnp.abs(zf)))
    lft_ref[...] = jnp.transpose(lf)[:N_HEADS, :]


def _in_proj(x, g, w_main, w_f, b_f, g_q, g_k, perm):
    t = x.shape[0]
    tt = perm.shape[0]
    assert t % tt == 0
    row = lambda i: (i, 0)
    const = lambda i: (0, 0)
    once = pl.Buffered(1)
    wide = lambda dt: jax.ShapeDtypeStruct((t, D_ATTN), dt)
    return pl.pallas_call(
        _in_proj_kernel,
        grid=(t // tt,),
        in_specs=[
            pl.BlockSpec((tt, D_MODEL), row),
            pl.BlockSpec((1, D_MODEL), const),
            pl.BlockSpec((D_SSM + 3 * D_ATTN, D_MODEL), const, pipeline_mode=once),
            pl.BlockSpec((LANES, D_MODEL), const, pipeline_mode=once),
            pl.BlockSpec((1, LANES), const),
            pl.BlockSpec((1, HEAD_DIM), const),
            pl.BlockSpec((1, HEAD_DIM), const),
            pl.BlockSpec((tt, tt), const, pipeline_mode=once),
        ],
        out_specs=[
            pl.BlockSpec((tt, D_SSM), row),
            pl.BlockSpec((tt, D_ATTN), row),
            pl.BlockSpec((tt, D_ATTN), row),
            pl.BlockSpec((tt, D_ATTN), row),
            pl.BlockSpec((tt, D_ATTN), row),
            pl.BlockSpec((tt, D_ATTN), row),
            pl.BlockSpec((N_HEADS, tt), lambda i: (0, i)),
        ],
        out_shape=[wide(F32), wide(BF16), wide(F32), wide(BF16), wide(F32), wide(BF16),
                   jax.ShapeDtypeStruct((N_HEADS, t), F32)],
        compiler_params=_params(("arbitrary",)),
        name="in_proj",
    )(x, g, w_main, w_f, b_f, g_q, g_k, perm)


def _cumsum_kernel(x_ref, o_ref):
    rows, length = x_ref.shape
    upper = (lax.broadcasted_iota(jnp.int32, (LANES, LANES), 0)
             <= lax.broadcasted_iota(jnp.int32, (LANES, LANES), 1)).astype(BF16)
    carry = jnp.zeros((rows, 1), F32)
    for b in range(length // LANES):
        sl = slice(b * LANES, (b + 1) * LANES)
        x = x_ref[:, sl]
        hi = x.astype(BF16)
        rest = x - hi.astype(F32)
        mid = rest.astype(BF16)
        lo = (rest - mid.astype(F32)).astype(BF16)
        x = (jnp.dot(hi, upper, preferred_element_type=F32)
             + jnp.dot(mid, upper, preferred_element_type=F32)
             + jnp.dot(lo, upper, preferred_element_type=F32)) + carry
        o_ref[:, sl] = x * LOG2E
        carry = x[:, LANES - 1:LANES]


def _cumsum_lanes(x):
    rows, length = x.shape
    assert length % LANES == 0 and rows % 8 == 0
    return pl.pallas_call(
        _cumsum_kernel,
        out_shape=jax.ShapeDtypeStruct((rows, length), F32),
        name="cumsum",
    )(x)


def _s5_tables(a_re, a_im, log_step, b_re, b_im, c_re, c_im, d, tc):
    step = jnp.exp(log_step)[:, None]
    mag = jnp.exp(a_re * step)
    abar_re = mag * jnp.cos(a_im * step)
    abar_im = mag * jnp.sin(a_im * step)
    den = a_re * a_re + a_im * a_im
    nr = abar_re - 1.0
    ni = abar_im
    fr = (nr * a_re + ni * a_im) / den
    fi = (ni * a_re - nr * a_im) / den
    bbar_re = fr[..., None] * b_re - fi[..., None] * b_im
    bbar_im = fr[..., None] * b_im + fi[..., None] * b_re
    row_group = lambda n, per: (lax.broadcasted_iota(jnp.int32, (n, 1), 0) // per)
    col_group = lambda n, per: (lax.broadcasted_iota(jnp.int32, (1, n), 1) // per)

    def in_blockdiag(b):
        b = b.reshape(N_SLABS, SLAB_GROUPS, SSM_STATE, SSM_GROUP).transpose(0, 1, 3, 2)
        b = jnp.tile(b.reshape(N_SLABS, SLAB, SSM_STATE), (1, 1, SLAB_GROUPS))
        keep = row_group(SLAB, SSM_GROUP) == col_group(SLAB_STATE, SSM_STATE)
        return jnp.where(keep[None], b, 0.0)

    def out_blockdiag(c):
        c = c.reshape(N_SLABS, SLAB_GROUPS, SSM_GROUP, SSM_STATE).transpose(0, 1, 3, 2)
        c = jnp.tile(c.reshape(N_SLABS, SLAB_STATE, SSM_GROUP), (1, 1, SLAB_GROUPS))
        keep = row_group(SLAB_STATE, SSM_STATE) == col_group(SLAB, SSM_GROUP)
        return jnp.where(keep[None], c, 0.0)

    bb = jnp.concatenate([in_blockdiag(bbar_re), in_blockdiag(bbar_im)], axis=-1).astype(BF16)
    cc = jnp.concatenate([out_blockdiag(c_re), out_blockdiag(-c_im)], axis=1).astype(BF16)
    n = jnp.arange(1, tc + 1, dtype=F32)[:, None, None]
    pmag = jnp.exp(a_re[None] * step[None] * n)
    ang = a_im[None] * step[None] * n
    p_re = (pmag * jnp.cos(ang)).reshape(tc, N_SLABS, SLAB_STATE).transpose(1, 0, 2)
    p_im = (pmag * jnp.sin(ang)).reshape(tc, N_SLABS, SLAB_STATE).transpose(1, 0, 2)
    pw = jnp.concatenate([p_re, p_im], axis=-1)
    dd = d.reshape(N_SLABS, 1, SLAB)
    return bb, cc, pw, dd


def _s5_kernel(u_ref, bb_ref, cc_ref, pw_ref, d_ref, h0_ref, y_ref, hfin_ref,
               st_s, hb_s, g_s, carry_s, *, tc, nc, tiles_per_seq, carry_mode):
    i = pl.program_id(1)
    u = u_ref[...]
    uh = u.astype(BF16)
    y_ref[...] = d_ref[...] * u
    blk = 2 * LANES
    nblk = SLAB_STATE // blk
    grp = max(1, 16 // nc)
    lanes = lambda b: slice(b * blk, (b + 1) * blk)
    step_rows = lambda t: slice(t * nc, (t + 1) * nc)

    if carry_mode:
        @pl.when(i % tiles_per_seq == 0)
        def _():
            carry_s[...] = jnp.zeros_like(carry_s)
    else:
        g_s[...] = h0_ref[...]

    for b in range(nblk):
        re, im = lanes(b), lanes(nblk + b)
        st_s[:, re] = jnp.dot(uh, bb_ref[:, re], preferred_element_type=F32)
        st_s[:, im] = jnp.dot(uh, bb_ref[:, im], preferred_element_type=F32)
        ar = pw_ref[0:1, re]
        ai = pw_ref[0:1, im]
        hr = jnp.zeros((nc, blk), F32)
        hi = jnp.zeros((nc, blk), F32)
        for t in range(tc):
            rows = step_rows(t)
            hr, hi = (ar * hr - ai * hi + st_s[rows, re],
                      ar * hi + ai * hr + st_s[rows, im])
            st_s[rows, re] = hr
            st_s[rows, im] = hi

        ar = pw_ref[tc - 1:tc, re]
        ai = pw_ref[tc - 1:tc, im]
        if carry_mode:
            gr = carry_s[0:1, re]
            gi = carry_s[0:1, im]
            for c in range(nc):
                g_s[c:c + 1, re] = gr
                g_s[c:c + 1, im] = gi
                gr, gi = (ar * gr - ai * gi + hr[c:c + 1, :],
                          ar * gi + ai * gr + hi[c:c + 1, :])
            carry_s[0:1, re] = gr
            carry_s[0:1, im] = gi
            hfin_ref[0:1, re] = gr
            hfin_ref[0:1, im] = gi
            gr = g_s[:, re]
            gi = g_s[:, im]
        else:
            gr = g_s[:, re]
            gi = g_s[:, im]
            hfin_ref[:, re] = ar * gr - ai * gi + hr
            hfin_ref[:, im] = ar * gi + ai * gr + hi

        for t0 in range(0, tc, grp):
            full_r, full_i = [], []
            for t in range(t0, t0 + grp):
                rows = step_rows(t)
                pr = pw_ref[t:t + 1, re]
                pi = pw_ref[t:t + 1, im]
                full_r.append(st_s[rows, re] + (pr * gr - pi * gi))
                full_i.append(st_s[rows, im] + (pr * gi + pi * gr))
            rows = slice(t0 * nc, (t0 + grp) * nc)
            hb_s[rows, re] = jnp.concatenate(full_r, axis=0).astype(BF16)
            hb_s[rows, im] = jnp.concatenate(full_i, axis=0).astype(BF16)

        y_ref[...] += (jnp.dot(hb_s[:, re], cc_ref[re, :], preferred_element_type=F32)
                       + jnp.dot(hb_s[:, im], cc_ref[im, :], preferred_element_type=F32))


def _step_major_perm(tt, tc):
    nc = tt // tc
    r_out = jnp.arange(tt)
    r_in = (r_out % nc) * tc + r_out // nc
    return (r_in[:, None] == jnp.arange(tt)[None, :]).astype(BF16)


def _s5(u, tables, h0, *, tt, tc, seq_len, carry_mode):
    bb, cc, pw, dd = tables
    t = u.shape[0]
    nc = tt // tc
    n_tiles = t // tt
    assert t % tt == 0 and tt % tc == 0
    if carry_mode:
        assert seq_len % tt == 0
        tiles_per_seq = seq_len // tt
        n_seq = t // seq_len
        hfin_shape = (N_SLABS, n_seq, 1, 2 * SLAB_STATE)
        hfin_spec = pl.BlockSpec((None, None, 1, 2 * SLAB_STATE),
                                 lambda s, i: (s, i // tiles_per_seq, 0, 0))
        h0 = jnp.zeros((N_SLABS, 8, 2 * SLAB_STATE), F32)
        h0_spec = pl.BlockSpec((None, 8, 2 * SLAB_STATE), lambda s, i: (s, 0, 0))
    else:
        assert seq_len == tc
        tiles_per_seq = 1
        hfin_shape = (N_SLABS, n_tiles, nc, 2 * SLAB_STATE)
        hfin_spec = pl.BlockSpec((None, None, nc, 2 * SLAB_STATE), lambda s, i: (s, i, 0, 0))
        h0_spec = pl.BlockSpec((None, nc, 2 * SLAB_STATE), lambda s, i: (s, i, 0))
    kern = functools.partial(_s5_kernel, tc=tc, nc=nc, tiles_per_seq=tiles_per_seq,
                             carry_mode=carry_mode)
    y, hfin = pl.pallas_call(
        kern,
        grid=(N_SLABS, n_tiles),
        in_specs=[
            pl.BlockSpec((tt, SLAB), lambda s, i: (i, s)),
            pl.BlockSpec((None, SLAB, 2 * SLAB_STATE), lambda s, i: (s, 0, 0)),
            pl.BlockSpec((None, 2 * SLAB_STATE, SLAB), lambda s, i: (s, 0, 0)),
            pl.BlockSpec((None, tc, 2 * SLAB_STATE), lambda s, i: (s, 0, 0)),
            pl.BlockSpec((None, 1, SLAB), lambda s, i: (s, 0, 0)),
            h0_spec,
        ],
        out_specs=[pl.BlockSpec((tt, SLAB), lambda s, i: (i, s)), hfin_spec],
        out_shape=[jax.ShapeDtypeStruct((t, D_SSM), F32),
                   jax.ShapeDtypeStruct(hfin_shape, F32)],
        scratch_shapes=[pltpu.VMEM((tt, 2 * SLAB_STATE), F32),
                        pltpu.VMEM((tt, 2 * SLAB_STATE), BF16),
                        pltpu.VMEM((nc, 2 * SLAB_STATE), F32),
                        pltpu.VMEM((8, 2 * SLAB_STATE), F32)],
        compiler_params=_params(("arbitrary", "arbitrary")),
        name="s5",
    )(u, bb, cc, pw, dd, h0)
    return y, hfin.reshape(N_SLABS, -1, 2 * SLAB_STATE)


def _state_to_gp(hfin):
    n = hfin.shape[1]
    h = hfin.transpose(1, 0, 2)
    re = h[..., :SLAB_STATE].reshape(n, N_SSM_GROUPS, SSM_STATE)
    im = h[..., SLAB_STATE:].reshape(n, N_SSM_GROUPS, SSM_STATE)
    return re, im


def _state_from_gp(re, im):
    n = re.shape[0]
    h = jnp.concatenate([re.reshape(n, N_SLABS, SLAB_STATE), im.reshape(n, N_SLABS, SLAB_STATE)],
                        axis=-1)
    return h.transpose(1, 0, 2)


def _col_from_row(row):
    n = row.shape[1]
    r = lax.broadcasted_iota(jnp.int32, (n, n), 0)
    c = lax.broadcasted_iota(jnp.int32, (n, n), 1)
    return jnp.sum(jnp.where(r == c, jnp.broadcast_to(row, (n, n)), 0.0), axis=1, keepdims=True)


def _attn_prompt_kernel(first_ref, q_ref, k_ref, v_ref, cq_ref, ck_ref, o_ref,
                        m_s, l_s, acc_s, cq_s, s_s, *, bq, online):
    i = pl.program_id(1)
    bk = bq // 2
    cq_s[...] = jnp.transpose(jnp.broadcast_to(cq_ref[...], (LANES, bq)))
    if online:
        m_s[...] = jnp.full_like(m_s, NEG_INF)
    l_s[...] = jnp.zeros_like(l_s)
    acc_s[...] = jnp.zeros_like(acc_s)
    nchunk = bk // LANES
    all_rows = slice(0, bq)

    def scores(j, slot, rows=all_rows):
        start = pl.multiple_of(j * bk, bk)
        s_s[slot, rows, :] = lax.dot_general(q_ref[rows, :], k_ref[pl.ds(start, bk), :],
                                             (((1,), (1,)), ((), ())), preferred_element_type=F32)

    def reduce_block(j, slot, rows=all_rows, causal=False):
        start = pl.multiple_of(j * bk, bk)
        v = v_ref[pl.ds(start, bk), :]
        nrows = rows.stop - rows.start

        def logits(c):
            col0 = pl.multiple_of(start + c * LANES, LANES)
            t = s_s[slot, rows, c * LANES:(c + 1) * LANES] - ck_ref[:, pl.ds(col0, LANES)]
            if causal:
                r = lax.broadcasted_iota(jnp.int32, (nrows, LANES), 0)
                col = lax.broadcasted_iota(jnp.int32, (nrows, LANES), 1) + c * LANES
                t = jnp.where(col <= r, t, NEG_INF)
            return t

        if online:
            m_loc = functools.reduce(jnp.maximum, [logits(c) for c in range(nchunk)])
            m_old = m_s[rows, :]
            m_new = jnp.maximum(m_old, jnp.max(m_loc, axis=1, keepdims=True) + cq_s[rows, :])
            m_s[rows, :] = m_new
            alpha = jnp.exp2(m_old - m_new)
            shift = cq_s[rows, :] - m_new
        else:
            shift = cq_s[rows, :]
        ps = [jnp.exp2(logits(c) + shift) for c in range(nchunk)]
        p = jnp.concatenate([pc.astype(BF16) for pc in ps], axis=1)
        l_new = functools.reduce(jnp.add, ps)
        acc_new = jnp.dot(p, v, preferred_element_type=F32)
        if online:
            l_s[rows, :] = alpha * l_s[rows, :] + l_new
            acc_s[rows, :] = alpha * acc_s[rows, :] + acc_new
        else:
            l_s[rows, :] += l_new
            acc_s[rows, :] += acc_new

    first = first_ref[pl.program_id(0) * pl.num_programs(1) + i]
    scores(2 * first, 0)

    def body(p, carry):
        j = 2 * p
        scores(j + 1, 1)
        reduce_block(j, 0)
        scores(j + 2, 0)
        reduce_block(j + 1, 1)
        return carry

    lax.fori_loop(first, i, body, 0)
    late_rows = slice(bk, bq)
    scores(2 * i + 1, 1, late_rows)
    reduce_block(2 * i, 0, causal=True)
    reduce_block(2 * i + 1, 1, late_rows, causal=True)

    o_ref[...] = acc_s[...] / jnp.sum(l_s[...], axis=1, keepdims=True)


def _attn_prompt(qb, kb, vb, first, cq_rows, ck_rows, *, n_seq, seq_len, bq, online):
    nq = seq_len // bq
    assert seq_len % bq == 0
    kern = functools.partial(_attn_prompt_kernel, bq=bq, online=online)
    q_map = lambda g, i, first: ((g // N_HEADS) * nq + i, g % N_HEADS)
    kv_map = lambda g, i, first: (g // N_HEADS, g % N_HEADS)
    grid_spec = pltpu.PrefetchScalarGridSpec(
        num_scalar_prefetch=1,
        grid=(n_seq * N_HEADS, nq),
        in_specs=[
            pl.BlockSpec((bq, HEAD_DIM), q_map),
            pl.BlockSpec((seq_len, HEAD_DIM), kv_map),
            pl.BlockSpec((seq_len, HEAD_DIM), kv_map),
            pl.BlockSpec((None, 1, bq), lambda g, i, first: (g, 0, i)),
            pl.BlockSpec((None, 1, seq_len), lambda g, i, first: (g, 0, 0)),
        ],
        out_specs=pl.BlockSpec((bq, HEAD_DIM), q_map),
        scratch_shapes=[pltpu.VMEM((bq, LANES), F32), pltpu.VMEM((bq, LANES), F32),
                        pltpu.VMEM((bq, HEAD_DIM), F32), pltpu.VMEM((bq, LANES), F32),
                        pltpu.VMEM((2, bq, bq // 2), F32)],
    )
    return pl.pallas_call(
        kern,
        grid_spec=grid_spec,
        out_shape=jax.ShapeDtypeStruct((n_seq * seq_len, D_ATTN), F32),
        compiler_params=_params(("arbitrary", "arbitrary")),
        name="attn_prompt",
    )(first, qb, kb, vb, cq_rows, ck_rows)


def _first_live_pair(c_rows, bq):
    g, _, length = c_rows.shape
    nq = length // bq
    blocks = c_rows.reshape(g, nq, bq)
    hi = jnp.max(blocks, axis=-1)
    lo = jnp.min(blocks, axis=-1)
    dead = (hi[:, :, None] - lo[:, None, :]) < EXP2_ZERO - 1.0
    pair = jnp.arange(nq, dtype=jnp.int32)
    dead = dead & (pair[None, None, :] < pair[None, :, None])
    first = jnp.min(jnp.where(dead, nq, pair[None, None, :]), axis=-1)
    return first.reshape(g * nq).astype(jnp.int32)


def _attn_sample_kernel(q_ref, kn_ref, vn_ref, ck_ref, cv_ref, c_ref, o_ref, *, past, s_new):
    r = lax.broadcasted_iota(jnp.int32, (s_new, s_new), 0)
    col = lax.broadcasted_iota(jnp.int32, (s_new, s_new), 1)
    causal = col <= r
    nt = (((1,), (1,)), ((), ()))
    heads = range(N_HEADS)
    lanes = lambda h: slice(h * HEAD_DIM, (h + 1) * HEAD_DIM)
    head_rows = lambda h: pl.ds(h, past, stride=N_HEADS)
    logits = []
    for h in heads:
        q = q_ref[:, lanes(h)]
        c_past = c_ref[h:h + 1, 0:past]
        c_new = c_ref[h:h + 1, past:past + s_new]
        cq = _col_from_row(c_new)
        s_p = lax.dot_general(q, ck_ref[head_rows(h), :].astype(BF16), nt,
                              preferred_element_type=F32)
        s_n = lax.dot_general(q, kn_ref[:, lanes(h)], nt, preferred_element_type=F32)
        logits.append((s_p + cq - c_past, jnp.where(causal, s_n + cq - c_new, NEG_INF)))
    probs = []
    for s_p, s_n in logits:
        m = jnp.maximum(jnp.max(s_p, axis=1, keepdims=True), jnp.max(s_n, axis=1, keepdims=True))
        p_p = jnp.exp2(s_p - m)
        p_n = jnp.exp2(s_n - m)
        l = jnp.sum(p_p, axis=1, keepdims=True) + jnp.sum(p_n, axis=1, keepdims=True)
        probs.append((p_p.astype(BF16), p_n.astype(BF16), l))
    for h, (p_p, p_n, l) in zip(heads, probs):
        acc = (jnp.dot(p_p, cv_ref[head_rows(h), :].astype(BF16), preferred_element_type=F32)
               + jnp.dot(p_n, vn_ref[:, lanes(h)], preferred_element_type=F32))
        o_ref[:, lanes(h)] = acc / l


def _attn_sample(qb, kb, vb, cache_k, cache_v, c_all, *, n_seq, s_new, past):
    kern = functools.partial(_attn_sample_kernel, past=past, s_new=s_new)
    new = pl.BlockSpec((s_new, D_ATTN), lambda b: (b, 0))
    cache_k = cache_k.reshape(n_seq, past * N_HEADS, HEAD_DIM)
    cache_v = cache_v.reshape(n_seq, past * N_HEADS, HEAD_DIM)
    cache = pl.BlockSpec((None, past * N_HEADS, HEAD_DIM), lambda b: (b, 0, 0))
    return pl.pallas_call(
        kern,
        grid=(n_seq,),
        in_specs=[new, new, new, cache, cache,
                  pl.BlockSpec((N_HEADS, c_all.shape[1]), lambda b: (b, 0))],
        out_specs=new,
        out_shape=jax.ShapeDtypeStruct((n_seq * s_new, D_ATTN), F32),
        compiler_params=_params(("arbitrary",)),
        name="attn_sample",
    )(qb, kb, vb, cache_k, cache_v, c_all)


def _gelu_tanh(x):
    return 0.5 * x * (1.0 + jnp.tanh(math.sqrt(2.0 / math.pi) * (x + 0.044715 * (x * x * x))))


def _mix_kernel(x_ref, y_ref, a_ref, unperm_ref, wglu_ref, gs_ref, ga_ref, wout_ref, gm_ref,
                x1_ref, hm_ref):
    gy = _gelu_tanh(y_ref[...])
    gate = jax.nn.sigmoid(jnp.dot(gy.astype(BF16), wglu_ref[...], preferred_element_type=F32))
    ssm = _rms(gy * gate, gs_ref[...]).astype(BF16)
    ssm = jnp.dot(unperm_ref[...], ssm, preferred_element_type=F32).astype(BF16)
    att = _rms(a_ref[...], ga_ref[...]).astype(BF16)
    x1 = (x_ref[...]
          + jnp.dot(ssm, wout_ref[0:D_SSM, :], preferred_element_type=F32)
          + jnp.dot(att, wout_ref[D_SSM:D_SSM + D_ATTN, :], preferred_element_type=F32))
    x1_ref[...] = x1
    hm_ref[...] = _rms(x1, gm_ref[...]).astype(BF16)


def _mix(x, y, attn, unperm, w_glu, g_ssm, g_attn, w_out, g_mlp, *, tt):
    t = x.shape[0]
    assert t % tt == 0 and unperm.shape == (tt, tt)
    row = lambda i: (i, 0)
    const = lambda i: (0, 0)
    once = pl.Buffered(1)
    return pl.pallas_call(
        _mix_kernel,
        grid=(t // tt,),
        in_specs=[
            pl.BlockSpec((tt, D_MODEL), row),
            pl.BlockSpec((tt, D_SSM), row),
            pl.BlockSpec((tt, D_ATTN), row),
            pl.BlockSpec((tt, tt), const, pipeline_mode=once),
            pl.BlockSpec((D_SSM, D_SSM), const, pipeline_mode=once),
            pl.BlockSpec((1, D_SSM), const),
            pl.BlockSpec((1, D_ATTN), const),
            pl.BlockSpec((D_SSM + D_ATTN, D_MODEL), const, pipeline_mode=once),
            pl.BlockSpec((1, D_MODEL), const),
        ],
        out_specs=[pl.BlockSpec((tt, D_MODEL), row), pl.BlockSpec((tt, D_MODEL), row)],
        out_shape=[jax.ShapeDtypeStruct((t, D_MODEL), F32), jax.ShapeDtypeStruct((t, D_MODEL), BF16)],
        compiler_params=_params(("arbitrary",)),
        name="mix",
    )(x, y, attn, unperm, w_glu, g_ssm, g_attn, w_out, g_mlp)


def _mlp_kernel(x1_ref, hm_ref, wup_ref, wdn_ref, o_ref):
    j = pl.program_id(1)

    @pl.when(j == 0)
    def _():
        o_ref[...] = x1_ref[...]

    a = jnp.maximum(jnp.dot(hm_ref[...], wup_ref[...], preferred_element_type=F32), 0.0)
    o_ref[...] += jnp.dot((a * a).astype(BF16), wdn_ref[...], preferred_element_type=F32)


def _mlp(x1, hm, w_up, w_down, *, tt, tf):
    t = x1.shape[0]
    assert t % tt == 0 and D_FF % tf == 0
    return pl.pallas_call(
        _mlp_kernel,
        grid=(t // tt, D_FF // tf),
        in_specs=[
            pl.BlockSpec((tt, D_MODEL), lambda i, j: (i, 0)),
            pl.BlockSpec((tt, D_MODEL), lambda i, j: (i, 0)),
            pl.BlockSpec((D_MODEL, tf), lambda i, j: (0, j)),
            pl.BlockSpec((tf, D_MODEL), lambda i, j: (j, 0)),
        ],
        out_specs=pl.BlockSpec((tt, D_MODEL), lambda i, j: (i, 0)),
        out_shape=jax.ShapeDtypeStruct((t, D_MODEL), F32),
        compiler_params=_params(("arbitrary", "arbitrary")),
        name="mlp",
    )(x1, hm, w_up, w_down)


def _layer(x, weights, *, n_seq, seq_len, cache=None, h0=None, tiles):
    (g_mix, w_main, w_f, b_f, g_q, g_k, s5_raw, w_glu, g_ssm, g_attn, w_out, g_mlp,
     w_up, w_down) = weights
    t = n_seq * seq_len
    x2 = x.reshape(t, D_MODEL)
    tc = tiles["chunk"] if cache is None else seq_len
    perm = _step_major_perm(tiles["s5"], tc)
    u, qb, k, kb, v, vb, lf_t = _in_proj(x2, g_mix, w_main, w_f, b_f, g_q, g_k, perm)
    lf_t = lf_t.reshape(N_HEADS, n_seq, seq_len)
    logf = lf_t.transpose(1, 2, 0)
    lf_rows = lf_t.transpose(1, 0, 2).reshape(n_seq * N_HEADS, seq_len)

    if cache is None:
        tables = _s5_tables(*s5_raw, tc=tc)
        y, hfin = _s5(u, tables, None, tt=tiles["s5"], tc=tc, seq_len=seq_len,
                      carry_mode=True)
        c_rows = _cumsum_lanes(lf_rows).reshape(n_seq * N_HEADS, 1, seq_len)
        attend = functools.partial(_attn_prompt, qb, kb, vb, n_seq=n_seq, seq_len=seq_len,
                                   bq=tiles["bq"])
        bound = (HEAD_DIM ** -0.5 * LOG2E) * HEAD_DIM * jnp.max(jnp.abs(g_q)) * jnp.max(jnp.abs(g_k))
        attn = lax.cond(
            bound <= MAX_FIXED_SHIFT,
            lambda: attend(_first_live_pair(c_rows, tiles["bq"]), c_rows - bound, c_rows,
                           online=False),
            lambda: attend(jnp.zeros((c_rows.shape[0] * (seq_len // tiles["bq"]),), jnp.int32),
                           c_rows, c_rows, online=True))
    else:
        cache_k, cache_v, cache_logf = cache
        past = cache_k.shape[1]
        tables = _s5_tables(*s5_raw, tc=tc)
        y, hfin = _s5(u, tables, h0, tt=tiles["s5"], tc=tc, seq_len=seq_len,
                      carry_mode=False)
        past_rows = cache_logf.transpose(0, 2, 1).reshape(n_seq * N_HEADS, past)
        total = past + seq_len
        padded = -(-total // LANES) * LANES
        lf_all = jnp.concatenate(
            [past_rows, lf_rows, jnp.zeros((n_seq * N_HEADS, padded - total), F32)], axis=1)
        c_all = _cumsum_lanes(lf_all)
        attn = _attn_sample(qb, kb, vb, cache_k, cache_v, c_all,
                            n_seq=n_seq, s_new=seq_len, past=past)

    x1, hm = _mix(x2, y, attn, perm.T, w_glu, g_ssm, g_attn, w_out, g_mlp, tt=tiles["s5"])
    out = _mlp(x1, hm, w_up, w_down, tt=tiles["mlp"], tf=tiles["tf"])
    h_re, h_im = _state_to_gp(hfin)
    return (out.reshape(n_seq, seq_len, D_MODEL),
            k.reshape(n_seq, seq_len, N_HEADS, HEAD_DIM),
            v.reshape(n_seq, seq_len, N_HEADS, HEAD_DIM),
            logf, h_re, h_im)


def _cast_kernel(w_ref, wf_ref, o_ref, of_ref):
    o_ref[...] = w_ref[...].astype(o_ref.dtype)
    of_ref[...] = jnp.zeros_like(of_ref)
    of_ref[0:N_HEADS, :] = wf_ref[...].astype(of_ref.dtype)


def _cast_w_in(w_t, n_main, rows_per_step=512):
    n_rows, d = w_t.shape
    assert n_main % rows_per_step == 0 and n_rows - n_main == N_HEADS and n_main % N_HEADS == 0
    return pl.pallas_call(
        _cast_kernel,
        grid=(n_main // rows_per_step,),
        in_specs=[pl.BlockSpec((rows_per_step, d), lambda i: (i, 0)),
                  pl.BlockSpec((N_HEADS, d), lambda i: (n_main // N_HEADS, 0))],
        out_specs=[pl.BlockSpec((rows_per_step, d), lambda i: (i, 0)),
                   pl.BlockSpec((LANES, d), lambda i: (0, 0))],
        out_shape=[jax.ShapeDtypeStruct((n_main, d), BF16),
                   jax.ShapeDtypeStruct((LANES, d), BF16)],
        compiler_params=_params(("arbitrary",)),
        name="cast_w_in",
    )(w_t, w_t)


def _prep_weights(l, g_norm_mix, w_in, b_f, ssm_a_re, ssm_a_im, ssm_log_step, ssm_b_re, ssm_b_im,
                  ssm_c_re, ssm_c_im, ssm_d, w_glu, g_q, g_k, g_out_ssm, g_out_attn, w_out,
                  g_norm_mlp, w_up, w_down):
    n_main = D_SSM + 3 * D_ATTN
    w_main, w_f = _cast_w_in(jnp.swapaxes(w_in[l], 0, 1), n_main)
    b = jnp.pad(b_f[l], (0, LANES - N_HEADS)).reshape(1, LANES)
    s5_raw = (ssm_a_re[l], ssm_a_im[l], ssm_log_step[l], ssm_b_re[l], ssm_b_im[l],
              ssm_c_re[l], ssm_c_im[l], ssm_d[l])
    return (g_norm_mix[l].reshape(1, D_MODEL), w_main, w_f, b,
            g_q[l].reshape(1, HEAD_DIM), g_k[l].reshape(1, HEAD_DIM), s5_raw,
            w_glu[l].astype(BF16), g_out_ssm[l].reshape(1, D_SSM), g_out_attn[l].reshape(1, D_ATTN),
            w_out[l].astype(BF16), g_norm_mlp[l].reshape(1, D_MODEL),
            w_up[l].astype(BF16), w_down[l].astype(BF16))


PROMPT_TILES = dict(s5=512, chunk=32, bq=1024, mlp=512, tf=2048)
SAMPLE_TILES = dict(s5=512, mlp=512, tf=2048)


def kernel(x_prompt, x_sample, cache_k, cache_v, cache_logf, state_ssm_re, state_ssm_im,
           g_norm_mix, w_in, b_f, ssm_a_re, ssm_a_im, ssm_log_step, ssm_b_re, ssm_b_im,
           ssm_c_re, ssm_c_im, ssm_d, w_glu, g_q, g_k, g_out_ssm, g_out_attn, w_out,
           g_norm_mlp, w_up, w_down):
    depth = w_in.shape[0]
    y_p, y_s = x_prompt, x_sample
    outs_p, outs_s = [], []
    for l in range(depth):
        weights = _prep_weights(l, g_norm_mix, w_in, b_f, ssm_a_re, ssm_a_im, ssm_log_step,
                                ssm_b_re, ssm_b_im, ssm_c_re, ssm_c_im, ssm_d, w_glu, g_q, g_k,
                                g_out_ssm, g_out_attn, w_out, g_norm_mlp, w_up, w_down)
        n_p, l_p = y_p.shape[0], y_p.shape[1]
        y_p, *rest_p = _layer(y_p, weights, n_seq=n_p, seq_len=l_p, tiles=PROMPT_TILES)
        n_s, l_s = y_s.shape[0], y_s.shape[1]
        h0 = _state_from_gp(state_ssm_re[l], state_ssm_im[l])
        y_s, *rest_s = _layer(y_s, weights, n_seq=n_s, seq_len=l_s,
                              cache=(cache_k[l], cache_v[l], cache_logf[l]), h0=h0,
                              tiles=SAMPLE_TILES)
        outs_p.append(rest_p)
        outs_s.append(rest_s)
    stack = lambda outs, idx: jnp.stack([o[idx] for o in outs])
    return (y_p, y_s,
            stack(outs_p, 0), stack(outs_p, 1), stack(outs_p, 2), stack(outs_p, 3), stack(outs_p, 4),
            stack(outs_s, 0), stack(outs_s, 1), stack(outs_s, 2), stack(outs_s, 3), stack(outs_s, 4))
```

```python
import functools
import math

import jax
import jax.numpy as jnp
from jax import lax
from jax.experimental import pallas as pl
from jax.experimental.pallas import tpu as pltpu

D_MODEL = 2048
D_SSM = 1024
SSM_GROUP = 16
N_SSM_GROUPS = 64
SSM_STATE = 64
D_ATTN = 1024
HEAD_DIM = 128
N_HEADS = 8
D_FF = 8192
EPS = 1e-6
NEG_INF = -1e30
LOG2E = math.log2(math.e)
MAX_FIXED_SHIFT = 45.0
EXP2_ZERO = -150.0

LANES = 128
SLAB = 256
N_SLABS = D_SSM // SLAB
SLAB_GROUPS = SLAB // SSM_GROUP
SLAB_STATE = SLAB_GROUPS * SSM_STATE
V7X_VMEM_BYTES = 64 * 1024 * 1024
VMEM_LIMIT = V7X_VMEM_BYTES - 2 * 1024 * 1024

F32 = jnp.float32
BF16 = jnp.bfloat16


def _params(sem, vmem=VMEM_LIMIT):
    return pltpu.CompilerParams(dimension_semantics=sem, vmem_limit_bytes=vmem)


def _rms(x, g):
    return x * lax.rsqrt(jnp.mean(x * x, axis=-1, keepdims=True) + EPS) * g


def _in_proj_kernel(x_ref, g_ref, w_ref, wf_ref, bf_ref, gq_ref, gk_ref,
                    u_ref, qb_ref, k_ref, kb_ref, v_ref, vb_ref, lft_ref):
    hb = _rms(x_ref[...], g_ref[...]).astype(BF16)

    def project(w_rows):
        return lax.dot_general(hb, w_rows, (((1,), (1,)), ((), ())), preferred_element_type=F32)

    u_ref[...] = project(w_ref[0:D_SSM, :])

    q = project(w_ref[D_SSM:D_SSM + D_ATTN, :])
    gq = gq_ref[...]
    for h in range(N_HEADS):
        sl = slice(h * HEAD_DIM, (h + 1) * HEAD_DIM)
        qb_ref[:, sl] = (_rms(q[:, sl], gq) * (HEAD_DIM ** -0.5 * LOG2E)).astype(BF16)

    k = project(w_ref[D_SSM + D_ATTN:D_SSM + 2 * D_ATTN, :])
    gk = gk_ref[...]
    for h in range(N_HEADS):
        sl = slice(h * HEAD_DIM, (h + 1) * HEAD_DIM)
        kn = _rms(k[:, sl], gk)
        k_ref[:, sl] = kn
        kb_ref[:, sl] = kn.astype(BF16)

    v = project(w_ref[D_SSM + 2 * D_ATTN:D_SSM + 3 * D_ATTN, :])
    v_ref[...] = v
    vb_ref[...] = v.astype(BF16)

    zf = project(wf_ref[...]) + bf_ref[...]
    lf = jnp.minimum(zf, 0.0) - jnp.log1p(jnp.exp(-jnp.abs(zf)))
    lft_ref[...] = jnp.transpose(lf)[:N_HEADS, :]


def _in_proj(x, g, w_main, w_f, b_f, g_q, g_k, *, tt):
    t = x.shape[0]
    assert t % tt == 0
    row = lambda i: (i, 0)
    const = lambda i: (0, 0)
    once = pl.Buffered(1)
    wide = lambda dt: jax.ShapeDtypeStruct((t, D_ATTN), dt)
    return pl.pallas_call(
        _in_proj_kernel,
        grid=(t // tt,),
        in_specs=[
            pl.BlockSpec((tt, D_MODEL), row),
            pl.BlockSpec((1, D_MODEL), const),
            pl.BlockSpec((D_SSM + 3 * D_ATTN, D_MODEL), const, pipeline_mode=once),
            pl.BlockSpec((LANES, D_MODEL), const, pipeline_mode=once),
            pl.BlockSpec((1, LANES), const),
            pl.BlockSpec((1, HEAD_DIM), const),
            pl.BlockSpec((1, HEAD_DIM), const),
        ],
        out_specs=[
            pl.BlockSpec((tt, D_SSM), row),
            pl.BlockSpec((tt, D_ATTN), row),
            pl.BlockSpec((tt, D_ATTN), row),
            pl.BlockSpec((tt, D_ATTN), row),
            pl.BlockSpec((tt, D_ATTN), row),
            pl.BlockSpec((tt, D_ATTN), row),
            pl.BlockSpec((N_HEADS, tt), lambda i: (0, i)),
        ],
        out_shape=[wide(F32), wide(BF16), wide(F32), wide(BF16), wide(F32), wide(BF16),
                   jax.ShapeDtypeStruct((N_HEADS, t), F32)],
        compiler_params=_params(("arbitrary",)),
        name="in_proj",
    )(x, g, w_main, w_f, b_f, g_q, g_k)


def _cumsum_kernel(x_ref, o_ref):
    rows, length = x_ref.shape
    upper = (lax.broadcasted_iota(jnp.int32, (LANES, LANES), 0)
             <= lax.broadcasted_iota(jnp.int32, (LANES, LANES), 1)).astype(BF16)
    carry = jnp.zeros((rows, 1), F32)
    for b in range(length // LANES):
        sl = slice(b * LANES, (b + 1) * LANES)
        x = x_ref[:, sl]
        hi = x.astype(BF16)
        rest = x - hi.astype(F32)
        mid = rest.astype(BF16)
        lo = (rest - mid.astype(F32)).astype(BF16)
        x = (jnp.dot(hi, upper, preferred_element_type=F32)
             + jnp.dot(mid, upper, preferred_element_type=F32)
             + jnp.dot(lo, upper, preferred_element_type=F32)) + carry
        o_ref[:, sl] = x * LOG2E
        carry = x[:, LANES - 1:LANES]


def _cumsum_lanes(x):
    rows, length = x.shape
    assert length % LANES == 0 and rows % 8 == 0
    return pl.pallas_call(
        _cumsum_kernel,
        out_shape=jax.ShapeDtypeStruct((rows, length), F32),
        name="cumsum",
    )(x)


def _s5_tables(a_re, a_im, log_step, b_re, b_im, c_re, c_im, d, tc):
    step = jnp.exp(log_step)[:, None]
    mag = jnp.exp(a_re * step)
    abar_re = mag * jnp.cos(a_im * step)
    abar_im = mag * jnp.sin(a_im * step)
    den = a_re * a_re + a_im * a_im
    nr = abar_re - 1.0
    ni = abar_im
    fr = (nr * a_re + ni * a_im) / den
    fi = (ni * a_re - nr * a_im) / den
    bbar_re = fr[..., None] * b_re - fi[..., None] * b_im
    bbar_im = fr[..., None] * b_im + fi[..., None] * b_re
    row_group = lambda n, per: (lax.broadcasted_iota(jnp.int32, (n, 1), 0) // per)
    col_group = lambda n, per: (lax.broadcasted_iota(jnp.int32, (1, n), 1) // per)

    def in_blockdiag(b):
        b = b.reshape(N_SLABS, SLAB_GROUPS, SSM_STATE, SSM_GROUP).transpose(0, 1, 3, 2)
        b = jnp.tile(b.reshape(N_SLABS, SLAB, SSM_STATE), (1, 1, SLAB_GROUPS))
        keep = row_group(SLAB, SSM_GROUP) == col_group(SLAB_STATE, SSM_STATE)
        return jnp.where(keep[None], b, 0.0)

    def out_blockdiag(c):
        c = c.reshape(N_SLABS, SLAB_GROUPS, SSM_GROUP, SSM_STATE).transpose(0, 1, 3, 2)
        c = jnp.tile(c.reshape(N_SLABS, SLAB_STATE, SSM_GROUP), (1, 1, SLAB_GROUPS))
        keep = row_group(SLAB_STATE, SSM_STATE) == col_group(SLAB, SSM_GROUP)
        return jnp.where(keep[None], c, 0.0)

    bb = jnp.concatenate([in_blockdiag(bbar_re), in_blockdiag(bbar_im)], axis=-1).astype(BF16)
    cc = jnp.concatenate([out_blockdiag(c_re), out_blockdiag(-c_im)], axis=1).astype(BF16)
    n = jnp.arange(1, tc + 1, dtype=F32)[:, None, None]
    pmag = jnp.exp(a_re[None] * step[None] * n)
    ang = a_im[None] * step[None] * n
    p_re = (pmag * jnp.cos(ang)).reshape(tc, N_SLABS, SLAB_STATE).transpose(1, 0, 2)
    p_im = (pmag * jnp.sin(ang)).reshape(tc, N_SLABS, SLAB_STATE).transpose(1, 0, 2)
    pw = jnp.concatenate([p_re, p_im], axis=-1)
    dd = d.reshape(N_SLABS, 1, SLAB)
    return bb, cc, pw, dd


def _s5_kernel(u_ref, bb_ref, cc_ref, pw_ref, d_ref, h0_ref, y_ref, hfin_ref,
               us_s, st_s, hb_s, g_s, carry_s, *, tc, nc, tiles_per_seq, carry_mode):
    i = pl.program_id(1)
    pitch = tc + 8
    for c in range(nc):
        for half in range(SLAB // LANES):
            us_s[half, c * pitch:c * pitch + tc, :] = u_ref[c * tc:(c + 1) * tc,
                                                            half * LANES:(half + 1) * LANES]
    u = jnp.concatenate(
        [jnp.concatenate([us_s[half, pl.ds(t, nc, stride=pitch), :]
                          for half in range(SLAB // LANES)], axis=1) for t in range(tc)], axis=0)
    uh = u.astype(BF16)
    y_ref[...] = d_ref[...] * u
    blk = 2 * LANES
    nblk = SLAB_STATE // blk
    grp = max(1, 16 // nc)
    lanes = lambda b: slice(b * blk, (b + 1) * blk)
    step_rows = lambda t: slice(t * nc, (t + 1) * nc)

    if carry_mode:
        @pl.when(i % tiles_per_seq == 0)
        def _():
            carry_s[...] = jnp.zeros_like(carry_s)
    else:
        g_s[...] = h0_ref[...]

    for b in range(nblk):
        re, im = lanes(b), lanes(nblk + b)
        st_s[:, re] = jnp.dot(uh, bb_ref[:, re], preferred_element_type=F32)
        st_s[:, im] = jnp.dot(uh, bb_ref[:, im], preferred_element_type=F32)
        ar = pw_ref[0:1, re]
        ai = pw_ref[0:1, im]
        hr = jnp.zeros((nc, blk), F32)
        hi = jnp.zeros((nc, blk), F32)
        for t in range(tc):
            rows = step_rows(t)
            hr, hi = (ar * hr - ai * hi + st_s[rows, re],
                      ar * hi + ai * hr + st_s[rows, im])
            st_s[rows, re] = hr
            st_s[rows, im] = hi

        ar = pw_ref[tc - 1:tc, re]
        ai = pw_ref[tc - 1:tc, im]
        if carry_mode:
            gr = carry_s[0:1, re]
            gi = carry_s[0:1, im]
            for c in range(nc):
                g_s[c:c + 1, re] = gr
                g_s[c:c + 1, im] = gi
                gr, gi = (ar * gr - ai * gi + hr[c:c + 1, :],
                          ar * gi + ai * gr + hi[c:c + 1, :])
            carry_s[0:1, re] = gr
            carry_s[0:1, im] = gi
            hfin_ref[0:1, re] = gr
            hfin_ref[0:1, im] = gi
            gr = g_s[:, re]
            gi = g_s[:, im]
        else:
            gr = g_s[:, re]
            gi = g_s[:, im]
            hfin_ref[:, re] = ar * gr - ai * gi + hr
            hfin_ref[:, im] = ar * gi + ai * gr + hi

        for t0 in range(0, tc, grp):
            full_r, full_i = [], []
            for t in range(t0, t0 + grp):
                rows = step_rows(t)
                pr = pw_ref[t:t + 1, re]
                pi = pw_ref[t:t + 1, im]
                full_r.append(st_s[rows, re] + (pr * gr - pi * gi))
                full_i.append(st_s[rows, im] + (pr * gi + pi * gr))
            rows = slice(t0 * nc, (t0 + grp) * nc)
            hb_s[rows, re] = jnp.concatenate(full_r, axis=0).astype(BF16)
            hb_s[rows, im] = jnp.concatenate(full_i, axis=0).astype(BF16)

        y_ref[...] += (jnp.dot(hb_s[:, re], cc_ref[re, :], preferred_element_type=F32)
                       + jnp.dot(hb_s[:, im], cc_ref[im, :], preferred_element_type=F32))


def _s5(u, tables, h0, *, tt, tc, seq_len, carry_mode):
    bb, cc, pw, dd = tables
    t = u.shape[0]
    nc = tt // tc
    n_tiles = t // tt
    assert t % tt == 0 and tt % tc == 0
    if carry_mode:
        assert seq_len % tt == 0
        tiles_per_seq = seq_len // tt
        n_seq = t // seq_len
        hfin_shape = (N_SLABS, n_seq, 1, 2 * SLAB_STATE)
        hfin_spec = pl.BlockSpec((None, None, 1, 2 * SLAB_STATE),
                                 lambda s, i: (s, i // tiles_per_seq, 0, 0))
        h0 = jnp.zeros((N_SLABS, 8, 2 * SLAB_STATE), F32)
        h0_spec = pl.BlockSpec((None, 8, 2 * SLAB_STATE), lambda s, i: (s, 0, 0))
    else:
        assert seq_len == tc
        tiles_per_seq = 1
        hfin_shape = (N_SLABS, n_tiles, nc, 2 * SLAB_STATE)
        hfin_spec = pl.BlockSpec((None, None, nc, 2 * SLAB_STATE), lambda s, i: (s, i, 0, 0))
        h0_spec = pl.BlockSpec((None, nc, 2 * SLAB_STATE), lambda s, i: (s, i, 0))
    kern = functools.partial(_s5_kernel, tc=tc, nc=nc, tiles_per_seq=tiles_per_seq,
                             carry_mode=carry_mode)
    y, hfin = pl.pallas_call(
        kern,
        grid=(N_SLABS, n_tiles),
        in_specs=[
            pl.BlockSpec((tt, SLAB), lambda s, i: (i, s)),
            pl.BlockSpec((None, SLAB, 2 * SLAB_STATE), lambda s, i: (s, 0, 0)),
            pl.BlockSpec((None, 2 * SLAB_STATE, SLAB), lambda s, i: (s, 0, 0)),
            pl.BlockSpec((None, tc, 2 * SLAB_STATE), lambda s, i: (s, 0, 0)),
            pl.BlockSpec((None, 1, SLAB), lambda s, i: (s, 0, 0)),
            h0_spec,
        ],
        out_specs=[pl.BlockSpec((tt, SLAB), lambda s, i: (i, s)), hfin_spec],
        out_shape=[jax.ShapeDtypeStruct((t, D_SSM), F32),
                   jax.ShapeDtypeStruct(hfin_shape, F32)],
        scratch_shapes=[pltpu.VMEM((SLAB // LANES, nc * (tc + 8), LANES), F32),
                        pltpu.VMEM((tt, 2 * SLAB_STATE), F32),
                        pltpu.VMEM((tt, 2 * SLAB_STATE), BF16),
                        pltpu.VMEM((nc, 2 * SLAB_STATE), F32),
                        pltpu.VMEM((8, 2 * SLAB_STATE), F32)],
        compiler_params=_params(("arbitrary", "arbitrary")),
        name="s5",
    )(u, bb, cc, pw, dd, h0)
    return y, hfin.reshape(N_SLABS, -1, 2 * SLAB_STATE)


def _state_to_gp(hfin):
    n = hfin.shape[1]
    h = hfin.transpose(1, 0, 2)
    re = h[..., :SLAB_STATE].reshape(n, N_SSM_GROUPS, SSM_STATE)
    im = h[..., SLAB_STATE:].reshape(n, N_SSM_GROUPS, SSM_STATE)
    return re, im


def _state_from_gp(re, im):
    n = re.shape[0]
    h = jnp.concatenate([re.reshape(n, N_SLABS, SLAB_STATE), im.reshape(n, N_SLABS, SLAB_STATE)],
                        axis=-1)
    return h.transpose(1, 0, 2)


def _col_from_row(row):
    n = row.shape[1]
    r = lax.broadcasted_iota(jnp.int32, (n, n), 0)
    c = lax.broadcasted_iota(jnp.int32, (n, n), 1)
    return jnp.sum(jnp.where(r == c, jnp.broadcast_to(row, (n, n)), 0.0), axis=1, keepdims=True)


def _attn_prompt_kernel(first_ref, q_ref, k_ref, v_ref, cq_ref, ck_ref, o_ref,
                        m_s, l_s, acc_s, cq_s, s_s, *, bq, online):
    i = pl.program_id(1)
    bk = bq // 2
    cq_s[...] = jnp.transpose(jnp.broadcast_to(cq_ref[...], (LANES, bq)))
    if online:
        m_s[...] = jnp.full_like(m_s, NEG_INF)
    l_s[...] = jnp.zeros_like(l_s)
    acc_s[...] = jnp.zeros_like(acc_s)
    nchunk = bk // LANES
    all_rows = slice(0, bq)

    def scores(j, slot, rows=all_rows):
        start = pl.multiple_of(j * bk, bk)
        s_s[slot, rows, :] = lax.dot_general(q_ref[rows, :], k_ref[pl.ds(start, bk), :],
                                             (((1,), (1,)), ((), ())), preferred_element_type=F32)

    def reduce_block(j, slot, rows=all_rows, causal=False):
        start = pl.multiple_of(j * bk, bk)
        v = v_ref[pl.ds(start, bk), :]
        nrows = rows.stop - rows.start

        def logits(c):
            col0 = pl.multiple_of(start + c * LANES, LANES)
            t = s_s[slot, rows, c * LANES:(c + 1) * LANES] - ck_ref[:, pl.ds(col0, LANES)]
            if causal:
                r = lax.broadcasted_iota(jnp.int32, (nrows, LANES), 0)
                col = lax.broadcasted_iota(jnp.int32, (nrows, LANES), 1) + c * LANES
                t = jnp.where(col <= r, t, NEG_INF)
            return t

        if online:
            m_loc = functools.reduce(jnp.maximum, [logits(c) for c in range(nchunk)])
            m_old = m_s[rows, :]
            m_new = jnp.maximum(m_old, jnp.max(m_loc, axis=1, keepdims=True) + cq_s[rows, :])
            m_s[rows, :] = m_new
            alpha = jnp.exp2(m_old - m_new)
            shift = cq_s[rows, :] - m_new
        else:
            shift = cq_s[rows, :]
        ps = [jnp.exp2(logits(c) + shift) for c in range(nchunk)]
        p = jnp.concatenate([pc.astype(BF16) for pc in ps], axis=1)
        l_new = functools.reduce(jnp.add, ps)
        acc_new = jnp.dot(p, v, preferred_element_type=F32)
        if online:
            l_s[rows, :] = alpha * l_s[rows, :] + l_new
            acc_s[rows, :] = alpha * acc_s[rows, :] + acc_new
        else:
            l_s[rows, :] += l_new
            acc_s[rows, :] += acc_new

    first = first_ref[pl.program_id(0) * pl.num_programs(1) + i]
    scores(2 * first, 0)

    def body(p, carry):
        j = 2 * p
        scores(j + 1, 1)
        reduce_block(j, 0)
        scores(j + 2, 0)
        reduce_block(j + 1, 1)
        return carry

    lax.fori_loop(first, i, body, 0)
    late_rows = slice(bk, bq)
    scores(2 * i + 1, 1, late_rows)
    reduce_block(2 * i, 0, causal=True)
    reduce_block(2 * i + 1, 1, late_rows, causal=True)

    o_ref[...] = acc_s[...] / jnp.sum(l_s[...], axis=1, keepdims=True)


def _attn_prompt(qb, kb, vb, first, cq_rows, ck_rows, *, n_seq, seq_len, bq, online):
    nq = seq_len // bq
    assert seq_len % bq == 0
    kern = functools.partial(_attn_prompt_kernel, bq=bq, online=online)
    q_map = lambda g, i, first: ((g // N_HEADS) * nq + i, g % N_HEADS)
    kv_map = lambda g, i, first: (g // N_HEADS, g % N_HEADS)
    grid_spec = pltpu.PrefetchScalarGridSpec(
        num_scalar_prefetch=1,
        grid=(n_seq * N_HEADS, nq),
        in_specs=[
            pl.BlockSpec((bq, HEAD_DIM), q_map),
            pl.BlockSpec((seq_len, HEAD_DIM), kv_map),
            pl.BlockSpec((seq_len, HEAD_DIM), kv_map),
            pl.BlockSpec((None, 1, bq), lambda g, i, first: (g, 0, i)),
            pl.BlockSpec((None, 1, seq_len), lambda g, i, first: (g, 0, 0)),
        ],
        out_specs=pl.BlockSpec((bq, HEAD_DIM), q_map),
        scratch_shapes=[pltpu.VMEM((bq, LANES), F32), pltpu.VMEM((bq, LANES), F32),
                        pltpu.VMEM((bq, HEAD_DIM), F32), pltpu.VMEM((bq, LANES), F32),
                        pltpu.VMEM((2, bq, bq // 2), F32)],
    )
    return pl.pallas_call(
        kern,
        grid_spec=grid_spec,
        out_shape=jax.ShapeDtypeStruct((n_seq * seq_len, D_ATTN), F32),
        compiler_params=_params(("arbitrary", "arbitrary")),
        name="attn_prompt",
    )(first, qb, kb, vb, cq_rows, ck_rows)


def _first_live_pair(c_rows, bq):
    g, _, length = c_rows.shape
    nq = length // bq
    blocks = c_rows.reshape(g, nq, bq)
    hi = jnp.max(blocks, axis=-1)
    lo = jnp.min(blocks, axis=-1)
    dead = (hi[:, :, None] - lo[:, None, :]) < EXP2_ZERO - 1.0
    pair = jnp.arange(nq, dtype=jnp.int32)
    dead = dead & (pair[None, None, :] < pair[None, :, None])
    first = jnp.min(jnp.where(dead, nq, pair[None, None, :]), axis=-1)
    return first.reshape(g * nq).astype(jnp.int32)


def _attn_sample_kernel(q_ref, kn_ref, vn_ref, ck_ref, cv_ref, c_ref, o_ref, *, past, s_new):
    r = lax.broadcasted_iota(jnp.int32, (s_new, s_new), 0)
    col = lax.broadcasted_iota(jnp.int32, (s_new, s_new), 1)
    causal = col <= r
    nt = (((1,), (1,)), ((), ()))
    heads = range(N_HEADS)
    lanes = lambda h: slice(h * HEAD_DIM, (h + 1) * HEAD_DIM)
    head_rows = lambda h: pl.ds(h, past, stride=N_HEADS)
    logits = []
    for h in heads:
        q = q_ref[:, lanes(h)]
        c_past = c_ref[h:h + 1, 0:past]
        c_new = c_ref[h:h + 1, past:past + s_new]
        cq = _col_from_row(c_new)
        s_p = lax.dot_general(q, ck_ref[head_rows(h), :].astype(BF16), nt,
                              preferred_element_type=F32)
        s_n = lax.dot_general(q, kn_ref[:, lanes(h)], nt, preferred_element_type=F32)
        logits.append((s_p + cq - c_past, jnp.where(causal, s_n + cq - c_new, NEG_INF)))
    probs = []
    for s_p, s_n in logits:
        m = jnp.maximum(jnp.max(s_p, axis=1, keepdims=True), jnp.max(s_n, axis=1, keepdims=True))
        p_p = jnp.exp2(s_p - m)
        p_n = jnp.exp2(s_n - m)
        l = jnp.sum(p_p, axis=1, keepdims=True) + jnp.sum(p_n, axis=1, keepdims=True)
        probs.append((p_p.astype(BF16), p_n.astype(BF16), l))
    for h, (p_p, p_n, l) in zip(heads, probs):
        acc = (jnp.dot(p_p, cv_ref[head_rows(h), :].astype(BF16), preferred_element_type=F32)
               + jnp.dot(p_n, vn_ref[:, lanes(h)], preferred_element_type=F32))
        o_ref[:, lanes(h)] = acc / l


def _attn_sample(qb, kb, vb, cache_k, cache_v, c_all, *, n_seq, s_new, past):
    kern = functools.partial(_attn_sample_kernel, past=past, s_new=s_new)
    new = pl.BlockSpec((s_new, D_ATTN), lambda b: (b, 0))
    cache_k = cache_k.reshape(n_seq, past * N_HEADS, HEAD_DIM)
    cache_v = cache_v.reshape(n_seq, past * N_HEADS, HEAD_DIM)
    cache = pl.BlockSpec((None, past * N_HEADS, HEAD_DIM), lambda b: (b, 0, 0))
    return pl.pallas_call(
        kern,
        grid=(n_seq,),
        in_specs=[new, new, new, cache, cache,
                  pl.BlockSpec((N_HEADS, c_all.shape[1]), lambda b: (b, 0))],
        out_specs=new,
        out_shape=jax.ShapeDtypeStruct((n_seq * s_new, D_ATTN), F32),
        compiler_params=_params(("arbitrary",)),
        name="attn_sample",
    )(qb, kb, vb, cache_k, cache_v, c_all)


def _gelu_tanh(x):
    return 0.5 * x * (1.0 + jnp.tanh(math.sqrt(2.0 / math.pi) * (x + 0.044715 * (x * x * x))))


def _mix_kernel(x_ref, y_ref, a_ref, wglu_ref, gs_ref, ga_ref, wout_ref, gm_ref,
                x1_ref, hm_ref, park_s, *, tc, nc):
    gy = _gelu_tanh(y_ref[...])
    gate = jax.nn.sigmoid(jnp.dot(gy.astype(BF16), wglu_ref[...], preferred_element_type=F32))
    ssm = _rms(gy * gate, gs_ref[...])
    pitch = nc + 8
    planes = D_SSM // LANES
    for t in range(tc):
        for p in range(planes):
            park_s[p, t * pitch:t * pitch + nc, :] = ssm[t * nc:(t + 1) * nc,
                                                         p * LANES:(p + 1) * LANES]
    ssm = jnp.concatenate(
        [jnp.concatenate([park_s[p, pl.ds(c, tc, stride=pitch), :] for p in range(planes)], axis=1)
         for c in range(nc)], axis=0).astype(BF16)
    att = _rms(a_ref[...], ga_ref[...]).astype(BF16)
    x1 = (x_ref[...]
          + jnp.dot(ssm, wout_ref[0:D_SSM, :], preferred_element_type=F32)
          + jnp.dot(att, wout_ref[D_SSM:D_SSM + D_ATTN, :], preferred_element_type=F32))
    x1_ref[...] = x1
    hm_ref[...] = _rms(x1, gm_ref[...]).astype(BF16)


def _mix(x, y, attn, w_glu, g_ssm, g_attn, w_out, g_mlp, *, tt, tc):
    t = x.shape[0]
    nc = tt // tc
    assert t % tt == 0 and tt % tc == 0 and nc % 8 == 0
    row = lambda i: (i, 0)
    const = lambda i: (0, 0)
    once = pl.Buffered(1)
    return pl.pallas_call(
        functools.partial(_mix_kernel, tc=tc, nc=nc),
        grid=(t // tt,),
        in_specs=[
            pl.BlockSpec((tt, D_MODEL), row),
            pl.BlockSpec((tt, D_SSM), row),
            pl.BlockSpec((tt, D_ATTN), row),
            pl.BlockSpec((D_SSM, D_SSM), const, pipeline_mode=once),
            pl.BlockSpec((1, D_SSM), const),
            pl.BlockSpec((1, D_ATTN), const),
            pl.BlockSpec((D_SSM + D_ATTN, D_MODEL), const, pipeline_mode=once),
            pl.BlockSpec((1, D_MODEL), const),
        ],
        out_specs=[pl.BlockSpec((tt, D_MODEL), row), pl.BlockSpec((tt, D_MODEL), row)],
        out_shape=[jax.ShapeDtypeStruct((t, D_MODEL), F32), jax.ShapeDtypeStruct((t, D_MODEL), BF16)],
        scratch_shapes=[pltpu.VMEM((D_SSM // LANES, tc * (nc + 8), LANES), F32)],
        compiler_params=_params(("arbitrary",)),
        name="mix",
    )(x, y, attn, w_glu, g_ssm, g_attn, w_out, g_mlp)


def _mlp_kernel(x1_ref, hm_ref, wup_ref, wdn_ref, o_ref):
    j = pl.program_id(1)

    @pl.when(j == 0)
    def _():
        o_ref[...] = x1_ref[...]

    a = jnp.maximum(jnp.dot(hm_ref[...], wup_ref[...], preferred_element_type=F32), 0.0)
    o_ref[...] += jnp.dot((a * a).astype(BF16), wdn_ref[...], preferred_element_type=F32)


def _mlp(x1, hm, w_up, w_down, *, tt, tf):
    t = x1.shape[0]
    assert t % tt == 0 and D_FF % tf == 0
    return pl.pallas_call(
        _mlp_kernel,
        grid=(t // tt, D_FF // tf),
        in_specs=[
            pl.BlockSpec((tt, D_MODEL), lambda i, j: (i, 0)),
            pl.BlockSpec((tt, D_MODEL), lambda i, j: (i, 0)),
            pl.BlockSpec((D_MODEL, tf), lambda i, j: (0, j)),
            pl.BlockSpec((tf, D_MODEL), lambda i, j: (j, 0)),
        ],
        out_specs=pl.BlockSpec((tt, D_MODEL), lambda i, j: (i, 0)),
        out_shape=jax.ShapeDtypeStruct((t, D_MODEL), F32),
        compiler_params=_params(("arbitrary", "arbitrary")),
        name="mlp",
    )(x1, hm, w_up, w_down)


def _layer(x, weights, *, n_seq, seq_len, cache=None, h0=None, tiles):
    (g_mix, w_main, w_f, b_f, g_q, g_k, s5_raw, w_glu, g_ssm, g_attn, w_out, g_mlp,
     w_up, w_down) = weights
    t = n_seq * seq_len
    x2 = x.reshape(t, D_MODEL)
    tc = tiles["chunk"] if cache is None else seq_len
    u, qb, k, kb, v, vb, lf_t = _in_proj(x2, g_mix, w_main, w_f, b_f, g_q, g_k, tt=tiles["s5"])
    lf_t = lf_t.reshape(N_HEADS, n_seq, seq_len)
    logf = lf_t.transpose(1, 2, 0)
    lf_rows = lf_t.transpose(1, 0, 2).reshape(n_seq * N_HEADS, seq_len)

    if cache is None:
        tables = _s5_tables(*s5_raw, tc=tc)
        y, hfin = _s5(u, tables, None, tt=tiles["s5"], tc=tc, seq_len=seq_len,
                      carry_mode=True)
        c_rows = _cumsum_lanes(lf_rows).reshape(n_seq * N_HEADS, 1, seq_len)
        attend = functools.partial(_attn_prompt, qb, kb, vb, n_seq=n_seq, seq_len=seq_len,
                                   bq=tiles["bq"])
        bound = (HEAD_DIM ** -0.5 * LOG2E) * HEAD_DIM * jnp.max(jnp.abs(g_q)) * jnp.max(jnp.abs(g_k))
        attn = lax.cond(
            bound <= MAX_FIXED_SHIFT,
            lambda: attend(_first_live_pair(c_rows, tiles["bq"]), c_rows - bound, c_rows,
                           online=False),
            lambda: attend(jnp.zeros((c_rows.shape[0] * (seq_len // tiles["bq"]),), jnp.int32),
                           c_rows, c_rows, online=True))
    else:
        cache_k, cache_v, cache_logf = cache
        past = cache_k.shape[1]
        tables = _s5_tables(*s5_raw, tc=tc)
        y, hfin = _s5(u, tables, h0, tt=tiles["s5"], tc=tc, seq_len=seq_len,
                      carry_mode=False)
        past_rows = cache_logf.transpose(0, 2, 1).reshape(n_seq * N_HEADS, past)
        total = past + seq_len
        padded = -(-total // LANES) * LANES
        lf_all = jnp.concatenate(
            [past_rows, lf_rows, jnp.zeros((n_seq * N_HEADS, padded - total), F32)], axis=1)
        c_all = _cumsum_lanes(lf_all)
        attn = _attn_sample(qb, kb, vb, cache_k, cache_v, c_all,
                            n_seq=n_seq, s_new=seq_len, past=past)

    x1, hm = _mix(x2, y, attn, w_glu, g_ssm, g_attn, w_out, g_mlp, tt=tiles["s5"], tc=tc)
    out = _mlp(x1, hm, w_up, w_down, tt=tiles["mlp"], tf=tiles["tf"])
    h_re, h_im = _state_to_gp(hfin)
    return (out.reshape(n_seq, seq_len, D_MODEL),
            k.reshape(n_seq, seq_len, N_HEADS, HEAD_DIM),
            v.reshape(n_seq, seq_len, N_HEADS, HEAD_DIM),
            logf, h_re, h_im)


def _cast_kernel(w_ref, wf_ref, o_ref, of_ref):
    o_ref[...] = w_ref[...].astype(o_ref.dtype)
    of_ref[...] = jnp.zeros_like(of_ref)
    of_ref[0:N_HEADS, :] = wf_ref[...].astype(of_ref.dtype)


def _cast_w_in(w_t, n_main, rows_per_step=512):
    n_rows, d = w_t.shape
    assert n_main % rows_per_step == 0 and n_rows - n_main == N_HEADS and n_main % N_HEADS == 0
    return pl.pallas_call(
        _cast_kernel,
        grid=(n_main // rows_per_step,),
        in_specs=[pl.BlockSpec((rows_per_step, d), lambda i: (i, 0)),
                  pl.BlockSpec((N_HEADS, d), lambda i: (n_main // N_HEADS, 0))],
        out_specs=[pl.BlockSpec((rows_per_step, d), lambda i: (i, 0)),
                   pl.BlockSpec((LANES, d), lambda i: (0, 0))],
        out_shape=[jax.ShapeDtypeStruct((n_main, d), BF16),
                   jax.ShapeDtypeStruct((LANES, d), BF16)],
        compiler_params=_params(("arbitrary",)),
        name="cast_w_in",
    )(w_t, w_t)


def _prep_weights(l, g_norm_mix, w_in, b_f, ssm_a_re, ssm_a_im, ssm_log_step, ssm_b_re, ssm_b_im,
                  ssm_c_re, ssm_c_im, ssm_d, w_glu, g_q, g_k, g_out_ssm, g_out_attn, w_out,
                  g_norm_mlp, w_up, w_down):
    n_main = D_SSM + 3 * D_ATTN
    w_main, w_f = _cast_w_in(jnp.swapaxes(w_in[l], 0, 1), n_main)
    b = jnp.pad(b_f[l], (0, LANES - N_HEADS)).reshape(1, LANES)
    s5_raw = (ssm_a_re[l], ssm_a_im[l], ssm_log_step[l], ssm_b_re[l], ssm_b_im[l],
              ssm_c_re[l], ssm_c_im[l], ssm_d[l])
    return (g_norm_mix[l].reshape(1, D_MODEL), w_main, w_f, b,
            g_q[l].reshape(1, HEAD_DIM), g_k[l].reshape(1, HEAD_DIM), s5_raw,
            w_glu[l].astype(BF16), g_out_ssm[l].reshape(1, D_SSM), g_out_attn[l].reshape(1, D_ATTN),
            w_out[l].astype(BF16), g_norm_mlp[l].reshape(1, D_MODEL),
            w_up[l].astype(BF16), w_down[l].astype(BF16))


PROMPT_TILES = dict(s5=512, chunk=32, bq=1024, mlp=512, tf=2048)
SAMPLE_TILES = dict(s5=512, mlp=512, tf=2048)


def kernel(x_prompt, x_sample, cache_k, cache_v, cache_logf, state_ssm_re, state_ssm_im,
           g_norm_mix, w_in, b_f, ssm_a_re, ssm_a_im, ssm_log_step, ssm_b_re, ssm_b_im,
           ssm_c_re, ssm_c_im, ssm_d, w_glu, g_q, g_k, g_out_ssm, g_out_attn, w_out,
           g_norm_mlp, w_up, w_down):
    depth = w_in.shape[0]
    y_p, y_s = x_prompt, x_sample
    outs_p, outs_s = [], []
    for l in range(depth):
        weights = _prep_weights(l, g_norm_mix, w_in, b_f, ssm_a_re, ssm_a_im, ssm_log_step,
                                ssm_b_re, ssm_b_im, ssm_c_re, ssm_c_im, ssm_d, w_glu, g_q, g_k,
                                g_out_ssm, g_out_attn, w_out, g_norm_mlp, w_up, w_down)
        n_p, l_p = y_p.shape[0], y_p.shape[1]
        y_p, *rest_p = _layer(y_p, weights, n_seq=n_p, seq_len=l_p, tiles=PROMPT_TILES)
        n_s, l_s = y_s.shape[0], y_s.shape[1]
        h0 = _state_from_gp(state_ssm_re[l], state_ssm_im[l])
        y_s, *rest_s = _layer(y_s, weights, n_seq=n_s, seq_len=l_s,
                              cache=(cache_k[l], cache_v[l], cache_logf[l]), h0=h0,
                              tiles=SAMPLE_TILES)
        outs_p.append(rest_p)
        outs_s.append(rest_s)
    stack = lambda outs, idx: jnp.stack([o[idx] for o in outs])
    return (y_p, y_s,
            stack(outs_p, 0), stack(outs_p, 1), stack(outs_p, 2), stack(outs_p, 3), stack(outs_p, 4),
            stack(outs_s, 0), stack(outs_s, 1), stack(outs_s, 2), stack(outs_s, 3), stack(outs_s, 4))
```

```python
import functools
import math

import jax
import jax.numpy as jnp
from jax import lax
from jax.experimental import pallas as pl
from jax.experimental.pallas import tpu as pltpu

D_MODEL = 2048
D_SSM = 1024
SSM_GROUP = 16
N_SSM_GROUPS = 64
SSM_STATE = 64
D_ATTN = 1024
HEAD_DIM = 128
N_HEADS = 8
D_FF = 8192
EPS = 1e-6
NEG_INF = -1e30
LOG2E = math.log2(math.e)
MAX_FIXED_SHIFT = 45.0
EXP2_ZERO = -150.0

LANES = 128
SLAB = 256
N_SLABS = D_SSM // SLAB
SLAB_GROUPS = SLAB // SSM_GROUP
SLAB_STATE = SLAB_GROUPS * SSM_STATE
V7X_VMEM_BYTES = 64 * 1024 * 1024
VMEM_LIMIT = V7X_VMEM_BYTES - 2 * 1024 * 1024

F32 = jnp.float32
BF16 = jnp.bfloat16


def _params(sem, vmem=VMEM_LIMIT):
    return pltpu.CompilerParams(dimension_semantics=sem, vmem_limit_bytes=vmem)


def _rms(x, g):
    return x * lax.rsqrt(jnp.mean(x * x, axis=-1, keepdims=True) + EPS) * g


def _in_proj_kernel(x_ref, g_ref, w_ref, wf_ref, bf_ref, gq_ref, gk_ref,
                    u_ref, qb_ref, k_ref, kb_ref, v_ref, vb_ref, lft_ref):
    hb = _rms(x_ref[...], g_ref[...]).astype(BF16)

    def project(w_rows):
        return lax.dot_general(hb, w_rows, (((1,), (1,)), ((), ())), preferred_element_type=F32)

    u_ref[...] = project(w_ref[0:D_SSM, :])

    q = project(w_ref[D_SSM:D_SSM + D_ATTN, :])
    gq = gq_ref[...]
    for h in range(N_HEADS):
        sl = slice(h * HEAD_DIM, (h + 1) * HEAD_DIM)
        qb_ref[:, sl] = (_rms(q[:, sl], gq) * (HEAD_DIM ** -0.5 * LOG2E)).astype(BF16)

    k = project(w_ref[D_SSM + D_ATTN:D_SSM + 2 * D_ATTN, :])
    gk = gk_ref[...]
    for h in range(N_HEADS):
        sl = slice(h * HEAD_DIM, (h + 1) * HEAD_DIM)
        kn = _rms(k[:, sl], gk)
        k_ref[:, sl] = kn
        kb_ref[:, sl] = kn.astype(BF16)

    v = project(w_ref[D_SSM + 2 * D_ATTN:D_SSM + 3 * D_ATTN, :])
    v_ref[...] = v
    vb_ref[...] = v.astype(BF16)

    zf = project(wf_ref[...]) + bf_ref[...]
    lf = jnp.minimum(zf, 0.0) - jnp.log1p(jnp.exp(-jnp.abs(zf)))
    lft_ref[...] = jnp.transpose(lf)[:N_HEADS, :]


def _in_proj(x, g, w_main, w_f, b_f, g_q, g_k, *, tt):
    t = x.shape[0]
    assert t % tt == 0
    row = lambda i: (i, 0)
    const = lambda i: (0, 0)
    once = pl.Buffered(1)
    wide = lambda dt: jax.ShapeDtypeStruct((t, D_ATTN), dt)
    return pl.pallas_call(
        _in_proj_kernel,
        grid=(t // tt,),
        in_specs=[
            pl.BlockSpec((tt, D_MODEL), row),
            pl.BlockSpec((1, D_MODEL), const),
            pl.BlockSpec((D_SSM + 3 * D_ATTN, D_MODEL), const, pipeline_mode=once),
            pl.BlockSpec((LANES, D_MODEL), const, pipeline_mode=once),
            pl.BlockSpec((1, LANES), const),
            pl.BlockSpec((1, HEAD_DIM), const),
            pl.BlockSpec((1, HEAD_DIM), const),
        ],
        out_specs=[
            pl.BlockSpec((tt, D_SSM), row),
            pl.BlockSpec((tt, D_ATTN), row),
            pl.BlockSpec((tt, D_ATTN), row),
            pl.BlockSpec((tt, D_ATTN), row),
            pl.BlockSpec((tt, D_ATTN), row),
            pl.BlockSpec((tt, D_ATTN), row),
            pl.BlockSpec((N_HEADS, tt), lambda i: (0, i)),
        ],
        out_shape=[wide(F32), wide(BF16), wide(F32), wide(BF16), wide(F32), wide(BF16),
                   jax.ShapeDtypeStruct((N_HEADS, t), F32)],
        compiler_params=_params(("arbitrary",)),
        name="in_proj",
    )(x, g, w_main, w_f, b_f, g_q, g_k)


def _cumsum_kernel(x_ref, o_ref):
    rows, length = x_ref.shape
    upper = (lax.broadcasted_iota(jnp.int32, (LANES, LANES), 0)
             <= lax.broadcasted_iota(jnp.int32, (LANES, LANES), 1)).astype(BF16)
    carry = jnp.zeros((rows, 1), F32)
    for b in range(length // LANES):
        sl = slice(b * LANES, (b + 1) * LANES)
        x = x_ref[:, sl]
        hi = x.astype(BF16)
        rest = x - hi.astype(F32)
        mid = rest.astype(BF16)
        lo = (rest - mid.astype(F32)).astype(BF16)
        x = (jnp.dot(hi, upper, preferred_element_type=F32)
             + jnp.dot(mid, upper, preferred_element_type=F32)
             + jnp.dot(lo, upper, preferred_element_type=F32)) + carry
        o_ref[:, sl] = x * LOG2E
        carry = x[:, LANES - 1:LANES]


def _cumsum_lanes(x):
    rows, length = x.shape
    assert length % LANES == 0 and rows % 8 == 0
    return pl.pallas_call(
        _cumsum_kernel,
        out_shape=jax.ShapeDtypeStruct((rows, length), F32),
        name="cumsum",
    )(x)


def _s5_tables(a_re, a_im, log_step, b_re, b_im, c_re, c_im, d, tc):
    step = jnp.exp(log_step)[:, None]
    mag = jnp.exp(a_re * step)
    abar_re = mag * jnp.cos(a_im * step)
    abar_im = mag * jnp.sin(a_im * step)
    den = a_re * a_re + a_im * a_im
    nr = abar_re - 1.0
    ni = abar_im
    fr = (nr * a_re + ni * a_im) / den
    fi = (ni * a_re - nr * a_im) / den
    bbar_re = fr[..., None] * b_re - fi[..., None] * b_im
    bbar_im = fr[..., None] * b_im + fi[..., None] * b_re
    row_group = lambda n, per: (lax.broadcasted_iota(jnp.int32, (n, 1), 0) // per)
    col_group = lambda n, per: (lax.broadcasted_iota(jnp.int32, (1, n), 1) // per)

    def in_blockdiag(b):
        b = b.reshape(N_SLABS, SLAB_GROUPS, SSM_STATE, SSM_GROUP).transpose(0, 1, 3, 2)
        b = jnp.tile(b.reshape(N_SLABS, SLAB, SSM_STATE), (1, 1, SLAB_GROUPS))
        keep = row_group(SLAB, SSM_GROUP) == col_group(SLAB_STATE, SSM_STATE)
        return jnp.where(keep[None], b, 0.0)

    def out_blockdiag(c):
        c = c.reshape(N_SLABS, SLAB_GROUPS, SSM_GROUP, SSM_STATE).transpose(0, 1, 3, 2)
        c = jnp.tile(c.reshape(N_SLABS, SLAB_STATE, SSM_GROUP), (1, 1, SLAB_GROUPS))
        keep = row_group(SLAB_STATE, SSM_STATE) == col_group(SLAB, SSM_GROUP)
        return jnp.where(keep[None], c, 0.0)

    bb = jnp.concatenate([in_blockdiag(bbar_re), in_blockdiag(bbar_im)], axis=-1).astype(BF16)
    cc = jnp.concatenate([out_blockdiag(c_re), out_blockdiag(-c_im)], axis=1).astype(BF16)
    n = jnp.arange(1, tc + 1, dtype=F32)[:, None, None]
    pmag = jnp.exp(a_re[None] * step[None] * n)
    ang = a_im[None] * step[None] * n
    p_re = (pmag * jnp.cos(ang)).reshape(tc, N_SLABS, SLAB_STATE).transpose(1, 0, 2)
    p_im = (pmag * jnp.sin(ang)).reshape(tc, N_SLABS, SLAB_STATE).transpose(1, 0, 2)
    pw = jnp.concatenate([p_re, p_im], axis=-1)
    dd = d.reshape(N_SLABS, 1, SLAB)
    return bb, cc, pw, dd


def _s5_kernel(u_ref, bb_ref, cc_ref, pw_ref, d_ref, h0_ref, y_ref, hfin_ref,
               us_s, st_s, hb_s, g_s, carry_s, *, tc, nc, tiles_per_seq, carry_mode):
    i = pl.program_id(1)
    pitch = tc + 8
    for c in range(nc):
        for half in range(SLAB // LANES):
            us_s[half, c * pitch:c * pitch + tc, :] = u_ref[c * tc:(c + 1) * tc,
                                                            half * LANES:(half + 1) * LANES]
    u = jnp.concatenate(
        [jnp.concatenate([us_s[half, pl.ds(t, nc, stride=pitch), :]
                          for half in range(SLAB // LANES)], axis=1) for t in range(tc)], axis=0)
    uh = u.astype(BF16)
    y_ref[...] = d_ref[...] * u
    blk = 2 * LANES
    nblk = SLAB_STATE // blk
    grp = max(1, 16 // nc)
    lanes = lambda b: slice(b * blk, (b + 1) * blk)
    step_rows = lambda t: slice(t * nc, (t + 1) * nc)

    if carry_mode:
        @pl.when(i % tiles_per_seq == 0)
        def _():
            carry_s[...] = jnp.zeros_like(carry_s)
    else:
        g_s[...] = h0_ref[...]

    for b in range(nblk):
        re, im = lanes(b), lanes(nblk + b)
        st_s[:, re] = jnp.dot(uh, bb_ref[:, re], preferred_element_type=F32)
        st_s[:, im] = jnp.dot(uh, bb_ref[:, im], preferred_element_type=F32)
        ar = pw_ref[0:1, re]
        ai = pw_ref[0:1, im]
        hr = jnp.zeros((nc, blk), F32)
        hi = jnp.zeros((nc, blk), F32)
        for t in range(tc):
            rows = step_rows(t)
            hr, hi = (ar * hr - ai * hi + st_s[rows, re],
                      ar * hi + ai * hr + st_s[rows, im])
            st_s[rows, re] = hr
            st_s[rows, im] = hi

        ar = pw_ref[tc - 1:tc, re]
        ai = pw_ref[tc - 1:tc, im]
        if carry_mode:
            gr = carry_s[0:1, re]
            gi = carry_s[0:1, im]
            for c in range(nc):
                g_s[c:c + 1, re] = gr
                g_s[c:c + 1, im] = gi
                gr, gi = (ar * gr - ai * gi + hr[c:c + 1, :],
                          ar * gi + ai * gr + hi[c:c + 1, :])
            carry_s[0:1, re] = gr
            carry_s[0:1, im] = gi
            hfin_ref[0:1, re] = gr
            hfin_ref[0:1, im] = gi
            gr = g_s[:, re]
            gi = g_s[:, im]
        else:
            gr = g_s[:, re]
            gi = g_s[:, im]
            hfin_ref[:, re] = ar * gr - ai * gi + hr
            hfin_ref[:, im] = ar * gi + ai * gr + hi

        for t0 in range(0, tc, grp):
            full_r, full_i = [], []
            for t in range(t0, t0 + grp):
                rows = step_rows(t)
                pr = pw_ref[t:t + 1, re]
                pi = pw_ref[t:t + 1, im]
                full_r.append(st_s[rows, re] + (pr * gr - pi * gi))
                full_i.append(st_s[rows, im] + (pr * gi + pi * gr))
            rows = slice(t0 * nc, (t0 + grp) * nc)
            hb_s[rows, re] = jnp.concatenate(full_r, axis=0).astype(BF16)
            hb_s[rows, im] = jnp.concatenate(full_i, axis=0).astype(BF16)

        y_ref[...] += (jnp.dot(hb_s[:, re], cc_ref[re, :], preferred_element_type=F32)
                       + jnp.dot(hb_s[:, im], cc_ref[im, :], preferred_element_type=F32))


def _s5(u, tables, h0, *, tt, tc, seq_len, carry_mode):
    bb, cc, pw, dd = tables
    t = u.shape[0]
    nc = tt // tc
    n_tiles = t // tt
    assert t % tt == 0 and tt % tc == 0
    if carry_mode:
        assert seq_len % tt == 0
        tiles_per_seq = seq_len // tt
        n_seq = t // seq_len
        hfin_shape = (N_SLABS, n_seq, 1, 2 * SLAB_STATE)
        hfin_spec = pl.BlockSpec((None, None, 1, 2 * SLAB_STATE),
                                 lambda s, i: (s, i // tiles_per_seq, 0, 0))
        h0 = jnp.zeros((N_SLABS, 8, 2 * SLAB_STATE), F32)
        h0_spec = pl.BlockSpec((None, 8, 2 * SLAB_STATE), lambda s, i: (s, 0, 0))
    else:
        assert seq_len == tc
        tiles_per_seq = 1
        hfin_shape = (N_SLABS, n_tiles, nc, 2 * SLAB_STATE)
        hfin_spec = pl.BlockSpec((None, None, nc, 2 * SLAB_STATE), lambda s, i: (s, i, 0, 0))
        h0_spec = pl.BlockSpec((None, nc, 2 * SLAB_STATE), lambda s, i: (s, i, 0))
    kern = functools.partial(_s5_kernel, tc=tc, nc=nc, tiles_per_seq=tiles_per_seq,
                             carry_mode=carry_mode)
    y, hfin = pl.pallas_call(
        kern,
        grid=(N_SLABS, n_tiles),
        in_specs=[
            pl.BlockSpec((tt, SLAB), lambda s, i: (i, s)),
            pl.BlockSpec((None, SLAB, 2 * SLAB_STATE), lambda s, i: (s, 0, 0)),
            pl.BlockSpec((None, 2 * SLAB_STATE, SLAB), lambda s, i: (s, 0, 0)),
            pl.BlockSpec((None, tc, 2 * SLAB_STATE), lambda s, i: (s, 0, 0)),
            pl.BlockSpec((None, 1, SLAB), lambda s, i: (s, 0, 0)),
            h0_spec,
        ],
        out_specs=[pl.BlockSpec((tt, SLAB), lambda s, i: (i, s)), hfin_spec],
        out_shape=[jax.ShapeDtypeStruct((t, D_SSM), F32),
                   jax.ShapeDtypeStruct(hfin_shape, F32)],
        scratch_shapes=[pltpu.VMEM((SLAB // LANES, nc * (tc + 8), LANES), F32),
                        pltpu.VMEM((tt, 2 * SLAB_STATE), F32),
                        pltpu.VMEM((tt, 2 * SLAB_STATE), BF16),
                        pltpu.VMEM((nc, 2 * SLAB_STATE), F32),
                        pltpu.VMEM((8, 2 * SLAB_STATE), F32)],
        compiler_params=_params(("arbitrary", "arbitrary")),
        name="s5",
    )(u, bb, cc, pw, dd, h0)
    return y, hfin.reshape(N_SLABS, -1, 2 * SLAB_STATE)


def _state_to_gp(hfin):
    n = hfin.shape[1]
    h = hfin.transpose(1, 0, 2)
    re = h[..., :SLAB_STATE].reshape(n, N_SSM_GROUPS, SSM_STATE)
    im = h[..., SLAB_STATE:].reshape(n, N_SSM_GROUPS, SSM_STATE)
    return re, im


def _state_from_gp(re, im):
    n = re.shape[0]
    h = jnp.concatenate([re.reshape(n, N_SLABS, SLAB_STATE), im.reshape(n, N_SLABS, SLAB_STATE)],
                        axis=-1)
    return h.transpose(1, 0, 2)


def _col_from_row(row):
    n = row.shape[1]
    r = lax.broadcasted_iota(jnp.int32, (n, n), 0)
    c = lax.broadcasted_iota(jnp.int32, (n, n), 1)
    return jnp.sum(jnp.where(r == c, jnp.broadcast_to(row, (n, n)), 0.0), axis=1, keepdims=True)


def _attn_prompt_kernel(first_ref, q_ref, k_ref, v_ref, cq_ref, ck_ref, o_ref,
                        m_s, l_s, acc_s, cq_s, s_s, *, bq, online):
    nq = q_ref.shape[0] // bq

    def query_block(i, carry):
        _attn_query_block(i, nq, first_ref, q_ref, k_ref, v_ref, cq_ref, ck_ref, o_ref,
                          m_s, l_s, acc_s, cq_s, s_s, bq=bq, online=online)
        return carry

    lax.fori_loop(0, nq, query_block, 0)


def _attn_query_block(i, nq, first_ref, q_ref, k_ref, v_ref, cq_ref, ck_ref, o_ref,
                      m_s, l_s, acc_s, cq_s, s_s, *, bq, online):
    bk = bq // 2
    row0 = pl.multiple_of(i * bq, bq)
    cq_s[...] = jnp.transpose(jnp.broadcast_to(cq_ref[:, pl.ds(row0, bq)], (LANES, bq)))
    if online:
        m_s[...] = jnp.full_like(m_s, NEG_INF)
    l_s[...] = jnp.zeros_like(l_s)
    acc_s[...] = jnp.zeros_like(acc_s)
    nchunk = bk // LANES
    all_rows = slice(0, bq)

    def scores(j, slot, rows=all_rows):
        start = pl.multiple_of(j * bk, bk)
        q_rows = pl.ds(pl.multiple_of(row0 + rows.start, LANES), rows.stop - rows.start)
        s_s[slot, rows, :] = lax.dot_general(q_ref[q_rows, :], k_ref[pl.ds(start, bk), :],
                                             (((1,), (1,)), ((), ())), preferred_element_type=F32)

    def reduce_block(j, slot, rows=all_rows, causal=False):
        start = pl.multiple_of(j * bk, bk)
        v = v_ref[pl.ds(start, bk), :]
        nrows = rows.stop - rows.start

        def logits(c):
            col0 = pl.multiple_of(start + c * LANES, LANES)
            t = s_s[slot, rows, c * LANES:(c + 1) * LANES] - ck_ref[:, pl.ds(col0, LANES)]
            if causal:
                r = lax.broadcasted_iota(jnp.int32, (nrows, LANES), 0)
                col = lax.broadcasted_iota(jnp.int32, (nrows, LANES), 1) + c * LANES
                t = jnp.where(col <= r, t, NEG_INF)
            return t

        if online:
            m_loc = functools.reduce(jnp.maximum, [logits(c) for c in range(nchunk)])
            m_old = m_s[rows, :]
            m_new = jnp.maximum(m_old, jnp.max(m_loc, axis=1, keepdims=True) + cq_s[rows, :])
            m_s[rows, :] = m_new
            alpha = jnp.exp2(m_old - m_new)
            shift = cq_s[rows, :] - m_new
        else:
            shift = cq_s[rows, :]
        ps = [jnp.exp2(logits(c) + shift) for c in range(nchunk)]
        p = jnp.concatenate([pc.astype(BF16) for pc in ps], axis=1)
        l_new = functools.reduce(jnp.add, ps)
        acc_new = jnp.dot(p, v, preferred_element_type=F32)
        if online:
            l_s[rows, :] = alpha * l_s[rows, :] + l_new
            acc_s[rows, :] = alpha * acc_s[rows, :] + acc_new
        else:
            l_s[rows, :] += l_new
            acc_s[rows, :] += acc_new

    first = first_ref[pl.program_id(0) * nq + i]
    scores(2 * first, 0)

    def body(p, carry):
        j = 2 * p
        scores(j + 1, 1)
        reduce_block(j, 0)
        scores(j + 2, 0)
        reduce_block(j + 1, 1)
        return carry

    lax.fori_loop(first, i, body, 0)
    late_rows = slice(bk, bq)
    scores(2 * i + 1, 1, late_rows)
    reduce_block(2 * i, 0, causal=True)
    reduce_block(2 * i + 1, 1, late_rows, causal=True)

    o_ref[pl.ds(row0, bq), :] = acc_s[...] / jnp.sum(l_s[...], axis=1, keepdims=True)


def _attn_prompt(qb, kb, vb, first, cq_rows, ck_rows, *, n_seq, seq_len, bq, online):
    assert seq_len % bq == 0
    kern = functools.partial(_attn_prompt_kernel, bq=bq, online=online)
    seq_map = lambda g, first: (g // N_HEADS, g % N_HEADS)
    c_map = lambda g, first: (g, 0, 0)
    grid_spec = pltpu.PrefetchScalarGridSpec(
        num_scalar_prefetch=1,
        grid=(n_seq * N_HEADS,),
        in_specs=[
            pl.BlockSpec((seq_len, HEAD_DIM), seq_map),
            pl.BlockSpec((seq_len, HEAD_DIM), seq_map),
            pl.BlockSpec((seq_len, HEAD_DIM), seq_map),
            pl.BlockSpec((None, 1, seq_len), c_map),
            pl.BlockSpec((None, 1, seq_len), c_map),
        ],
        out_specs=pl.BlockSpec((seq_len, HEAD_DIM), seq_map),
        scratch_shapes=[pltpu.VMEM((bq, LANES), F32), pltpu.VMEM((bq, LANES), F32),
                        pltpu.VMEM((bq, HEAD_DIM), F32), pltpu.VMEM((bq, LANES), F32),
                        pltpu.VMEM((2, bq, bq // 2), F32)],
    )
    return pl.pallas_call(
        kern,
        grid_spec=grid_spec,
        out_shape=jax.ShapeDtypeStruct((n_seq * seq_len, D_ATTN), F32),
        compiler_params=_params(("arbitrary",)),
        name="attn_prompt",
    )(first, qb, kb, vb, cq_rows, ck_rows)


def _first_live_pair(c_rows, bq):
    g, _, length = c_rows.shape
    nq = length // bq
    blocks = c_rows.reshape(g, nq, bq)
    hi = jnp.max(blocks, axis=-1)
    lo = jnp.min(blocks, axis=-1)
    dead = (hi[:, :, None] - lo[:, None, :]) < EXP2_ZERO - 1.0
    pair = jnp.arange(nq, dtype=jnp.int32)
    dead = dead & (pair[None, None, :] < pair[None, :, None])
    first = jnp.min(jnp.where(dead, nq, pair[None, None, :]), axis=-1)
    return first.reshape(g * nq).astype(jnp.int32)


def _attn_sample_kernel(q_ref, kn_ref, vn_ref, ck_ref, cv_ref, c_ref, o_ref, *, past, s_new):
    r = lax.broadcasted_iota(jnp.int32, (s_new, s_new), 0)
    col = lax.broadcasted_iota(jnp.int32, (s_new, s_new), 1)
    causal = col <= r
    nt = (((1,), (1,)), ((), ()))
    heads = range(N_HEADS)
    lanes = lambda h: slice(h * HEAD_DIM, (h + 1) * HEAD_DIM)
    head_rows = lambda h: pl.ds(h, past, stride=N_HEADS)
    logits = []
    for h in heads:
        q = q_ref[:, lanes(h)]
        c_past = c_ref[h:h + 1, 0:past]
        c_new = c_ref[h:h + 1, past:past + s_new]
        cq = _col_from_row(c_new)
        s_p = lax.dot_general(q, ck_ref[head_rows(h), :].astype(BF16), nt,
                              preferred_element_type=F32)
        s_n = lax.dot_general(q, kn_ref[:, lanes(h)], nt, preferred_element_type=F32)
        logits.append((s_p + cq - c_past, jnp.where(causal, s_n + cq - c_new, NEG_INF)))
    probs = []
    for s_p, s_n in logits:
        m = jnp.maximum(jnp.max(s_p, axis=1, keepdims=True), jnp.max(s_n, axis=1, keepdims=True))
        p_p = jnp.exp2(s_p - m)
        p_n = jnp.exp2(s_n - m)
        l = jnp.sum(p_p, axis=1, keepdims=True) + jnp.sum(p_n, axis=1, keepdims=True)
        probs.append((p_p.astype(BF16), p_n.astype(BF16), l))
    for h, (p_p, p_n, l) in zip(heads, probs):
        acc = (jnp.dot(p_p, cv_ref[head_rows(h), :].astype(BF16), preferred_element_type=F32)
               + jnp.dot(p_n, vn_ref[:, lanes(h)], preferred_element_type=F32))
        o_ref[:, lanes(h)] = acc / l


def _attn_sample(qb, kb, vb, cache_k, cache_v, c_all, *, n_seq, s_new, past):
    kern = functools.partial(_attn_sample_kernel, past=past, s_new=s_new)
    new = pl.BlockSpec((s_new, D_ATTN), lambda b: (b, 0))
    cache_k = cache_k.reshape(n_seq, past * N_HEADS, HEAD_DIM)
    cache_v = cache_v.reshape(n_seq, past * N_HEADS, HEAD_DIM)
    cache = pl.BlockSpec((None, past * N_HEADS, HEAD_DIM), lambda b: (b, 0, 0))
    return pl.pallas_call(
        kern,
        grid=(n_seq,),
        in_specs=[new, new, new, cache, cache,
                  pl.BlockSpec((N_HEADS, c_all.shape[1]), lambda b: (b, 0))],
        out_specs=new,
        out_shape=jax.ShapeDtypeStruct((n_seq * s_new, D_ATTN), F32),
        compiler_params=_params(("arbitrary",)),
        name="attn_sample",
    )(qb, kb, vb, cache_k, cache_v, c_all)


def _gelu_tanh(x):
    return 0.5 * x * (1.0 + jnp.tanh(math.sqrt(2.0 / math.pi) * (x + 0.044715 * (x * x * x))))


def _mix_kernel(x_ref, y_ref, a_ref, wglu_ref, gs_ref, ga_ref, wout_ref, gm_ref,
                x1_ref, hm_ref, park_s, *, tc, nc):
    gy = _gelu_tanh(y_ref[...])
    gate = jax.nn.sigmoid(jnp.dot(gy.astype(BF16), wglu_ref[...], preferred_element_type=F32))
    ssm = _rms(gy * gate, gs_ref[...])
    pitch = nc + 8
    planes = D_SSM // LANES
    for t in range(tc):
        for p in range(planes):
            park_s[p, t * pitch:t * pitch + nc, :] = ssm[t * nc:(t + 1) * nc,
                                                         p * LANES:(p + 1) * LANES]
    ssm = jnp.concatenate(
        [jnp.concatenate([park_s[p, pl.ds(c, tc, stride=pitch), :] for p in range(planes)], axis=1)
         for c in range(nc)], axis=0).astype(BF16)
    att = _rms(a_ref[...], ga_ref[...]).astype(BF16)
    x1 = (x_ref[...]
          + jnp.dot(ssm, wout_ref[0:D_SSM, :], preferred_element_type=F32)
          + jnp.dot(att, wout_ref[D_SSM:D_SSM + D_ATTN, :], preferred_element_type=F32))
    x1_ref[...] = x1
    hm_ref[...] = _rms(x1, gm_ref[...]).astype(BF16)


def _mix(x, y, attn, w_glu, g_ssm, g_attn, w_out, g_mlp, *, tt, tc):
    t = x.shape[0]
    nc = tt // tc
    assert t % tt == 0 and tt % tc == 0 and nc % 8 == 0
    row = lambda i: (i, 0)
    const = lambda i: (0, 0)
    once = pl.Buffered(1)
    return pl.pallas_call(
        functools.partial(_mix_kernel, tc=tc, nc=nc),
        grid=(t // tt,),
        in_specs=[
            pl.BlockSpec((tt, D_MODEL), row),
            pl.BlockSpec((tt, D_SSM), row),
            pl.BlockSpec((tt, D_ATTN), row),
            pl.BlockSpec((D_SSM, D_SSM), const, pipeline_mode=once),
            pl.BlockSpec((1, D_SSM), const),
            pl.BlockSpec((1, D_ATTN), const),
            pl.BlockSpec((D_SSM + D_ATTN, D_MODEL), const, pipeline_mode=once),
            pl.BlockSpec((1, D_MODEL), const),
        ],
        out_specs=[pl.BlockSpec((tt, D_MODEL), row), pl.BlockSpec((tt, D_MODEL), row)],
        out_shape=[jax.ShapeDtypeStruct((t, D_MODEL), F32), jax.ShapeDtypeStruct((t, D_MODEL), BF16)],
        scratch_shapes=[pltpu.VMEM((D_SSM // LANES, tc * (nc + 8), LANES), F32)],
        compiler_params=_params(("arbitrary",)),
        name="mix",
    )(x, y, attn, w_glu, g_ssm, g_attn, w_out, g_mlp)


def _mlp_kernel(x1_ref, hm_ref, wup_ref, wdn_ref, o_ref):
    j = pl.program_id(1)

    @pl.when(j == 0)
    def _():
        o_ref[...] = x1_ref[...]

    a = jnp.maximum(jnp.dot(hm_ref[...], wup_ref[...], preferred_element_type=F32), 0.0)
    o_ref[...] += jnp.dot((a * a).astype(BF16), wdn_ref[...], preferred_element_type=F32)


def _mlp(x1, hm, w_up, w_down, *, tt, tf):
    t = x1.shape[0]
    assert t % tt == 0 and D_FF % tf == 0
    return pl.pallas_call(
        _mlp_kernel,
        grid=(t // tt, D_FF // tf),
        in_specs=[
            pl.BlockSpec((tt, D_MODEL), lambda i, j: (i, 0)),
            pl.BlockSpec((tt, D_MODEL), lambda i, j: (i, 0)),
            pl.BlockSpec((D_MODEL, tf), lambda i, j: (0, j)),
            pl.BlockSpec((tf, D_MODEL), lambda i, j: (j, 0)),
        ],
        out_specs=pl.BlockSpec((tt, D_MODEL), lambda i, j: (i, 0)),
        out_shape=jax.ShapeDtypeStruct((t, D_MODEL), F32),
        compiler_params=_params(("arbitrary", "arbitrary")),
        name="mlp",
    )(x1, hm, w_up, w_down)


def _layer(x, weights, *, n_seq, seq_len, cache=None, h0=None, tiles):
    (g_mix, w_main, w_f, b_f, g_q, g_k, s5_raw, w_glu, g_ssm, g_attn, w_out, g_mlp,
     w_up, w_down) = weights
    t = n_seq * seq_len
    x2 = x.reshape(t, D_MODEL)
    tc = tiles["chunk"] if cache is None else seq_len
    u, qb, k, kb, v, vb, lf_t = _in_proj(x2, g_mix, w_main, w_f, b_f, g_q, g_k, tt=tiles["s5"])
    lf_t = lf_t.reshape(N_HEADS, n_seq, seq_len)
    logf = lf_t.transpose(1, 2, 0)
    lf_rows = lf_t.transpose(1, 0, 2).reshape(n_seq * N_HEADS, seq_len)

    if cache is None:
        tables = _s5_tables(*s5_raw, tc=tc)
        y, hfin = _s5(u, tables, None, tt=tiles["s5"], tc=tc, seq_len=seq_len,
                      carry_mode=True)
        c_rows = _cumsum_lanes(lf_rows).reshape(n_seq * N_HEADS, 1, seq_len)
        attend = functools.partial(_attn_prompt, qb, kb, vb, n_seq=n_seq, seq_len=seq_len,
                                   bq=tiles["bq"])
        bound = (HEAD_DIM ** -0.5 * LOG2E) * HEAD_DIM * jnp.max(jnp.abs(g_q)) * jnp.max(jnp.abs(g_k))
        attn = lax.cond(
            bound <= MAX_FIXED_SHIFT,
            lambda: attend(_first_live_pair(c_rows, tiles["bq"]), c_rows - bound, c_rows,
                           online=False),
            lambda: attend(jnp.zeros((c_rows.shape[0] * (seq_len // tiles["bq"]),), jnp.int32),
                           c_rows, c_rows, online=True))
    else:
        cache_k, cache_v, cache_logf = cache
        past = cache_k.shape[1]
        tables = _s5_tables(*s5_raw, tc=tc)
        y, hfin = _s5(u, tables, h0, tt=tiles["s5"], tc=tc, seq_len=seq_len,
                      carry_mode=False)
        past_rows = cache_logf.transpose(0, 2, 1).reshape(n_seq * N_HEADS, past)
        total = past + seq_len
        padded = -(-total // LANES) * LANES
        lf_all = jnp.concatenate(
            [past_rows, lf_rows, jnp.zeros((n_seq * N_HEADS, padded - total), F32)], axis=1)
        c_all = _cumsum_lanes(lf_all)
        attn = _attn_sample(qb, kb, vb, cache_k, cache_v, c_all,
                            n_seq=n_seq, s_new=seq_len, past=past)

    x1, hm = _mix(x2, y, attn, w_glu, g_ssm, g_attn, w_out, g_mlp, tt=tiles["s5"], tc=tc)
    out = _mlp(x1, hm, w_up, w_down, tt=tiles["mlp"], tf=tiles["tf"])
    h_re, h_im = _state_to_gp(hfin)
    return (out.reshape(n_seq, seq_len, D_MODEL),
            k.reshape(n_seq, seq_len, N_HEADS, HEAD_DIM),
            v.reshape(n_seq, seq_len, N_HEADS, HEAD_DIM),
            logf, h_re, h_im)


def _cast_kernel(w_ref, wf_ref, o_ref, of_ref):
    o_ref[...] = w_ref[...].astype(o_ref.dtype)
    of_ref[...] = jnp.zeros_like(of_ref)
    of_ref[0:N_HEADS, :] = wf_ref[...].astype(of_ref.dtype)


def _cast_w_in(w_t, n_main, rows_per_step=512):
    n_rows, d = w_t.shape
    assert n_main % rows_per_step == 0 and n_rows - n_main == N_HEADS and n_main % N_HEADS == 0
    return pl.pallas_call(
        _cast_kernel,
        grid=(n_main // rows_per_step,),
        in_specs=[pl.BlockSpec((rows_per_step, d), lambda i: (i, 0)),
                  pl.BlockSpec((N_HEADS, d), lambda i: (n_main // N_HEADS, 0))],
        out_specs=[pl.BlockSpec((rows_per_step, d), lambda i: (i, 0)),
                   pl.BlockSpec((LANES, d), lambda i: (0, 0))],
        out_shape=[jax.ShapeDtypeStruct((n_main, d), BF16),
                   jax.ShapeDtypeStruct((LANES, d), BF16)],
        compiler_params=_params(("arbitrary",)),
        name="cast_w_in",
    )(w_t, w_t)


def _prep_weights(l, g_norm_mix, w_in, b_f, ssm_a_re, ssm_a_im, ssm_log_step, ssm_b_re, ssm_b_im,
                  ssm_c_re, ssm_c_im, ssm_d, w_glu, g_q, g_k, g_out_ssm, g_out_attn, w_out,
                  g_norm_mlp, w_up, w_down):
    n_main = D_SSM + 3 * D_ATTN
    w_main, w_f = _cast_w_in(jnp.swapaxes(w_in[l], 0, 1), n_main)
    b = jnp.pad(b_f[l], (0, LANES - N_HEADS)).reshape(1, LANES)
    s5_raw = (ssm_a_re[l], ssm_a_im[l], ssm_log_step[l], ssm_b_re[l], ssm_b_im[l],
              ssm_c_re[l], ssm_c_im[l], ssm_d[l])
    return (g_norm_mix[l].reshape(1, D_MODEL), w_main, w_f, b,
            g_q[l].reshape(1, HEAD_DIM), g_k[l].reshape(1, HEAD_DIM), s5_raw,
            w_glu[l].astype(BF16), g_out_ssm[l].reshape(1, D_SSM), g_out_attn[l].reshape(1, D_ATTN),
            w_out[l].astype(BF16), g_norm_mlp[l].reshape(1, D_MODEL),
            w_up[l].astype(BF16), w_down[l].astype(BF16))


PROMPT_TILES = dict(s5=512, chunk=32, bq=1024, mlp=512, tf=2048)
SAMPLE_TILES = dict(s5=512, mlp=512, tf=2048)


def kernel(x_prompt, x_sample, cache_k, cache_v, cache_logf, state_ssm_re, state_ssm_im,
           g_norm_mix, w_in, b_f, ssm_a_re, ssm_a_im, ssm_log_step, ssm_b_re, ssm_b_im,
           ssm_c_re, ssm_c_im, ssm_d, w_glu, g_q, g_k, g_out_ssm, g_out_attn, w_out,
           g_norm_mlp, w_up, w_down):
    depth = w_in.shape[0]
    y_p, y_s = x_prompt, x_sample
    outs_p, outs_s = [], []
    for l in range(depth):
        weights = _prep_weights(l, g_norm_mix, w_in, b_f, ssm_a_re, ssm_a_im, ssm_log_step,
                                ssm_b_re, ssm_b_im, ssm_c_re, ssm_c_im, ssm_d, w_glu, g_q, g_k,
                                g_out_ssm, g_out_attn, w_out, g_norm_mlp, w_up, w_down)
        n_p, l_p = y_p.shape[0], y_p.shape[1]
        y_p, *rest_p = _layer(y_p, weights, n_seq=n_p, seq_len=l_p, tiles=PROMPT_TILES)
        n_s, l_s = y_s.shape[0], y_s.shape[1]
        h0 = _state_from_gp(state_ssm_re[l], state_ssm_im[l])
        y_s, *rest_s = _layer(y_s, weights, n_seq=n_s, seq_len=l_s,
                              cache=(cache_k[l], cache_v[l], cache_logf[l]), h0=h0,
                              tiles=SAMPLE_TILES)
        outs_p.append(rest_p)
        outs_s.append(rest_s)
    stack = lambda outs, idx: jnp.stack([o[idx] for o in outs])
    return (y_p, y_s,
            stack(outs_p, 0), stack(outs_p, 1), stack(outs_p, 2), stack(outs_p, 3), stack(outs_p, 4),
            stack(outs_s, 0), stack(outs_s, 1), stack(outs_s, 2), stack(outs_s, 3), stack(outs_s, 4))
```

```python
import functools
import math

import jax
import jax.numpy as jnp
from jax import lax
from jax.experimental import pallas as pl
from jax.experimental.pallas import tpu as pltpu

D_MODEL = 2048
D_SSM = 1024
SSM_GROUP = 16
N_SSM_GROUPS = 64
SSM_STATE = 64
D_ATTN = 1024
HEAD_DIM = 128
N_HEADS = 8
D_FF = 8192
EPS = 1e-6
NEG_INF = -1e30
LOG2E = math.log2(math.e)
MAX_FIXED_SHIFT = 45.0
EXP2_ZERO = -150.0

LANES = 128
SLAB = 256
N_SLABS = D_SSM // SLAB
SLAB_GROUPS = SLAB // SSM_GROUP
SLAB_STATE = SLAB_GROUPS * SSM_STATE
V7X_VMEM_BYTES = 64 * 1024 * 1024
VMEM_LIMIT = V7X_VMEM_BYTES - 2 * 1024 * 1024

F32 = jnp.float32
BF16 = jnp.bfloat16


def _params(sem, vmem=VMEM_LIMIT):
    return pltpu.CompilerParams(dimension_semantics=sem, vmem_limit_bytes=vmem)


def _rms(x, g):
    return x * lax.rsqrt(jnp.mean(x * x, axis=-1, keepdims=True) + EPS) * g


def _in_proj_kernel(x_ref, g_ref, w_ref, wf_ref, bf_ref, gq_ref, gk_ref,
                    u_ref, qb_ref, k_ref, kb_ref, v_ref, vb_ref, lft_ref):
    hb = _rms(x_ref[...], g_ref[...]).astype(BF16)

    def project(w_rows):
        return lax.dot_general(hb, w_rows, (((1,), (1,)), ((), ())), preferred_element_type=F32)

    u_ref[...] = project(w_ref[0:D_SSM, :])

    q = project(w_ref[D_SSM:D_SSM + D_ATTN, :])
    gq = gq_ref[...]
    for h in range(N_HEADS):
        sl = slice(h * HEAD_DIM, (h + 1) * HEAD_DIM)
        qb_ref[:, sl] = (_rms(q[:, sl], gq) * (HEAD_DIM ** -0.5 * LOG2E)).astype(BF16)

    k = project(w_ref[D_SSM + D_ATTN:D_SSM + 2 * D_ATTN, :])
    gk = gk_ref[...]
    for h in range(N_HEADS):
        sl = slice(h * HEAD_DIM, (h + 1) * HEAD_DIM)
        kn = _rms(k[:, sl], gk)
        k_ref[:, sl] = kn
        kb_ref[:, sl] = kn.astype(BF16)

    v = project(w_ref[D_SSM + 2 * D_ATTN:D_SSM + 3 * D_ATTN, :])
    v_ref[...] = v
    vb_ref[...] = v.astype(BF16)

    zf = project(wf_ref[...]) + bf_ref[...]
    lf = jnp.minimum(zf, 0.0) - jnp.log1p(jnp.exp(-jnp.abs(zf)))
    lft_ref[...] = jnp.transpose(lf)[:N_HEADS, :]


def _in_proj(x, g, w_main, w_f, b_f, g_q, g_k, *, tt):
    t = x.shape[0]
    assert t % tt == 0
    row = lambda i: (i, 0)
    const = lambda i: (0, 0)
    once = pl.Buffered(1)
    wide = lambda dt: jax.ShapeDtypeStruct((t, D_ATTN), dt)
    return pl.pallas_call(
        _in_proj_kernel,
        grid=(t // tt,),
        in_specs=[
            pl.BlockSpec((tt, D_MODEL), row),
            pl.BlockSpec((1, D_MODEL), const),
            pl.BlockSpec((D_SSM + 3 * D_ATTN, D_MODEL), const, pipeline_mode=once),
            pl.BlockSpec((LANES, D_MODEL), const, pipeline_mode=once),
            pl.BlockSpec((1, LANES), const),
            pl.BlockSpec((1, HEAD_DIM), const),
            pl.BlockSpec((1, HEAD_DIM), const),
        ],
        out_specs=[
            pl.BlockSpec((tt, D_SSM), row),
            pl.BlockSpec((tt, D_ATTN), row),
            pl.BlockSpec((tt, D_ATTN), row),
            pl.BlockSpec((tt, D_ATTN), row),
            pl.BlockSpec((tt, D_ATTN), row),
            pl.BlockSpec((tt, D_ATTN), row),
            pl.BlockSpec((N_HEADS, tt), lambda i: (0, i)),
        ],
        out_shape=[wide(F32), wide(BF16), wide(F32), wide(BF16), wide(F32), wide(BF16),
                   jax.ShapeDtypeStruct((N_HEADS, t), F32)],
        compiler_params=_params(("arbitrary",)),
        name="in_proj",
    )(x, g, w_main, w_f, b_f, g_q, g_k)


def _cumsum_kernel(x_ref, o_ref):
    rows, length = x_ref.shape
    upper = (lax.broadcasted_iota(jnp.int32, (LANES, LANES), 0)
             <= lax.broadcasted_iota(jnp.int32, (LANES, LANES), 1)).astype(BF16)
    carry = jnp.zeros((rows, 1), F32)
    for b in range(length // LANES):
        sl = slice(b * LANES, (b + 1) * LANES)
        x = x_ref[:, sl]
        hi = x.astype(BF16)
        rest = x - hi.astype(F32)
        mid = rest.astype(BF16)
        lo = (rest - mid.astype(F32)).astype(BF16)
        x = (jnp.dot(hi, upper, preferred_element_type=F32)
             + jnp.dot(mid, upper, preferred_element_type=F32)
             + jnp.dot(lo, upper, preferred_element_type=F32)) + carry
        o_ref[:, sl] = x * LOG2E
        carry = x[:, LANES - 1:LANES]


def _cumsum_lanes(x):
    rows, length = x.shape
    assert length % LANES == 0 and rows % 8 == 0
    return pl.pallas_call(
        _cumsum_kernel,
        out_shape=jax.ShapeDtypeStruct((rows, length), F32),
        name="cumsum",
    )(x)


def _s5_tables(a_re, a_im, log_step, b_re, b_im, c_re, c_im, d, tc):
    step = jnp.exp(log_step)[:, None]
    mag = jnp.exp(a_re * step)
    abar_re = mag * jnp.cos(a_im * step)
    abar_im = mag * jnp.sin(a_im * step)
    den = a_re * a_re + a_im * a_im
    nr = abar_re - 1.0
    ni = abar_im
    fr = (nr * a_re + ni * a_im) / den
    fi = (ni * a_re - nr * a_im) / den
    bbar_re = fr[..., None] * b_re - fi[..., None] * b_im
    bbar_im = fr[..., None] * b_im + fi[..., None] * b_re
    row_group = lambda n, per: (lax.broadcasted_iota(jnp.int32, (n, 1), 0) // per)
    col_group = lambda n, per: (lax.broadcasted_iota(jnp.int32, (1, n), 1) // per)

    def in_blockdiag(b):
        b = b.reshape(N_SLABS, SLAB_GROUPS, SSM_STATE, SSM_GROUP).transpose(0, 1, 3, 2)
        b = jnp.tile(b.reshape(N_SLABS, SLAB, SSM_STATE), (1, 1, SLAB_GROUPS))
        keep = row_group(SLAB, SSM_GROUP) == col_group(SLAB_STATE, SSM_STATE)
        return jnp.where(keep[None], b, 0.0)

    def out_blockdiag(c):
        c = c.reshape(N_SLABS, SLAB_GROUPS, SSM_GROUP, SSM_STATE).transpose(0, 1, 3, 2)
        c = jnp.tile(c.reshape(N_SLABS, SLAB_STATE, SSM_GROUP), (1, 1, SLAB_GROUPS))
        keep = row_group(SLAB_STATE, SSM_STATE) == col_group(SLAB, SSM_GROUP)
        return jnp.where(keep[None], c, 0.0)

    bb = jnp.concatenate([in_blockdiag(bbar_re), in_blockdiag(bbar_im)], axis=-1).astype(BF16)
    cc = jnp.concatenate([out_blockdiag(c_re), out_blockdiag(-c_im)], axis=1).astype(BF16)
    n = jnp.arange(1, tc + 1, dtype=F32)[:, None, None]
    pmag = jnp.exp(a_re[None] * step[None] * n)
    ang = a_im[None] * step[None] * n
    p_re = (pmag * jnp.cos(ang)).reshape(tc, N_SLABS, SLAB_STATE).transpose(1, 0, 2)
    p_im = (pmag * jnp.sin(ang)).reshape(tc, N_SLABS, SLAB_STATE).transpose(1, 0, 2)
    pw = jnp.concatenate([p_re, p_im], axis=-1)
    dd = d.reshape(N_SLABS, 1, SLAB)
    return bb, cc, pw, dd


def _s5_kernel(*refs, subs, **static):
    if subs == 1:
        _s5_tile(0, *refs, subs=subs, **static)
    else:
        def tile(sub, carry):
            _s5_tile(sub, *refs, subs=subs, **static)
            return carry
        lax.fori_loop(0, subs, tile, 0)


def _s5_tile(sub, u_ref, bb_ref, cc_ref, pw_ref, d_ref, h0_ref, y_ref, hfin_ref,
             us_s, st_s, hb_s, g_s, carry_s, *, subs, tc, nc, tiles_per_seq, carry_mode):
    i = pl.program_id(1) * subs + sub
    tile_rows = pl.ds(pl.multiple_of(sub * (tc * nc), tc * nc), tc * nc)
    seq_rows = pl.ds(pl.multiple_of(sub * nc, nc), nc)
    pitch = tc + 8
    for c in range(nc):
        chunk = pl.ds(pl.multiple_of(sub * (tc * nc) + c * tc, tc), tc)
        for half in range(SLAB // LANES):
            us_s[half, c * pitch:c * pitch + tc, :] = u_ref[chunk, half * LANES:(half + 1) * LANES]
    u = jnp.concatenate(
        [jnp.concatenate([us_s[half, pl.ds(t, nc, stride=pitch), :]
                          for half in range(SLAB // LANES)], axis=1) for t in range(tc)], axis=0)
    uh = u.astype(BF16)
    y_ref[tile_rows, :] = d_ref[...] * u
    blk = 2 * LANES
    nblk = SLAB_STATE // blk
    grp = max(1, 16 // nc)
    lanes = lambda b: slice(b * blk, (b + 1) * blk)
    step_rows = lambda t: slice(t * nc, (t + 1) * nc)

    if carry_mode:
        @pl.when(i % tiles_per_seq == 0)
        def _():
            carry_s[...] = jnp.zeros_like(carry_s)
    else:
        g_s[...] = h0_ref[seq_rows, :]

    for b in range(nblk):
        re, im = lanes(b), lanes(nblk + b)
        st_s[:, re] = jnp.dot(uh, bb_ref[:, re], preferred_element_type=F32)
        st_s[:, im] = jnp.dot(uh, bb_ref[:, im], preferred_element_type=F32)
        ar = pw_ref[0:1, re]
        ai = pw_ref[0:1, im]
        hr = jnp.zeros((nc, blk), F32)
        hi = jnp.zeros((nc, blk), F32)
        for t in range(tc):
            rows = step_rows(t)
            hr, hi = (ar * hr - ai * hi + st_s[rows, re],
                      ar * hi + ai * hr + st_s[rows, im])
            st_s[rows, re] = hr
            st_s[rows, im] = hi

        ar = pw_ref[tc - 1:tc, re]
        ai = pw_ref[tc - 1:tc, im]
        if carry_mode:
            gr = carry_s[0:1, re]
            gi = carry_s[0:1, im]
            for c in range(nc):
                g_s[c:c + 1, re] = gr
                g_s[c:c + 1, im] = gi
                gr, gi = (ar * gr - ai * gi + hr[c:c + 1, :],
                          ar * gi + ai * gr + hi[c:c + 1, :])
            carry_s[0:1, re] = gr
            carry_s[0:1, im] = gi
            hfin_ref[0:1, re] = gr
            hfin_ref[0:1, im] = gi
            gr = g_s[:, re]
            gi = g_s[:, im]
        else:
            gr = g_s[:, re]
            gi = g_s[:, im]
            hfin_ref[seq_rows, re] = ar * gr - ai * gi + hr
            hfin_ref[seq_rows, im] = ar * gi + ai * gr + hi

        for t0 in range(0, tc, grp):
            full_r, full_i = [], []
            for t in range(t0, t0 + grp):
                rows = step_rows(t)
                pr = pw_ref[t:t + 1, re]
                pi = pw_ref[t:t + 1, im]
                full_r.append(st_s[rows, re] + (pr * gr - pi * gi))
                full_i.append(st_s[rows, im] + (pr * gi + pi * gr))
            rows = slice(t0 * nc, (t0 + grp) * nc)
            hb_s[rows, re] = jnp.concatenate(full_r, axis=0).astype(BF16)
            hb_s[rows, im] = jnp.concatenate(full_i, axis=0).astype(BF16)

        y_ref[tile_rows, :] += (jnp.dot(hb_s[:, re], cc_ref[re, :], preferred_element_type=F32)
                                + jnp.dot(hb_s[:, im], cc_ref[im, :], preferred_element_type=F32))


def _s5(u, tables, h0, *, tt, tc, seq_len, carry_mode, subs=1):
    bb, cc, pw, dd = tables
    t = u.shape[0]
    nc = tt // tc
    assert t % tt == 0 and tt % tc == 0
    subs = math.gcd(subs, seq_len // tt if carry_mode else t // tt)
    step_rows = tt * subs
    n_steps = t // step_rows
    if carry_mode:
        tiles_per_seq = seq_len // tt
        assert seq_len % tt == 0 and tiles_per_seq % subs == 0
        n_seq = t // seq_len
        hfin_shape = (N_SLABS, n_seq, 1, 2 * SLAB_STATE)
        hfin_spec = pl.BlockSpec((None, None, 1, 2 * SLAB_STATE),
                                 lambda s, i: (s, (i * subs) // tiles_per_seq, 0, 0))
        h0 = jnp.zeros((N_SLABS, 8, 2 * SLAB_STATE), F32)
        h0_spec = pl.BlockSpec((None, 8, 2 * SLAB_STATE), lambda s, i: (s, 0, 0))
    else:
        assert seq_len == tc
        tiles_per_seq = 1
        hfin_shape = (N_SLABS, n_steps, subs * nc, 2 * SLAB_STATE)
        hfin_spec = pl.BlockSpec((None, None, subs * nc, 2 * SLAB_STATE),
                                 lambda s, i: (s, i, 0, 0))
        h0_spec = pl.BlockSpec((None, subs * nc, 2 * SLAB_STATE), lambda s, i: (s, i, 0))
    kern = functools.partial(_s5_kernel, subs=subs, tc=tc, nc=nc, tiles_per_seq=tiles_per_seq,
                             carry_mode=carry_mode)
    y, hfin = pl.pallas_call(
        kern,
        grid=(N_SLABS, n_steps),
        in_specs=[
            pl.BlockSpec((step_rows, SLAB), lambda s, i: (i, s)),
            pl.BlockSpec((None, SLAB, 2 * SLAB_STATE), lambda s, i: (s, 0, 0)),
            pl.BlockSpec((None, 2 * SLAB_STATE, SLAB), lambda s, i: (s, 0, 0)),
            pl.BlockSpec((None, tc, 2 * SLAB_STATE), lambda s, i: (s, 0, 0)),
            pl.BlockSpec((None, 1, SLAB), lambda s, i: (s, 0, 0)),
            h0_spec,
        ],
        out_specs=[pl.BlockSpec((step_rows, SLAB), lambda s, i: (i, s)), hfin_spec],
        out_shape=[jax.ShapeDtypeStruct((t, D_SSM), F32),
                   jax.ShapeDtypeStruct(hfin_shape, F32)],
        scratch_shapes=[pltpu.VMEM((SLAB // LANES, nc * (tc + 8), LANES), F32),
                        pltpu.VMEM((tt, 2 * SLAB_STATE), F32),
                        pltpu.VMEM((tt, 2 * SLAB_STATE), BF16),
                        pltpu.VMEM((nc, 2 * SLAB_STATE), F32),
                        pltpu.VMEM((8, 2 * SLAB_STATE), F32)],
        compiler_params=_params(("arbitrary", "arbitrary")),
        name="s5",
    )(u, bb, cc, pw, dd, h0)
    return y, hfin.reshape(N_SLABS, -1, 2 * SLAB_STATE)


def _state_to_gp(hfin):
    n = hfin.shape[1]
    h = hfin.transpose(1, 0, 2)
    re = h[..., :SLAB_STATE].reshape(n, N_SSM_GROUPS, SSM_STATE)
    im = h[..., SLAB_STATE:].reshape(n, N_SSM_GROUPS, SSM_STATE)
    return re, im


def _state_from_gp(re, im):
    n = re.shape[0]
    h = jnp.concatenate([re.reshape(n, N_SLABS, SLAB_STATE), im.reshape(n, N_SLABS, SLAB_STATE)],
                        axis=-1)
    return h.transpose(1, 0, 2)


def _col_from_row(row):
    n = row.shape[1]
    r = lax.broadcasted_iota(jnp.int32, (n, n), 0)
    c = lax.broadcasted_iota(jnp.int32, (n, n), 1)
    return jnp.sum(jnp.where(r == c, jnp.broadcast_to(row, (n, n)), 0.0), axis=1, keepdims=True)


def _attn_prompt_kernel(first_ref, q_ref, k_ref, v_ref, cq_ref, ck_ref, o_ref,
                        m_s, l_s, acc_s, cq_s, s_s, *, bq, online):
    nq = q_ref.shape[0] // bq

    def query_block(i, carry):
        _attn_query_block(i, nq, first_ref, q_ref, k_ref, v_ref, cq_ref, ck_ref, o_ref,
                          m_s, l_s, acc_s, cq_s, s_s, bq=bq, online=online)
        return carry

    lax.fori_loop(0, nq, query_block, 0)


def _attn_query_block(i, nq, first_ref, q_ref, k_ref, v_ref, cq_ref, ck_ref, o_ref,
                      m_s, l_s, acc_s, cq_s, s_s, *, bq, online):
    bk = bq // 2
    row0 = pl.multiple_of(i * bq, bq)
    cq_s[...] = jnp.transpose(jnp.broadcast_to(cq_ref[:, pl.ds(row0, bq)], (LANES, bq)))
    if online:
        m_s[...] = jnp.full_like(m_s, NEG_INF)
    l_s[...] = jnp.zeros_like(l_s)
    acc_s[...] = jnp.zeros_like(acc_s)
    nchunk = bk // LANES
    all_rows = slice(0, bq)

    def scores(j, slot, rows=all_rows):
        start = pl.multiple_of(j * bk, bk)
        q_rows = pl.ds(pl.multiple_of(row0 + rows.start, LANES), rows.stop - rows.start)
        s_s[slot, rows, :] = lax.dot_general(q_ref[q_rows, :], k_ref[pl.ds(start, bk), :],
                                             (((1,), (1,)), ((), ())), preferred_element_type=F32)

    def reduce_block(j, slot, rows=all_rows, causal=False):
        start = pl.multiple_of(j * bk, bk)
        v = v_ref[pl.ds(start, bk), :]
        nrows = rows.stop - rows.start

        def logits(c):
            col0 = pl.multiple_of(start + c * LANES, LANES)
            t = s_s[slot, rows, c * LANES:(c + 1) * LANES] - ck_ref[:, pl.ds(col0, LANES)]
            if causal:
                r = lax.broadcasted_iota(jnp.int32, (nrows, LANES), 0)
                col = lax.broadcasted_iota(jnp.int32, (nrows, LANES), 1) + c * LANES
                t = jnp.where(col <= r, t, NEG_INF)
            return t

        if online:
            m_loc = functools.reduce(jnp.maximum, [logits(c) for c in range(nchunk)])
            m_old = m_s[rows, :]
            m_new = jnp.maximum(m_old, jnp.max(m_loc, axis=1, keepdims=True) + cq_s[rows, :])
            m_s[rows, :] = m_new
            alpha = jnp.exp2(m_old - m_new)
            shift = cq_s[rows, :] - m_new
        else:
            shift = cq_s[rows, :]
        ps = [jnp.exp2(logits(c) + shift) for c in range(nchunk)]
        p = jnp.concatenate([pc.astype(BF16) for pc in ps], axis=1)
        l_new = functools.reduce(jnp.add, ps)
        acc_new = jnp.dot(p, v, preferred_element_type=F32)
        if online:
            l_s[rows, :] = alpha * l_s[rows, :] + l_new
            acc_s[rows, :] = alpha * acc_s[rows, :] + acc_new
        else:
            l_s[rows, :] += l_new
            acc_s[rows, :] += acc_new

    first = first_ref[pl.program_id(0) * nq + i]
    scores(2 * first, 0)

    def body(p, carry):
        j = 2 * p
        scores(j + 1, 1)
        reduce_block(j, 0)
        scores(j + 2, 0)
        reduce_block(j + 1, 1)
        return carry

    lax.fori_loop(first, i, body, 0)
    late_rows = slice(bk, bq)
    scores(2 * i + 1, 1, late_rows)
    reduce_block(2 * i, 0, causal=True)
    reduce_block(2 * i + 1, 1, late_rows, causal=True)

    o_ref[pl.ds(row0, bq), :] = acc_s[...] / jnp.sum(l_s[...], axis=1, keepdims=True)


def _attn_prompt(qb, kb, vb, first, cq_rows, ck_rows, *, n_seq, seq_len, bq, online):
    assert seq_len % bq == 0
    kern = functools.partial(_attn_prompt_kernel, bq=bq, online=online)
    seq_map = lambda g, first: (g // N_HEADS, g % N_HEADS)
    c_map = lambda g, first: (g, 0, 0)
    grid_spec = pltpu.PrefetchScalarGridSpec(
        num_scalar_prefetch=1,
        grid=(n_seq * N_HEADS,),
        in_specs=[
            pl.BlockSpec((seq_len, HEAD_DIM), seq_map),
            pl.BlockSpec((seq_len, HEAD_DIM), seq_map),
            pl.BlockSpec((seq_len, HEAD_DIM), seq_map),
            pl.BlockSpec((None, 1, seq_len), c_map),
            pl.BlockSpec((None, 1, seq_len), c_map),
        ],
        out_specs=pl.BlockSpec((seq_len, HEAD_DIM), seq_map),
        scratch_shapes=[pltpu.VMEM((bq, LANES), F32), pltpu.VMEM((bq, LANES), F32),
                        pltpu.VMEM((bq, HEAD_DIM), F32), pltpu.VMEM((bq, LANES), F32),
                        pltpu.VMEM((2, bq, bq // 2), F32)],
    )
    return pl.pallas_call(
        kern,
        grid_spec=grid_spec,
        out_shape=jax.ShapeDtypeStruct((n_seq * seq_len, D_ATTN), F32),
        compiler_params=_params(("arbitrary",)),
        name="attn_prompt",
    )(first, qb, kb, vb, cq_rows, ck_rows)


def _first_live_pair(c_rows, bq):
    g, _, length = c_rows.shape
    nq = length // bq
    blocks = c_rows.reshape(g, nq, bq)
    hi = jnp.max(blocks, axis=-1)
    lo = jnp.min(blocks, axis=-1)
    dead = (hi[:, :, None] - lo[:, None, :]) < EXP2_ZERO - 1.0
    pair = jnp.arange(nq, dtype=jnp.int32)
    dead = dead & (pair[None, None, :] < pair[None, :, None])
    first = jnp.min(jnp.where(dead, nq, pair[None, None, :]), axis=-1)
    return first.reshape(g * nq).astype(jnp.int32)


def _attn_sample_kernel(q_ref, kn_ref, vn_ref, ck_ref, cv_ref, c_ref, o_ref, *, past, s_new):
    r = lax.broadcasted_iota(jnp.int32, (s_new, s_new), 0)
    col = lax.broadcasted_iota(jnp.int32, (s_new, s_new), 1)
    causal = col <= r
    nt = (((1,), (1,)), ((), ()))
    heads = range(N_HEADS)
    lanes = lambda h: slice(h * HEAD_DIM, (h + 1) * HEAD_DIM)
    head_rows = lambda h: pl.ds(h, past, stride=N_HEADS)
    logits = []
    for h in heads:
        q = q_ref[:, lanes(h)]
        c_past = c_ref[h:h + 1, 0:past]
        c_new = c_ref[h:h + 1, past:past + s_new]
        cq = _col_from_row(c_new)
        s_p = lax.dot_general(q, ck_ref[head_rows(h), :].astype(BF16), nt,
                              preferred_element_type=F32)
        s_n = lax.dot_general(q, kn_ref[:, lanes(h)], nt, preferred_element_type=F32)
        logits.append((s_p + cq - c_past, jnp.where(causal, s_n + cq - c_new, NEG_INF)))
    probs = []
    for s_p, s_n in logits:
        m = jnp.maximum(jnp.max(s_p, axis=1, keepdims=True), jnp.max(s_n, axis=1, keepdims=True))
        p_p = jnp.exp2(s_p - m)
        p_n = jnp.exp2(s_n - m)
        l = jnp.sum(p_p, axis=1, keepdims=True) + jnp.sum(p_n, axis=1, keepdims=True)
        probs.append((p_p.astype(BF16), p_n.astype(BF16), l))
    for h, (p_p, p_n, l) in zip(heads, probs):
        acc = (jnp.dot(p_p, cv_ref[head_rows(h), :].astype(BF16), preferred_element_type=F32)
               + jnp.dot(p_n, vn_ref[:, lanes(h)], preferred_element_type=F32))
        o_ref[:, lanes(h)] = acc / l


def _attn_sample(qb, kb, vb, cache_k, cache_v, c_all, *, n_seq, s_new, past):
    kern = functools.partial(_attn_sample_kernel, past=past, s_new=s_new)
    new = pl.BlockSpec((s_new, D_ATTN), lambda b: (b, 0))
    cache_k = cache_k.reshape(n_seq, past * N_HEADS, HEAD_DIM)
    cache_v = cache_v.reshape(n_seq, past * N_HEADS, HEAD_DIM)
    cache = pl.BlockSpec((None, past * N_HEADS, HEAD_DIM), lambda b: (b, 0, 0))
    return pl.pallas_call(
        kern,
        grid=(n_seq,),
        in_specs=[new, new, new, cache, cache,
                  pl.BlockSpec((N_HEADS, c_all.shape[1]), lambda b: (b, 0))],
        out_specs=new,
        out_shape=jax.ShapeDtypeStruct((n_seq * s_new, D_ATTN), F32),
        compiler_params=_params(("arbitrary",)),
        name="attn_sample",
    )(qb, kb, vb, cache_k, cache_v, c_all)


def _gelu_tanh(x):
    return 0.5 * x * (1.0 + jnp.tanh(math.sqrt(2.0 / math.pi) * (x + 0.044715 * (x * x * x))))


def _mix_kernel(x_ref, y_ref, a_ref, wglu_ref, gs_ref, ga_ref, wout_ref, gm_ref,
                x1_ref, hm_ref, park_s, *, tc, nc):
    gy = _gelu_tanh(y_ref[...])
    gate = jax.nn.sigmoid(jnp.dot(gy.astype(BF16), wglu_ref[...], preferred_element_type=F32))
    ssm = _rms(gy * gate, gs_ref[...])
    pitch = nc + 8
    planes = D_SSM // LANES
    for t in range(tc):
        for p in range(planes):
            park_s[p, t * pitch:t * pitch + nc, :] = ssm[t * nc:(t + 1) * nc,
                                                         p * LANES:(p + 1) * LANES]
    ssm = jnp.concatenate(
        [jnp.concatenate([park_s[p, pl.ds(c, tc, stride=pitch), :] for p in range(planes)], axis=1)
         for c in range(nc)], axis=0).astype(BF16)
    att = _rms(a_ref[...], ga_ref[...]).astype(BF16)
    x1 = (x_ref[...]
          + jnp.dot(ssm, wout_ref[0:D_SSM, :], preferred_element_type=F32)
          + jnp.dot(att, wout_ref[D_SSM:D_SSM + D_ATTN, :], preferred_element_type=F32))
    x1_ref[...] = x1
    hm_ref[...] = _rms(x1, gm_ref[...]).astype(BF16)


def _mix(x, y, attn, w_glu, g_ssm, g_attn, w_out, g_mlp, *, tt, tc):
    t = x.shape[0]
    nc = tt // tc
    assert t % tt == 0 and tt % tc == 0 and nc % 8 == 0
    row = lambda i: (i, 0)
    const = lambda i: (0, 0)
    once = pl.Buffered(1)
    return pl.pallas_call(
        functools.partial(_mix_kernel, tc=tc, nc=nc),
        grid=(t // tt,),
        in_specs=[
            pl.BlockSpec((tt, D_MODEL), row),
            pl.BlockSpec((tt, D_SSM), row),
            pl.BlockSpec((tt, D_ATTN), row),
            pl.BlockSpec((D_SSM, D_SSM), const, pipeline_mode=once),
            pl.BlockSpec((1, D_SSM), const),
            pl.BlockSpec((1, D_ATTN), const),
            pl.BlockSpec((D_SSM + D_ATTN, D_MODEL), const, pipeline_mode=once),
            pl.BlockSpec((1, D_MODEL), const),
        ],
        out_specs=[pl.BlockSpec((tt, D_MODEL), row), pl.BlockSpec((tt, D_MODEL), row)],
        out_shape=[jax.ShapeDtypeStruct((t, D_MODEL), F32), jax.ShapeDtypeStruct((t, D_MODEL), BF16)],
        scratch_shapes=[pltpu.VMEM((D_SSM // LANES, tc * (nc + 8), LANES), F32)],
        compiler_params=_params(("arbitrary",)),
        name="mix",
    )(x, y, attn, w_glu, g_ssm, g_attn, w_out, g_mlp)


def _mlp_kernel(x1_ref, hm_ref, wup_ref, wdn_ref, o_ref):
    j = pl.program_id(1)

    @pl.when(j == 0)
    def _():
        o_ref[...] = x1_ref[...]

    a = jnp.maximum(jnp.dot(hm_ref[...], wup_ref[...], preferred_element_type=F32), 0.0)
    o_ref[...] += jnp.dot((a * a).astype(BF16), wdn_ref[...], preferred_element_type=F32)


def _mlp(x1, hm, w_up, w_down, *, tt, tf):
    t = x1.shape[0]
    assert t % tt == 0 and D_FF % tf == 0
    return pl.pallas_call(
        _mlp_kernel,
        grid=(t // tt, D_FF // tf),
        in_specs=[
            pl.BlockSpec((tt, D_MODEL), lambda i, j: (i, 0)),
            pl.BlockSpec((tt, D_MODEL), lambda i, j: (i, 0)),
            pl.BlockSpec((D_MODEL, tf), lambda i, j: (0, j)),
            pl.BlockSpec((tf, D_MODEL), lambda i, j: (j, 0)),
        ],
        out_specs=pl.BlockSpec((tt, D_MODEL), lambda i, j: (i, 0)),
        out_shape=jax.ShapeDtypeStruct((t, D_MODEL), F32),
        compiler_params=_params(("arbitrary", "arbitrary")),
        name="mlp",
    )(x1, hm, w_up, w_down)


def _layer(x, weights, *, n_seq, seq_len, cache=None, h0=None, tiles):
    (g_mix, w_main, w_f, b_f, g_q, g_k, s5_raw, w_glu, g_ssm, g_attn, w_out, g_mlp,
     w_up, w_down) = weights
    t = n_seq * seq_len
    x2 = x.reshape(t, D_MODEL)
    tc = tiles["chunk"] if cache is None else seq_len
    u, qb, k, kb, v, vb, lf_t = _in_proj(x2, g_mix, w_main, w_f, b_f, g_q, g_k, tt=tiles["s5"])
    lf_t = lf_t.reshape(N_HEADS, n_seq, seq_len)
    logf = lf_t.transpose(1, 2, 0)
    lf_rows = lf_t.transpose(1, 0, 2).reshape(n_seq * N_HEADS, seq_len)

    if cache is None:
        tables = _s5_tables(*s5_raw, tc=tc)
        y, hfin = _s5(u, tables, None, tt=tiles["s5"], tc=tc, seq_len=seq_len,
                      carry_mode=True, subs=tiles["s5_subs"])
        c_rows = _cumsum_lanes(lf_rows).reshape(n_seq * N_HEADS, 1, seq_len)
        attend = functools.partial(_attn_prompt, qb, kb, vb, n_seq=n_seq, seq_len=seq_len,
                                   bq=tiles["bq"])
        bound = (HEAD_DIM ** -0.5 * LOG2E) * HEAD_DIM * jnp.max(jnp.abs(g_q)) * jnp.max(jnp.abs(g_k))
        attn = lax.cond(
            bound <= MAX_FIXED_SHIFT,
            lambda: attend(_first_live_pair(c_rows, tiles["bq"]), c_rows - bound, c_rows,
                           online=False),
            lambda: attend(jnp.zeros((c_rows.shape[0] * (seq_len // tiles["bq"]),), jnp.int32),
                           c_rows, c_rows, online=True))
    else:
        cache_k, cache_v, cache_logf = cache
        past = cache_k.shape[1]
        tables = _s5_tables(*s5_raw, tc=tc)
        y, hfin = _s5(u, tables, h0, tt=tiles["s5"], tc=tc, seq_len=seq_len,
                      carry_mode=False, subs=tiles["s5_subs"])
        past_rows = cache_logf.transpose(0, 2, 1).reshape(n_seq * N_HEADS, past)
        total = past + seq_len
        padded = -(-total // LANES) * LANES
        lf_all = jnp.concatenate(
            [past_rows, lf_rows, jnp.zeros((n_seq * N_HEADS, padded - total), F32)], axis=1)
        c_all = _cumsum_lanes(lf_all)
        attn = _attn_sample(qb, kb, vb, cache_k, cache_v, c_all,
                            n_seq=n_seq, s_new=seq_len, past=past)

    x1, hm = _mix(x2, y, attn, w_glu, g_ssm, g_attn, w_out, g_mlp, tt=tiles["s5"], tc=tc)
    out = _mlp(x1, hm, w_up, w_down, tt=tiles["mlp"], tf=tiles["tf"])
    h_re, h_im = _state_to_gp(hfin)
    return (out.reshape(n_seq, seq_len, D_MODEL),
            k.reshape(n_seq, seq_len, N_HEADS, HEAD_DIM),
            v.reshape(n_seq, seq_len, N_HEADS, HEAD_DIM),
            logf, h_re, h_im)


def _cast_kernel(w_ref, wf_ref, o_ref, of_ref):
    o_ref[...] = w_ref[...].astype(o_ref.dtype)
    of_ref[...] = jnp.zeros_like(of_ref)
    of_ref[0:N_HEADS, :] = wf_ref[...].astype(of_ref.dtype)


def _cast_w_in(w_t, n_main, rows_per_step=512):
    n_rows, d = w_t.shape
    assert n_main % rows_per_step == 0 and n_rows - n_main == N_HEADS and n_main % N_HEADS == 0
    return pl.pallas_call(
        _cast_kernel,
        grid=(n_main // rows_per_step,),
        in_specs=[pl.BlockSpec((rows_per_step, d), lambda i: (i, 0)),
                  pl.BlockSpec((N_HEADS, d), lambda i: (n_main // N_HEADS, 0))],
        out_specs=[pl.BlockSpec((rows_per_step, d), lambda i: (i, 0)),
                   pl.BlockSpec((LANES, d), lambda i: (0, 0))],
        out_shape=[jax.ShapeDtypeStruct((n_main, d), BF16),
                   jax.ShapeDtypeStruct((LANES, d), BF16)],
        compiler_params=_params(("arbitrary",)),
        name="cast_w_in",
    )(w_t, w_t)


def _prep_weights(l, g_norm_mix, w_in, b_f, ssm_a_re, ssm_a_im, ssm_log_step, ssm_b_re, ssm_b_im,
                  ssm_c_re, ssm_c_im, ssm_d, w_glu, g_q, g_k, g_out_ssm, g_out_attn, w_out,
                  g_norm_mlp, w_up, w_down):
    n_main = D_SSM + 3 * D_ATTN
    w_main, w_f = _cast_w_in(jnp.swapaxes(w_in[l], 0, 1), n_main)
    b = jnp.pad(b_f[l], (0, LANES - N_HEADS)).reshape(1, LANES)
    s5_raw = (ssm_a_re[l], ssm_a_im[l], ssm_log_step[l], ssm_b_re[l], ssm_b_im[l],
              ssm_c_re[l], ssm_c_im[l], ssm_d[l])
    return (g_norm_mix[l].reshape(1, D_MODEL), w_main, w_f, b,
            g_q[l].reshape(1, HEAD_DIM), g_k[l].reshape(1, HEAD_DIM), s5_raw,
            w_glu[l].astype(BF16), g_out_ssm[l].reshape(1, D_SSM), g_out_attn[l].reshape(1, D_ATTN),
            w_out[l].astype(BF16), g_norm_mlp[l].reshape(1, D_MODEL),
            w_up[l].astype(BF16), w_down[l].astype(BF16))


PROMPT_TILES = dict(s5=512, s5_subs=4, chunk=32, bq=1024, mlp=512, tf=2048)
SAMPLE_TILES = dict(s5=512, s5_subs=1, mlp=512, tf=2048)


def kernel(x_prompt, x_sample, cache_k, cache_v, cache_logf, state_ssm_re, state_ssm_im,
           g_norm_mix, w_in, b_f, ssm_a_re, ssm_a_im, ssm_log_step, ssm_b_re, ssm_b_im,
           ssm_c_re, ssm_c_im, ssm_d, w_glu, g_q, g_k, g_out_ssm, g_out_attn, w_out,
           g_norm_mlp, w_up, w_down):
    depth = w_in.shape[0]
    y_p, y_s = x_prompt, x_sample
    outs_p, outs_s = [], []
    for l in range(depth):
        weights = _prep_weights(l, g_norm_mix, w_in, b_f, ssm_a_re, ssm_a_im, ssm_log_step,
                                ssm_b_re, ssm_b_im, ssm_c_re, ssm_c_im, ssm_d, w_glu, g_q, g_k,
                                g_out_ssm, g_out_attn, w_out, g_norm_mlp, w_up, w_down)
        n_p, l_p = y_p.shape[0], y_p.shape[1]
        y_p, *rest_p = _layer(y_p, weights, n_seq=n_p, seq_len=l_p, tiles=PROMPT_TILES)
        n_s, l_s = y_s.shape[0], y_s.shape[1]
        h0 = _state_from_gp(state_ssm_re[l], state_ssm_im[l])
        y_s, *rest_s = _layer(y_s, weights, n_seq=n_s, seq_len=l_s,
                              cache=(cache_k[l], cache_v[l], cache_logf[l]), h0=h0,
                              tiles=SAMPLE_TILES)
        outs_p.append(rest_p)
        outs_s.append(rest_s)
    stack = lambda outs, idx: jnp.stack([o[idx] for o in outs])
    return (y_p, y_s,
            stack(outs_p, 0), stack(outs_p, 1), stack(outs_p, 2), stack(outs_p, 3), stack(outs_p, 4),
            stack(outs_s, 0), stack(outs_s, 1), stack(outs_s, 2), stack(outs_s, 3), stack(outs_s, 4))
```

```python
import functools
import math

import jax
import jax.numpy as jnp
from jax import lax
from jax.experimental import pallas as pl
from jax.experimental.pallas import tpu as pltpu

D_MODEL = 2048
D_SSM = 1024
SSM_GROUP = 16
N_SSM_GROUPS = 64
SSM_STATE = 64
D_ATTN = 1024
HEAD_DIM = 128
N_HEADS = 8
D_FF = 8192
EPS = 1e-6
NEG_INF = -1e30
LOG2E = math.log2(math.e)
MAX_FIXED_SHIFT = 45.0
EXP2_ZERO = -150.0

LANES = 128
SLAB = 256
N_SLABS = D_SSM // SLAB
SLAB_GROUPS = SLAB // SSM_GROUP
SLAB_STATE = SLAB_GROUPS * SSM_STATE
V7X_VMEM_BYTES = 64 * 1024 * 1024
VMEM_LIMIT = V7X_VMEM_BYTES - 2 * 1024 * 1024

F32 = jnp.float32
BF16 = jnp.bfloat16


def _params(sem, vmem=VMEM_LIMIT):
    return pltpu.CompilerParams(dimension_semantics=sem, vmem_limit_bytes=vmem)


def _rms(x, g):
    return x * lax.rsqrt(jnp.mean(x * x, axis=-1, keepdims=True) + EPS) * g


def _in_proj_kernel(x_ref, g_ref, w_ref, wf_ref, bf_ref, gq_ref, gk_ref,
                    u_ref, qb_ref, k_ref, kb_ref, v_ref, vb_ref, lft_ref):
    hb = _rms(x_ref[...], g_ref[...]).astype(BF16)

    def project(w_rows):
        return lax.dot_general(hb, w_rows, (((1,), (1,)), ((), ())), preferred_element_type=F32)

    u_ref[...] = project(w_ref[0:D_SSM, :])

    q = project(w_ref[D_SSM:D_SSM + D_ATTN, :])
    gq = gq_ref[...]
    for h in range(N_HEADS):
        sl = slice(h * HEAD_DIM, (h + 1) * HEAD_DIM)
        qb_ref[:, sl] = (_rms(q[:, sl], gq) * (HEAD_DIM ** -0.5 * LOG2E)).astype(BF16)

    k = project(w_ref[D_SSM + D_ATTN:D_SSM + 2 * D_ATTN, :])
    gk = gk_ref[...]
    for h in range(N_HEADS):
        sl = slice(h * HEAD_DIM, (h + 1) * HEAD_DIM)
        kn = _rms(k[:, sl], gk)
        k_ref[:, sl] = kn
        kb_ref[:, sl] = kn.astype(BF16)

    v = project(w_ref[D_SSM + 2 * D_ATTN:D_SSM + 3 * D_ATTN, :])
    v_ref[...] = v
    vb_ref[...] = v.astype(BF16)

    zf = project(wf_ref[...]) + bf_ref[...]
    lf = jnp.minimum(zf, 0.0) - jnp.log1p(jnp.exp(-jnp.abs(zf)))
    lft_ref[...] = jnp.transpose(lf)[:N_HEADS, :]


def _in_proj(x, g, w_main, w_f, b_f, g_q, g_k, *, tt):
    t = x.shape[0]
    assert t % tt == 0
    row = lambda i: (i, 0)
    const = lambda i: (0, 0)
    once = pl.Buffered(1)
    wide = lambda dt: jax.ShapeDtypeStruct((t, D_ATTN), dt)
    return pl.pallas_call(
        _in_proj_kernel,
        grid=(t // tt,),
        in_specs=[
            pl.BlockSpec((tt, D_MODEL), row),
            pl.BlockSpec((1, D_MODEL), const),
            pl.BlockSpec((D_SSM + 3 * D_ATTN, D_MODEL), const, pipeline_mode=once),
            pl.BlockSpec((LANES, D_MODEL), const, pipeline_mode=once),
            pl.BlockSpec((1, LANES), const),
            pl.BlockSpec((1, HEAD_DIM), const),
            pl.BlockSpec((1, HEAD_DIM), const),
        ],
        out_specs=[
            pl.BlockSpec((tt, D_SSM), row),
            pl.BlockSpec((tt, D_ATTN), row),
            pl.BlockSpec((tt, D_ATTN), row),
            pl.BlockSpec((tt, D_ATTN), row),
            pl.BlockSpec((tt, D_ATTN), row),
            pl.BlockSpec((tt, D_ATTN), row),
            pl.BlockSpec((N_HEADS, tt), lambda i: (0, i)),
        ],
        out_shape=[wide(F32), wide(BF16), wide(F32), wide(BF16), wide(F32), wide(BF16),
                   jax.ShapeDtypeStruct((N_HEADS, t), F32)],
        compiler_params=_params(("arbitrary",)),
        name="in_proj",
    )(x, g, w_main, w_f, b_f, g_q, g_k)


def _cumsum_kernel(x_ref, o_ref):
    rows, length = x_ref.shape
    upper = (lax.broadcasted_iota(jnp.int32, (LANES, LANES), 0)
             <= lax.broadcasted_iota(jnp.int32, (LANES, LANES), 1)).astype(BF16)
    carry = jnp.zeros((rows, 1), F32)
    for b in range(length // LANES):
        sl = slice(b * LANES, (b + 1) * LANES)
        x = x_ref[:, sl]
        hi = x.astype(BF16)
        rest = x - hi.astype(F32)
        mid = rest.astype(BF16)
        lo = (rest - mid.astype(F32)).astype(BF16)
        x = (jnp.dot(hi, upper, preferred_element_type=F32)
             + jnp.dot(mid, upper, preferred_element_type=F32)
             + jnp.dot(lo, upper, preferred_element_type=F32)) + carry
        o_ref[:, sl] = x * LOG2E
        carry = x[:, LANES - 1:LANES]


def _cumsum_lanes(x):
    rows, length = x.shape
    assert length % LANES == 0 and rows % 8 == 0
    return pl.pallas_call(
        _cumsum_kernel,
        out_shape=jax.ShapeDtypeStruct((rows, length), F32),
        name="cumsum",
    )(x)


def _s5_tables(a_re, a_im, log_step, b_re, b_im, c_re, c_im, d, tc):
    step = jnp.exp(log_step)[:, None]
    mag = jnp.exp(a_re * step)
    abar_re = mag * jnp.cos(a_im * step)
    abar_im = mag * jnp.sin(a_im * step)
    den = a_re * a_re + a_im * a_im
    nr = abar_re - 1.0
    ni = abar_im
    fr = (nr * a_re + ni * a_im) / den
    fi = (ni * a_re - nr * a_im) / den
    bbar_re = fr[..., None] * b_re - fi[..., None] * b_im
    bbar_im = fr[..., None] * b_im + fi[..., None] * b_re
    row_group = lambda n, per: (lax.broadcasted_iota(jnp.int32, (n, 1), 0) // per)
    col_group = lambda n, per: (lax.broadcasted_iota(jnp.int32, (1, n), 1) // per)

    def in_blockdiag(b):
        b = b.reshape(N_SLABS, SLAB_GROUPS, SSM_STATE, SSM_GROUP).transpose(0, 1, 3, 2)
        b = jnp.tile(b.reshape(N_SLABS, SLAB, SSM_STATE), (1, 1, SLAB_GROUPS))
        keep = row_group(SLAB, SSM_GROUP) == col_group(SLAB_STATE, SSM_STATE)
        return jnp.where(keep[None], b, 0.0)

    def out_blockdiag(c):
        c = c.reshape(N_SLABS, SLAB_GROUPS, SSM_GROUP, SSM_STATE).transpose(0, 1, 3, 2)
        c = jnp.tile(c.reshape(N_SLABS, SLAB_STATE, SSM_GROUP), (1, 1, SLAB_GROUPS))
        keep = row_group(SLAB_STATE, SSM_STATE) == col_group(SLAB, SSM_GROUP)
        return jnp.where(keep[None], c, 0.0)

    bb = jnp.concatenate([in_blockdiag(bbar_re), in_blockdiag(bbar_im)], axis=-1).astype(BF16)
    cc = jnp.concatenate([out_blockdiag(c_re), out_blockdiag(-c_im)], axis=1).astype(BF16)
    n = jnp.arange(1, tc + 1, dtype=F32)[:, None, None]
    pmag = jnp.exp(a_re[None] * step[None] * n)
    ang = a_im[None] * step[None] * n
    p_re = (pmag * jnp.cos(ang)).reshape(tc, N_SLABS, SLAB_STATE).transpose(1, 0, 2)
    p_im = (pmag * jnp.sin(ang)).reshape(tc, N_SLABS, SLAB_STATE).transpose(1, 0, 2)
    pw = jnp.concatenate([p_re, p_im], axis=-1)
    dd = d.reshape(N_SLABS, 1, SLAB)
    return bb, cc, pw, dd


def _s5_kernel(*refs, subs, **static):
    if subs == 1:
        _s5_tile(0, *refs, subs=subs, **static)
    else:
        def tile(sub, carry):
            _s5_tile(sub, *refs, subs=subs, **static)
            return carry
        lax.fori_loop(0, subs, tile, 0)


def _s5_tile(sub, u_ref, bb_ref, cc_ref, pw_ref, d_ref, h0_ref, y_ref, hfin_ref,
             us_s, st_s, hb_s, g_s, carry_s, *, subs, tc, nc, tiles_per_seq, carry_mode):
    i = pl.program_id(1) * subs + sub
    tile_rows = pl.ds(pl.multiple_of(sub * (tc * nc), tc * nc), tc * nc)
    seq_rows = pl.ds(pl.multiple_of(sub * nc, nc), nc)
    pitch = tc + 8
    for c in range(nc):
        chunk = pl.ds(pl.multiple_of(sub * (tc * nc) + c * tc, tc), tc)
        for half in range(SLAB // LANES):
            us_s[half, c * pitch:c * pitch + tc, :] = u_ref[chunk, half * LANES:(half + 1) * LANES]
    u = jnp.concatenate(
        [jnp.concatenate([us_s[half, pl.ds(t, nc, stride=pitch), :]
                          for half in range(SLAB // LANES)], axis=1) for t in range(tc)], axis=0)
    uh = u.astype(BF16)
    y_ref[tile_rows, :] = d_ref[...] * u
    blk = 2 * LANES
    nblk = SLAB_STATE // blk
    grp = max(1, 16 // nc)
    lanes = lambda b: slice(b * blk, (b + 1) * blk)
    step_rows = lambda t: slice(t * nc, (t + 1) * nc)

    if carry_mode:
        @pl.when(i % tiles_per_seq == 0)
        def _():
            carry_s[...] = jnp.zeros_like(carry_s)
    else:
        g_s[...] = h0_ref[seq_rows, :]

    for b in range(nblk):
        re, im = lanes(b), lanes(nblk + b)
        st_s[:, re] = jnp.dot(uh, bb_ref[:, re], preferred_element_type=F32)
        st_s[:, im] = jnp.dot(uh, bb_ref[:, im], preferred_element_type=F32)
        ar = pw_ref[0:1, re]
        ai = pw_ref[0:1, im]
        hr = jnp.zeros((nc, blk), F32)
        hi = jnp.zeros((nc, blk), F32)
        for t in range(tc):
            rows = step_rows(t)
            hr, hi = (ar * hr - ai * hi + st_s[rows, re],
                      ar * hi + ai * hr + st_s[rows, im])
            st_s[rows, re] = hr
            st_s[rows, im] = hi

        ar = pw_ref[tc - 1:tc, re]
        ai = pw_ref[tc - 1:tc, im]
        if carry_mode:
            gr = carry_s[0:1, re]
            gi = carry_s[0:1, im]
            for c in range(nc):
                g_s[c:c + 1, re] = gr
                g_s[c:c + 1, im] = gi
                gr, gi = (ar * gr - ai * gi + hr[c:c + 1, :],
                          ar * gi + ai * gr + hi[c:c + 1, :])
            carry_s[0:1, re] = gr
            carry_s[0:1, im] = gi
            hfin_ref[0:1, re] = gr
            hfin_ref[0:1, im] = gi
            gr = g_s[:, re]
            gi = g_s[:, im]
        else:
            gr = g_s[:, re]
            gi = g_s[:, im]
            hfin_ref[seq_rows, re] = ar * gr - ai * gi + hr
            hfin_ref[seq_rows, im] = ar * gi + ai * gr + hi

        for t0 in range(0, tc, grp):
            full_r, full_i = [], []
            for t in range(t0, t0 + grp):
                rows = step_rows(t)
                pr = pw_ref[t:t + 1, re]
                pi = pw_ref[t:t + 1, im]
                full_r.append(st_s[rows, re] + (pr * gr - pi * gi))
                full_i.append(st_s[rows, im] + (pr * gi + pi * gr))
            rows = slice(t0 * nc, (t0 + grp) * nc)
            hb_s[rows, re] = jnp.concatenate(full_r, axis=0).astype(BF16)
            hb_s[rows, im] = jnp.concatenate(full_i, axis=0).astype(BF16)

        y_ref[tile_rows, :] += (jnp.dot(hb_s[:, re], cc_ref[re, :], preferred_element_type=F32)
                                + jnp.dot(hb_s[:, im], cc_ref[im, :], preferred_element_type=F32))


def _s5(u, tables, h0, *, tt, tc, seq_len, carry_mode, subs=1):
    bb, cc, pw, dd = tables
    t = u.shape[0]
    nc = tt // tc
    assert t % tt == 0 and tt % tc == 0
    subs = math.gcd(subs, seq_len // tt if carry_mode else t // tt)
    step_rows = tt * subs
    n_steps = t // step_rows
    if carry_mode:
        tiles_per_seq = seq_len // tt
        assert seq_len % tt == 0 and tiles_per_seq % subs == 0
        n_seq = t // seq_len
        hfin_shape = (N_SLABS, n_seq, 1, 2 * SLAB_STATE)
        hfin_spec = pl.BlockSpec((None, None, 1, 2 * SLAB_STATE),
                                 lambda s, i: (s, (i * subs) // tiles_per_seq, 0, 0))
        h0 = jnp.zeros((N_SLABS, 8, 2 * SLAB_STATE), F32)
        h0_spec = pl.BlockSpec((None, 8, 2 * SLAB_STATE), lambda s, i: (s, 0, 0))
    else:
        assert seq_len == tc
        tiles_per_seq = 1
        hfin_shape = (N_SLABS, n_steps, subs * nc, 2 * SLAB_STATE)
        hfin_spec = pl.BlockSpec((None, None, subs * nc, 2 * SLAB_STATE),
                                 lambda s, i: (s, i, 0, 0))
        h0_spec = pl.BlockSpec((None, subs * nc, 2 * SLAB_STATE), lambda s, i: (s, i, 0))
    kern = functools.partial(_s5_kernel, subs=subs, tc=tc, nc=nc, tiles_per_seq=tiles_per_seq,
                             carry_mode=carry_mode)
    y, hfin = pl.pallas_call(
        kern,
        grid=(N_SLABS, n_steps),
        in_specs=[
            pl.BlockSpec((step_rows, SLAB), lambda s, i: (i, s)),
            pl.BlockSpec((None, SLAB, 2 * SLAB_STATE), lambda s, i: (s, 0, 0)),
            pl.BlockSpec((None, 2 * SLAB_STATE, SLAB), lambda s, i: (s, 0, 0)),
            pl.BlockSpec((None, tc, 2 * SLAB_STATE), lambda s, i: (s, 0, 0)),
            pl.BlockSpec((None, 1, SLAB), lambda s, i: (s, 0, 0)),
            h0_spec,
        ],
        out_specs=[pl.BlockSpec((step_rows, SLAB), lambda s, i: (i, s)), hfin_spec],
        out_shape=[jax.ShapeDtypeStruct((t, D_SSM), F32),
                   jax.ShapeDtypeStruct(hfin_shape, F32)],
        scratch_shapes=[pltpu.VMEM((SLAB // LANES, nc * (tc + 8), LANES), F32),
                        pltpu.VMEM((tt, 2 * SLAB_STATE), F32),
                        pltpu.VMEM((tt, 2 * SLAB_STATE), BF16),
                        pltpu.VMEM((nc, 2 * SLAB_STATE), F32),
                        pltpu.VMEM((8, 2 * SLAB_STATE), F32)],
        compiler_params=_params(("arbitrary", "arbitrary")),
        name="s5",
    )(u, bb, cc, pw, dd, h0)
    return y, hfin.reshape(N_SLABS, -1, 2 * SLAB_STATE)


def _state_to_gp(hfin):
    n = hfin.shape[1]
    h = hfin.transpose(1, 0, 2)
    re = h[..., :SLAB_STATE].reshape(n, N_SSM_GROUPS, SSM_STATE)
    im = h[..., SLAB_STATE:].reshape(n, N_SSM_GROUPS, SSM_STATE)
    return re, im


def _state_from_gp(re, im):
    n = re.shape[0]
    h = jnp.concatenate([re.reshape(n, N_SLABS, SLAB_STATE), im.reshape(n, N_SLABS, SLAB_STATE)],
                        axis=-1)
    return h.transpose(1, 0, 2)


def _col_from_row(row):
    n = row.shape[1]
    r = lax.broadcasted_iota(jnp.int32, (n, n), 0)
    c = lax.broadcasted_iota(jnp.int32, (n, n), 1)
    return jnp.sum(jnp.where(r == c, jnp.broadcast_to(row, (n, n)), 0.0), axis=1, keepdims=True)


def _attn_prompt_kernel(first_ref, q_ref, k_ref, v_ref, cq_ref, ck_ref, o_ref,
                        m_s, l_s, acc_s, cq_s, s_s, *, bq, online):
    nq = q_ref.shape[0] // bq

    def query_block(i, carry):
        _attn_query_block(i, nq, first_ref, q_ref, k_ref, v_ref, cq_ref, ck_ref, o_ref,
                          m_s, l_s, acc_s, cq_s, s_s, bq=bq, online=online)
        return carry

    lax.fori_loop(0, nq, query_block, 0)


def _attn_query_block(i, nq, first_ref, q_ref, k_ref, v_ref, cq_ref, ck_ref, o_ref,
                      m_s, l_s, acc_s, cq_s, s_s, *, bq, online):
    bk = bq // 2
    row0 = pl.multiple_of(i * bq, bq)
    cq_s[...] = jnp.transpose(jnp.broadcast_to(cq_ref[:, pl.ds(row0, bq)], (LANES, bq)))
    if online:
        m_s[...] = jnp.full_like(m_s, NEG_INF)
    l_s[...] = jnp.zeros_like(l_s)
    acc_s[...] = jnp.zeros_like(acc_s)
    nchunk = bk // LANES
    all_rows = slice(0, bq)

    def scores(j, slot, rows=all_rows):
        start = pl.multiple_of(j * bk, bk)
        q_rows = pl.ds(pl.multiple_of(row0 + rows.start, LANES), rows.stop - rows.start)
        s_s[slot, rows, :] = lax.dot_general(q_ref[q_rows, :], k_ref[pl.ds(start, bk), :],
                                             (((1,), (1,)), ((), ())), preferred_element_type=F32)

    def reduce_block(j, slot, rows=all_rows, causal=False):
        start = pl.multiple_of(j * bk, bk)
        v = v_ref[pl.ds(start, bk), :]
        nrows = rows.stop - rows.start

        def logits(c):
            col0 = pl.multiple_of(start + c * LANES, LANES)
            t = s_s[slot, rows, c * LANES:(c + 1) * LANES] - ck_ref[:, pl.ds(col0, LANES)]
            if causal:
                r = lax.broadcasted_iota(jnp.int32, (nrows, LANES), 0)
                col = lax.broadcasted_iota(jnp.int32, (nrows, LANES), 1) + c * LANES
                t = jnp.where(col <= r, t, NEG_INF)
            return t

        if online:
            m_loc = functools.reduce(jnp.maximum, [logits(c) for c in range(nchunk)])
            m_old = m_s[rows, :]
            m_new = jnp.maximum(m_old, jnp.max(m_loc, axis=1, keepdims=True) + cq_s[rows, :])
            m_s[rows, :] = m_new
            alpha = jnp.exp2(m_old - m_new)
            shift = cq_s[rows, :] - m_new
        else:
            shift = cq_s[rows, :]
        ps = [jnp.exp2(logits(c) + shift) for c in range(nchunk)]
        p = jnp.concatenate([pc.astype(BF16) for pc in ps], axis=1)
        l_new = functools.reduce(jnp.add, ps)
        acc_new = jnp.dot(p, v, preferred_element_type=F32)
        if online:
            l_s[rows, :] = alpha * l_s[rows, :] + l_new
            acc_s[rows, :] = alpha * acc_s[rows, :] + acc_new
        else:
            l_s[rows, :] += l_new
            acc_s[rows, :] += acc_new

    first = first_ref[pl.program_id(0) * nq + i]
    scores(2 * first, 0)

    def body(p, carry):
        j = 2 * p
        scores(j + 1, 1)
        reduce_block(j, 0)
        scores(j + 2, 0)
        reduce_block(j + 1, 1)
        return carry

    lax.fori_loop(first, i, body, 0)
    late_rows = slice(bk, bq)
    scores(2 * i + 1, 1, late_rows)
    reduce_block(2 * i, 0, causal=True)
    reduce_block(2 * i + 1, 1, late_rows, causal=True)

    o_ref[pl.ds(row0, bq), :] = acc_s[...] / jnp.sum(l_s[...], axis=1, keepdims=True)


def _attn_prompt(qb, kb, vb, first, cq_rows, ck_rows, *, n_seq, seq_len, bq, online):
    assert seq_len % bq == 0
    kern = functools.partial(_attn_prompt_kernel, bq=bq, online=online)
    seq_map = lambda g, first: (g // N_HEADS, g % N_HEADS)
    c_map = lambda g, first: (g, 0, 0)
    grid_spec = pltpu.PrefetchScalarGridSpec(
        num_scalar_prefetch=1,
        grid=(n_seq * N_HEADS,),
        in_specs=[
            pl.BlockSpec((seq_len, HEAD_DIM), seq_map),
            pl.BlockSpec((seq_len, HEAD_DIM), seq_map),
            pl.BlockSpec((seq_len, HEAD_DIM), seq_map),
            pl.BlockSpec((None, 1, seq_len), c_map),
            pl.BlockSpec((None, 1, seq_len), c_map),
        ],
        out_specs=pl.BlockSpec((seq_len, HEAD_DIM), seq_map),
        scratch_shapes=[pltpu.VMEM((bq, LANES), F32), pltpu.VMEM((bq, LANES), F32),
                        pltpu.VMEM((bq, HEAD_DIM), F32), pltpu.VMEM((bq, LANES), F32),
                        pltpu.VMEM((2, bq, bq // 2), F32)],
    )
    return pl.pallas_call(
        kern,
        grid_spec=grid_spec,
        out_shape=jax.ShapeDtypeStruct((n_seq * seq_len, D_ATTN), F32),
        compiler_params=_params(("arbitrary",)),
        name="attn_prompt",
    )(first, qb, kb, vb, cq_rows, ck_rows)


def _first_live_pair(c_rows, bq):
    g, _, length = c_rows.shape
    nq = length // bq
    blocks = c_rows.reshape(g, nq, bq)
    hi = jnp.max(blocks, axis=-1)
    lo = jnp.min(blocks, axis=-1)
    dead = (hi[:, :, None] - lo[:, None, :]) < EXP2_ZERO - 1.0
    pair = jnp.arange(nq, dtype=jnp.int32)
    dead = dead & (pair[None, None, :] < pair[None, :, None])
    first = jnp.min(jnp.where(dead, nq, pair[None, None, :]), axis=-1)
    return first.reshape(g * nq).astype(jnp.int32)


def _attn_sample_kernel(q_ref, kn_ref, vn_ref, ck_ref, cv_ref, c_ref, o_ref, *, past, s_new):
    r = lax.broadcasted_iota(jnp.int32, (s_new, s_new), 0)
    col = lax.broadcasted_iota(jnp.int32, (s_new, s_new), 1)
    causal = col <= r
    nt = (((1,), (1,)), ((), ()))
    heads = range(N_HEADS)
    lanes = lambda h: slice(h * HEAD_DIM, (h + 1) * HEAD_DIM)
    head_rows = lambda h: pl.ds(h, past, stride=N_HEADS)
    logits = []
    for h in heads:
        q = q_ref[:, lanes(h)]
        c_past = c_ref[h:h + 1, 0:past]
        c_new = c_ref[h:h + 1, past:past + s_new]
        cq = _col_from_row(c_new)
        s_p = lax.dot_general(q, ck_ref[head_rows(h), :].astype(BF16), nt,
                              preferred_element_type=F32)
        s_n = lax.dot_general(q, kn_ref[:, lanes(h)], nt, preferred_element_type=F32)
        logits.append((s_p + cq - c_past, jnp.where(causal, s_n + cq - c_new, NEG_INF)))
    probs = []
    for s_p, s_n in logits:
        m = jnp.maximum(jnp.max(s_p, axis=1, keepdims=True), jnp.max(s_n, axis=1, keepdims=True))
        p_p = jnp.exp2(s_p - m)
        p_n = jnp.exp2(s_n - m)
        l = jnp.sum(p_p, axis=1, keepdims=True) + jnp.sum(p_n, axis=1, keepdims=True)
        probs.append((p_p.astype(BF16), p_n.astype(BF16), l))
    for h, (p_p, p_n, l) in zip(heads, probs):
        acc = (jnp.dot(p_p, cv_ref[head_rows(h), :].astype(BF16), preferred_element_type=F32)
               + jnp.dot(p_n, vn_ref[:, lanes(h)], preferred_element_type=F32))
        o_ref[:, lanes(h)] = acc / l


def _attn_sample(qb, kb, vb, cache_k, cache_v, c_all, *, n_seq, s_new, past):
    kern = functools.partial(_attn_sample_kernel, past=past, s_new=s_new)
    new = pl.BlockSpec((s_new, D_ATTN), lambda b: (b, 0))
    cache_k = cache_k.reshape(n_seq, past * N_HEADS, HEAD_DIM)
    cache_v = cache_v.reshape(n_seq, past * N_HEADS, HEAD_DIM)
    cache = pl.BlockSpec((None, past * N_HEADS, HEAD_DIM), lambda b: (b, 0, 0))
    return pl.pallas_call(
        kern,
        grid=(n_seq,),
        in_specs=[new, new, new, cache, cache,
                  pl.BlockSpec((N_HEADS, c_all.shape[1]), lambda b: (b, 0))],
        out_specs=new,
        out_shape=jax.ShapeDtypeStruct((n_seq * s_new, D_ATTN), F32),
        compiler_params=_params(("arbitrary",)),
        name="attn_sample",
    )(qb, kb, vb, cache_k, cache_v, c_all)


def _gelu_tanh(x):
    return 0.5 * x * (1.0 + jnp.tanh(math.sqrt(2.0 / math.pi) * (x + 0.044715 * (x * x * x))))


def _mix_kernel(x_ref, y_ref, a_ref, wglu_ref, gs_ref, ga_ref, wout_ref, gm_ref,
                x1_ref, hm_ref, park_s, *, tc, nc):
    gy = _gelu_tanh(y_ref[...])
    gate = jax.nn.sigmoid(jnp.dot(gy.astype(BF16), wglu_ref[...], preferred_element_type=F32))
    ssm = _rms(gy * gate, gs_ref[...])
    pitch = nc + 8
    planes = D_SSM // LANES
    for t in range(tc):
        for p in range(planes):
            park_s[p, t * pitch:t * pitch + nc, :] = ssm[t * nc:(t + 1) * nc,
                                                         p * LANES:(p + 1) * LANES]
    ssm = jnp.concatenate(
        [jnp.concatenate([park_s[p, pl.ds(c, tc, stride=pitch), :] for p in range(planes)], axis=1)
         for c in range(nc)], axis=0).astype(BF16)
    att = _rms(a_ref[...], ga_ref[...]).astype(BF16)
    x1 = (x_ref[...]
          + jnp.dot(ssm, wout_ref[0:D_SSM, :], preferred_element_type=F32)
          + jnp.dot(att, wout_ref[D_SSM:D_SSM + D_ATTN, :], preferred_element_type=F32))
    x1_ref[...] = x1
    hm_ref[...] = _rms(x1, gm_ref[...]).astype(BF16)


def _mix(x, y, attn, w_glu, g_ssm, g_attn, w_out, g_mlp, *, tt, tc):
    t = x.shape[0]
    nc = tt // tc
    assert t % tt == 0 and tt % tc == 0 and nc % 8 == 0
    row = lambda i: (i, 0)
    const = lambda i: (0, 0)
    once = pl.Buffered(1)
    return pl.pallas_call(
        functools.partial(_mix_kernel, tc=tc, nc=nc),
        grid=(t // tt,),
        in_specs=[
            pl.BlockSpec((tt, D_MODEL), row),
            pl.BlockSpec((tt, D_SSM), row),
            pl.BlockSpec((tt, D_ATTN), row),
            pl.BlockSpec((D_SSM, D_SSM), const, pipeline_mode=once),
            pl.BlockSpec((1, D_SSM), const),
            pl.BlockSpec((1, D_ATTN), const),
            pl.BlockSpec((D_SSM + D_ATTN, D_MODEL), const, pipeline_mode=once),
            pl.BlockSpec((1, D_MODEL), const),
        ],
        out_specs=[pl.BlockSpec((tt, D_MODEL), row), pl.BlockSpec((tt, D_MODEL), row)],
        out_shape=[jax.ShapeDtypeStruct((t, D_MODEL), F32), jax.ShapeDtypeStruct((t, D_MODEL), BF16)],
        scratch_shapes=[pltpu.VMEM((D_SSM // LANES, tc * (nc + 8), LANES), F32)],
        compiler_params=_params(("arbitrary",)),
        name="mix",
    )(x, y, attn, w_glu, g_ssm, g_attn, w_out, g_mlp)


def _mlp_kernel(x1_ref, hm_ref, wup_ref, wdn_ref, o_ref):
    j = pl.program_id(1)

    @pl.when(j == 0)
    def _():
        o_ref[...] = x1_ref[...]

    a = jnp.maximum(jnp.dot(hm_ref[...], wup_ref[...], preferred_element_type=F32), 0.0)
    o_ref[...] += jnp.dot((a * a).astype(BF16), wdn_ref[...], preferred_element_type=F32)


def _mlp(x1, hm, w_up, w_down, *, tt, tf):
    t = x1.shape[0]
    assert t % tt == 0 and D_FF % tf == 0
    return pl.pallas_call(
        _mlp_kernel,
        grid=(t // tt, D_FF // tf),
        in_specs=[
            pl.BlockSpec((tt, D_MODEL), lambda i, j: (i, 0)),
            pl.BlockSpec((tt, D_MODEL), lambda i, j: (i, 0)),
            pl.BlockSpec((D_MODEL, tf), lambda i, j: (0, j)),
            pl.BlockSpec((tf, D_MODEL), lambda i, j: (j, 0)),
        ],
        out_specs=pl.BlockSpec((tt, D_MODEL), lambda i, j: (i, 0)),
        out_shape=jax.ShapeDtypeStruct((t, D_MODEL), F32),
        compiler_params=_params(("arbitrary", "arbitrary")),
        name="mlp",
    )(x1, hm, w_up, w_down)


def _layer(x, weights, *, n_seq, seq_len, cache=None, h0=None, tiles):
    (g_mix, w_main, w_f, b_f, g_q, g_k, s5_raw, w_glu, g_ssm, g_attn, w_out, g_mlp,
     w_up, w_down) = weights
    t = n_seq * seq_len
    x2 = x.reshape(t, D_MODEL)
    tc = tiles["chunk"] if cache is None else seq_len
    u, qb, k, kb, v, vb, lf_t = _in_proj(x2, g_mix, w_main, w_f, b_f, g_q, g_k, tt=tiles["s5"])
    lf_t = lf_t.reshape(N_HEADS, n_seq, seq_len)
    logf = lf_t.transpose(1, 2, 0)
    lf_rows = lf_t.transpose(1, 0, 2).reshape(n_seq * N_HEADS, seq_len)

    if cache is None:
        tables = _s5_tables(*s5_raw, tc=tc)
        y, hfin = _s5(u, tables, None, tt=tiles["s5"], tc=tc, seq_len=seq_len,
                      carry_mode=True, subs=tiles["s5_subs"])
        c_rows = _cumsum_lanes(lf_rows).reshape(n_seq * N_HEADS, 1, seq_len)
        attend = functools.partial(_attn_prompt, qb, kb, vb, n_seq=n_seq, seq_len=seq_len,
                                   bq=tiles["bq"])
        bound = (HEAD_DIM ** -0.5 * LOG2E) * HEAD_DIM * jnp.max(jnp.abs(g_q)) * jnp.max(jnp.abs(g_k))
        attn = lax.cond(
            bound <= MAX_FIXED_SHIFT,
            lambda: attend(_first_live_pair(c_rows, tiles["bq"]), c_rows - bound, c_rows,
                           online=False),
            lambda: attend(jnp.zeros((c_rows.shape[0] * (seq_len // tiles["bq"]),), jnp.int32),
                           c_rows, c_rows, online=True))
    else:
        cache_k, cache_v, cache_logf = cache
        past = cache_k.shape[1]
        tables = _s5_tables(*s5_raw, tc=tc)
        y, hfin = _s5(u, tables, h0, tt=tiles["s5"], tc=tc, seq_len=seq_len,
                      carry_mode=False, subs=tiles["s5_subs"])
        past_rows = cache_logf.transpose(0, 2, 1).reshape(n_seq * N_HEADS, past)
        total = past + seq_len
        padded = -(-total // LANES) * LANES
        lf_all = jnp.concatenate(
            [past_rows, lf_rows, jnp.zeros((n_seq * N_HEADS, padded - total), F32)], axis=1)
        c_all = _cumsum_lanes(lf_all)
        attn = _attn_sample(qb, kb, vb, cache_k, cache_v, c_all,
                            n_seq=n_seq, s_new=seq_len, past=past)

    x1, hm = _mix(x2, y, attn, w_glu, g_ssm, g_attn, w_out, g_mlp, tt=tiles["s5"], tc=tc)
    out = _mlp(x1, hm, w_up, w_down, tt=tiles["mlp"], tf=tiles["tf"])
    h_re, h_im = _state_to_gp(hfin)
    return (out.reshape(n_seq, seq_len, D_MODEL),
            k.reshape(n_seq, seq_len, N_HEADS, HEAD_DIM),
            v.reshape(n_seq, seq_len, N_HEADS, HEAD_DIM),
            logf, h_re, h_im)


def _cast_kernel(w_ref, wf_ref, o_ref, of_ref):
    o_ref[...] = w_ref[...].astype(o_ref.dtype)
    of_ref[...] = jnp.zeros_like(of_ref)
    of_ref[0:N_HEADS, :] = wf_ref[...].astype(of_ref.dtype)


def _cast_w_in(w_t, n_main, rows_per_step=512):
    n_rows, d = w_t.shape
    assert n_main % rows_per_step == 0 and n_rows - n_main == N_HEADS and n_main % N_HEADS == 0
    return pl.pallas_call(
        _cast_kernel,
        grid=(n_main // rows_per_step,),
        in_specs=[pl.BlockSpec((rows_per_step, d), lambda i: (i, 0)),
                  pl.BlockSpec((N_HEADS, d), lambda i: (n_main // N_HEADS, 0))],
        out_specs=[pl.BlockSpec((rows_per_step, d), lambda i: (i, 0)),
                   pl.BlockSpec((LANES, d), lambda i: (0, 0))],
        out_shape=[jax.ShapeDtypeStruct((n_main, d), BF16),
                   jax.ShapeDtypeStruct((LANES, d), BF16)],
        compiler_params=_params(("arbitrary",)),
        name="cast_w_in",
    )(w_t, w_t)


def _prep_weights(l, g_norm_mix, w_in, b_f, ssm_a_re, ssm_a_im, ssm_log_step, ssm_b_re, ssm_b_im,
                  ssm_c_re, ssm_c_im, ssm_d, w_glu, g_q, g_k, g_out_ssm, g_out_attn, w_out,
                  g_norm_mlp, w_up, w_down):
    n_main = D_SSM + 3 * D_ATTN
    w_main, w_f = _cast_w_in(jnp.swapaxes(w_in[l], 0, 1), n_main)
    b = jnp.pad(b_f[l], (0, LANES - N_HEADS)).reshape(1, LANES)
    s5_raw = (ssm_a_re[l], ssm_a_im[l], ssm_log_step[l], ssm_b_re[l], ssm_b_im[l],
              ssm_c_re[l], ssm_c_im[l], ssm_d[l])
    return (g_norm_mix[l].reshape(1, D_MODEL), w_main, w_f, b,
            g_q[l].reshape(1, HEAD_DIM), g_k[l].reshape(1, HEAD_DIM), s5_raw,
            w_glu[l].astype(BF16), g_out_ssm[l].reshape(1, D_SSM), g_out_attn[l].reshape(1, D_ATTN),
            w_out[l].astype(BF16), g_norm_mlp[l].reshape(1, D_MODEL),
            w_up[l].astype(BF16), w_down[l].astype(BF16))


PROMPT_TILES = dict(s5=512, s5_subs=16, chunk=32, bq=1024, mlp=512, tf=2048)
SAMPLE_TILES = dict(s5=512, s5_subs=1, mlp=512, tf=2048)


def kernel(x_prompt, x_sample, cache_k, cache_v, cache_logf, state_ssm_re, state_ssm_im,
           g_norm_mix, w_in, b_f, ssm_a_re, ssm_a_im, ssm_log_step, ssm_b_re, ssm_b_im,
           ssm_c_re, ssm_c_im, ssm_d, w_glu, g_q, g_k, g_out_ssm, g_out_attn, w_out,
           g_norm_mlp, w_up, w_down):
    depth = w_in.shape[0]
    y_p, y_s = x_prompt, x_sample
    outs_p, outs_s = [], []
    for l in range(depth):
        weights = _prep_weights(l, g_norm_mix, w_in, b_f, ssm_a_re, ssm_a_im, ssm_log_step,
                                ssm_b_re, ssm_b_im, ssm_c_re, ssm_c_im, ssm_d, w_glu, g_q, g_k,
                                g_out_ssm, g_out_attn, w_out, g_norm_mlp, w_up, w_down)
        n_p, l_p = y_p.shape[0], y_p.shape[1]
        y_p, *rest_p = _layer(y_p, weights, n_seq=n_p, seq_len=l_p, tiles=PROMPT_TILES)
        n_s, l_s = y_s.shape[0], y_s.shape[1]
        h0 = _state_from_gp(state_ssm_re[l], state_ssm_im[l])
        y_s, *rest_s = _layer(y_s, weights, n_seq=n_s, seq_len=l_s,
                              cache=(cache_k[l], cache_v[l], cache_logf[l]), h0=h0,
                              tiles=SAMPLE_TILES)
        outs_p.append(rest_p)
        outs_s.append(rest_s)
    stack = lambda outs, idx: jnp.stack([o[idx] for o in outs])
    return (y_p, y_s,
            stack(outs_p, 0), stack(outs_p, 1), stack(outs_p, 2), stack(outs_p, 3), stack(outs_p, 4),
            stack(outs_s, 0), stack(outs_s, 1), stack(outs_s, 2), stack(outs_s, 3), stack(outs_s, 4))
```

```python
import functools
import math

import jax
import jax.numpy as jnp
from jax import lax
from jax.experimental import pallas as pl
from jax.experimental.pallas import tpu as pltpu

D_MODEL = 2048
D_SSM = 1024
SSM_GROUP = 16
N_SSM_GROUPS = 64
SSM_STATE = 64
D_ATTN = 1024
HEAD_DIM = 128
N_HEADS = 8
D_FF = 8192
EPS = 1e-6
NEG_INF = -1e30
LOG2E = math.log2(math.e)
MAX_FIXED_SHIFT = 45.0
EXP2_ZERO = -150.0

LANES = 128
SLAB = 256
N_SLABS = D_SSM // SLAB
SLAB_GROUPS = SLAB // SSM_GROUP
SLAB_STATE = SLAB_GROUPS * SSM_STATE
V7X_VMEM_BYTES = 64 * 1024 * 1024
VMEM_LIMIT = V7X_VMEM_BYTES - 2 * 1024 * 1024

F32 = jnp.float32
BF16 = jnp.bfloat16


def _params(sem, vmem=VMEM_LIMIT):
    return pltpu.CompilerParams(dimension_semantics=sem, vmem_limit_bytes=vmem)


def _rms(x, g):
    return x * lax.rsqrt(jnp.mean(x * x, axis=-1, keepdims=True) + EPS) * g


def _in_proj_kernel(x_ref, g_ref, w_ref, wf_ref, bf_ref, gq_ref, gk_ref,
                    u_ref, qb_ref, k_ref, kb_ref, v_ref, vb_ref, lft_ref):
    hb = _rms(x_ref[...], g_ref[...]).astype(BF16)

    def project(w_rows):
        return lax.dot_general(hb, w_rows, (((1,), (1,)), ((), ())), preferred_element_type=F32)

    u_ref[...] = project(w_ref[0:D_SSM, :])

    q = project(w_ref[D_SSM:D_SSM + D_ATTN, :])
    gq = gq_ref[...]
    for h in range(N_HEADS):
        sl = slice(h * HEAD_DIM, (h + 1) * HEAD_DIM)
        qb_ref[:, sl] = (_rms(q[:, sl], gq) * (HEAD_DIM ** -0.5 * LOG2E)).astype(BF16)

    k = project(w_ref[D_SSM + D_ATTN:D_SSM + 2 * D_ATTN, :])
    gk = gk_ref[...]
    for h in range(N_HEADS):
        sl = slice(h * HEAD_DIM, (h + 1) * HEAD_DIM)
        kn = _rms(k[:, sl], gk)
        k_ref[:, sl] = kn
        kb_ref[:, sl] = kn.astype(BF16)

    v = project(w_ref[D_SSM + 2 * D_ATTN:D_SSM + 3 * D_ATTN, :])
    v_ref[...] = v
    vb_ref[...] = v.astype(BF16)

    zf = lax.dot_general(wf_ref[0:2 * N_HEADS, :], hb, (((1,), (1,)), ((), ())),
                         preferred_element_type=F32)[:N_HEADS, :] + bf_ref[...]
    lft_ref[...] = jnp.minimum(zf, 0.0) - jnp.log1p(jnp.exp(-jnp.abs(zf)))


def _in_proj(x, g, w_main, w_f, b_f, g_q, g_k, *, tt):
    t = x.shape[0]
    assert t % tt == 0
    row = lambda i: (i, 0)
    const = lambda i: (0, 0)
    once = pl.Buffered(1)
    wide = lambda dt: jax.ShapeDtypeStruct((t, D_ATTN), dt)
    return pl.pallas_call(
        _in_proj_kernel,
        grid=(t // tt,),
        in_specs=[
            pl.BlockSpec((tt, D_MODEL), row),
            pl.BlockSpec((1, D_MODEL), const),
            pl.BlockSpec((D_SSM + 3 * D_ATTN, D_MODEL), const, pipeline_mode=once),
            pl.BlockSpec((LANES, D_MODEL), const, pipeline_mode=once),
            pl.BlockSpec((N_HEADS, 1), const),
            pl.BlockSpec((1, HEAD_DIM), const),
            pl.BlockSpec((1, HEAD_DIM), const),
        ],
        out_specs=[
            pl.BlockSpec((tt, D_SSM), row),
            pl.BlockSpec((tt, D_ATTN), row),
            pl.BlockSpec((tt, D_ATTN), row),
            pl.BlockSpec((tt, D_ATTN), row),
            pl.BlockSpec((tt, D_ATTN), row),
            pl.BlockSpec((tt, D_ATTN), row),
            pl.BlockSpec((N_HEADS, tt), lambda i: (0, i)),
        ],
        out_shape=[wide(F32), wide(BF16), wide(F32), wide(BF16), wide(F32), wide(BF16),
                   jax.ShapeDtypeStruct((N_HEADS, t), F32)],
        compiler_params=_params(("arbitrary",)),
        name="in_proj",
    )(x, g, w_main, w_f, b_f, g_q, g_k)


def _cumsum_kernel(x_ref, o_ref):
    rows, length = x_ref.shape
    upper = (lax.broadcasted_iota(jnp.int32, (LANES, LANES), 0)
             <= lax.broadcasted_iota(jnp.int32, (LANES, LANES), 1)).astype(BF16)
    carry = jnp.zeros((rows, 1), F32)
    for b in range(length // LANES):
        sl = slice(b * LANES, (b + 1) * LANES)
        x = x_ref[:, sl]
        hi = x.astype(BF16)
        rest = x - hi.astype(F32)
        mid = rest.astype(BF16)
        lo = (rest - mid.astype(F32)).astype(BF16)
        x = (jnp.dot(hi, upper, preferred_element_type=F32)
             + jnp.dot(mid, upper, preferred_element_type=F32)
             + jnp.dot(lo, upper, preferred_element_type=F32)) + carry
        o_ref[:, sl] = x * LOG2E
        carry = x[:, LANES - 1:LANES]


def _cumsum_lanes(x):
    rows, length = x.shape
    assert length % LANES == 0 and rows % 8 == 0
    return pl.pallas_call(
        _cumsum_kernel,
        out_shape=jax.ShapeDtypeStruct((rows, length), F32),
        name="cumsum",
    )(x)


def _s5_tables(a_re, a_im, log_step, b_re, b_im, c_re, c_im, d, tc):
    step = jnp.exp(log_step)[:, None]
    mag = jnp.exp(a_re * step)
    abar_re = mag * jnp.cos(a_im * step)
    abar_im = mag * jnp.sin(a_im * step)
    den = a_re * a_re + a_im * a_im
    nr = abar_re - 1.0
    ni = abar_im
    fr = (nr * a_re + ni * a_im) / den
    fi = (ni * a_re - nr * a_im) / den
    bbar_re = fr[..., None] * b_re - fi[..., None] * b_im
    bbar_im = fr[..., None] * b_im + fi[..., None] * b_re
    row_group = lambda n, per: (lax.broadcasted_iota(jnp.int32, (n, 1), 0) // per)
    col_group = lambda n, per: (lax.broadcasted_iota(jnp.int32, (1, n), 1) // per)

    def in_blockdiag(b):
        b = b.reshape(N_SLABS, SLAB_GROUPS, SSM_STATE, SSM_GROUP).transpose(0, 1, 3, 2)
        b = jnp.tile(b.reshape(N_SLABS, SLAB, SSM_STATE), (1, 1, SLAB_GROUPS))
        keep = row_group(SLAB, SSM_GROUP) == col_group(SLAB_STATE, SSM_STATE)
        return jnp.where(keep[None], b, 0.0)

    def out_blockdiag(c):
        c = c.reshape(N_SLABS, SLAB_GROUPS, SSM_GROUP, SSM_STATE).transpose(0, 1, 3, 2)
        c = jnp.tile(c.reshape(N_SLABS, SLAB_STATE, SSM_GROUP), (1, 1, SLAB_GROUPS))
        keep = row_group(SLAB_STATE, SSM_STATE) == col_group(SLAB, SSM_GROUP)
        return jnp.where(keep[None], c, 0.0)

    bb = jnp.concatenate([in_blockdiag(bbar_re), in_blockdiag(bbar_im)], axis=-1).astype(BF16)
    cc = jnp.concatenate([out_blockdiag(c_re), out_blockdiag(-c_im)], axis=1).astype(BF16)
    n = jnp.arange(1, tc + 1, dtype=F32)[:, None, None]
    pmag = jnp.exp(a_re[None] * step[None] * n)
    ang = a_im[None] * step[None] * n
    p_re = (pmag * jnp.cos(ang)).reshape(tc, N_SLABS, SLAB_STATE).transpose(1, 0, 2)
    p_im = (pmag * jnp.sin(ang)).reshape(tc, N_SLABS, SLAB_STATE).transpose(1, 0, 2)
    pw = jnp.concatenate([p_re, p_im], axis=-1)
    dd = d.reshape(N_SLABS, 1, SLAB)
    return bb, cc, pw, dd


def _s5_kernel(*refs, subs, **static):
    if subs == 1:
        _s5_tile(0, *refs, subs=subs, **static)
    else:
        def tile(sub, carry):
            _s5_tile(sub, *refs, subs=subs, **static)
            return carry
        lax.fori_loop(0, subs, tile, 0)


def _s5_tile(sub, u_ref, bb_ref, cc_ref, pw_ref, d_ref, h0_ref, y_ref, hfin_ref,
             us_s, st_s, hb_s, g_s, carry_s, *, subs, tc, nc, tiles_per_seq, carry_mode):
    i = pl.program_id(1) * subs + sub
    tile_rows = pl.ds(pl.multiple_of(sub * (tc * nc), tc * nc), tc * nc)
    seq_rows = pl.ds(pl.multiple_of(sub * nc, nc), nc)
    pitch = tc + 8
    for c in range(nc):
        chunk = pl.ds(pl.multiple_of(sub * (tc * nc) + c * tc, tc), tc)
        for half in range(SLAB // LANES):
            us_s[half, c * pitch:c * pitch + tc, :] = u_ref[chunk, half * LANES:(half + 1) * LANES]
    u = jnp.concatenate(
        [jnp.concatenate([us_s[half, pl.ds(t, nc, stride=pitch), :]
                          for half in range(SLAB // LANES)], axis=1) for t in range(tc)], axis=0)
    uh = u.astype(BF16)
    y_ref[tile_rows, :] = d_ref[...] * u
    blk = 2 * LANES
    nblk = SLAB_STATE // blk
    grp = max(1, 16 // nc)
    lanes = lambda b: slice(b * blk, (b + 1) * blk)
    step_rows = lambda t: slice(t * nc, (t + 1) * nc)

    if carry_mode:
        @pl.when(i % tiles_per_seq == 0)
        def _():
            carry_s[...] = jnp.zeros_like(carry_s)
    else:
        g_s[...] = h0_ref[seq_rows, :]

    for b in range(nblk):
        re, im = lanes(b), lanes(nblk + b)
        st_s[:, re] = jnp.dot(uh, bb_ref[:, re], preferred_element_type=F32)
        st_s[:, im] = jnp.dot(uh, bb_ref[:, im], preferred_element_type=F32)
        ar = pw_ref[0:1, re]
        ai = pw_ref[0:1, im]
        hr = jnp.zeros((nc, blk), F32)
        hi = jnp.zeros((nc, blk), F32)
        for t in range(tc):
            rows = step_rows(t)
            hr, hi = (ar * hr - ai * hi + st_s[rows, re],
                      ar * hi + ai * hr + st_s[rows, im])
            st_s[rows, re] = hr
            st_s[rows, im] = hi

        ar = pw_ref[tc - 1:tc, re]
        ai = pw_ref[tc - 1:tc, im]
        if carry_mode:
            gr = carry_s[0:1, re]
            gi = carry_s[0:1, im]
            for c in range(nc):
                g_s[c:c + 1, re] = gr
                g_s[c:c + 1, im] = gi
                gr, gi = (ar * gr - ai * gi + hr[c:c + 1, :],
                          ar * gi + ai * gr + hi[c:c + 1, :])
            carry_s[0:1, re] = gr
            carry_s[0:1, im] = gi
            hfin_ref[0:1, re] = gr
            hfin_ref[0:1, im] = gi
            gr = g_s[:, re]
            gi = g_s[:, im]
        else:
            gr = g_s[:, re]
            gi = g_s[:, im]
            hfin_ref[seq_rows, re] = ar * gr - ai * gi + hr
            hfin_ref[seq_rows, im] = ar * gi + ai * gr + hi

        for t0 in range(0, tc, grp):
            full_r, full_i = [], []
            for t in range(t0, t0 + grp):
                rows = step_rows(t)
                pr = pw_ref[t:t + 1, re]
                pi = pw_ref[t:t + 1, im]
                full_r.append(st_s[rows, re] + (pr * gr - pi * gi))
                full_i.append(st_s[rows, im] + (pr * gi + pi * gr))
            rows = slice(t0 * nc, (t0 + grp) * nc)
            hb_s[rows, re] = jnp.concatenate(full_r, axis=0).astype(BF16)
            hb_s[rows, im] = jnp.concatenate(full_i, axis=0).astype(BF16)

        y_ref[tile_rows, :] += (jnp.dot(hb_s[:, re], cc_ref[re, :], preferred_element_type=F32)
                                + jnp.dot(hb_s[:, im], cc_ref[im, :], preferred_element_type=F32))


def _s5(u, tables, h0, *, tt, tc, seq_len, carry_mode, subs=1):
    bb, cc, pw, dd = tables
    t = u.shape[0]
    nc = tt // tc
    assert t % tt == 0 and tt % tc == 0
    subs = math.gcd(subs, seq_len // tt if carry_mode else t // tt)
    step_rows = tt * subs
    n_steps = t // step_rows
    if carry_mode:
        tiles_per_seq = seq_len // tt
        assert seq_len % tt == 0 and tiles_per_seq % subs == 0
        n_seq = t // seq_len
        hfin_shape = (N_SLABS, n_seq, 1, 2 * SLAB_STATE)
        hfin_spec = pl.BlockSpec((None, None, 1, 2 * SLAB_STATE),
                                 lambda s, i: (s, (i * subs) // tiles_per_seq, 0, 0))
        h0 = jnp.zeros((N_SLABS, 8, 2 * SLAB_STATE), F32)
        h0_spec = pl.BlockSpec((None, 8, 2 * SLAB_STATE), lambda s, i: (s, 0, 0))
    else:
        assert seq_len == tc
        tiles_per_seq = 1
        hfin_shape = (N_SLABS, n_steps, subs * nc, 2 * SLAB_STATE)
        hfin_spec = pl.BlockSpec((None, None, subs * nc, 2 * SLAB_STATE),
                                 lambda s, i: (s, i, 0, 0))
        h0_spec = pl.BlockSpec((None, subs * nc, 2 * SLAB_STATE), lambda s, i: (s, i, 0))
    kern = functools.partial(_s5_kernel, subs=subs, tc=tc, nc=nc, tiles_per_seq=tiles_per_seq,
                             carry_mode=carry_mode)
    y, hfin = pl.pallas_call(
        kern,
        grid=(N_SLABS, n_steps),
        in_specs=[
            pl.BlockSpec((step_rows, SLAB), lambda s, i: (i, s)),
            pl.BlockSpec((None, SLAB, 2 * SLAB_STATE), lambda s, i: (s, 0, 0)),
            pl.BlockSpec((None, 2 * SLAB_STATE, SLAB), lambda s, i: (s, 0, 0)),
            pl.BlockSpec((None, tc, 2 * SLAB_STATE), lambda s, i: (s, 0, 0)),
            pl.BlockSpec((None, 1, SLAB), lambda s, i: (s, 0, 0)),
            h0_spec,
        ],
        out_specs=[pl.BlockSpec((step_rows, SLAB), lambda s, i: (i, s)), hfin_spec],
        out_shape=[jax.ShapeDtypeStruct((t, D_SSM), F32),
                   jax.ShapeDtypeStruct(hfin_shape, F32)],
        scratch_shapes=[pltpu.VMEM((SLAB // LANES, nc * (tc + 8), LANES), F32),
                        pltpu.VMEM((tt, 2 * SLAB_STATE), F32),
                        pltpu.VMEM((tt, 2 * SLAB_STATE), BF16),
                        pltpu.VMEM((nc, 2 * SLAB_STATE), F32),
                        pltpu.VMEM((8, 2 * SLAB_STATE), F32)],
        compiler_params=_params(("arbitrary", "arbitrary")),
        name="s5",
    )(u, bb, cc, pw, dd, h0)
    return y, hfin.reshape(N_SLABS, -1, 2 * SLAB_STATE)


def _state_to_gp(hfin):
    n = hfin.shape[1]
    h = hfin.transpose(1, 0, 2)
    re = h[..., :SLAB_STATE].reshape(n, N_SSM_GROUPS, SSM_STATE)
    im = h[..., SLAB_STATE:].reshape(n, N_SSM_GROUPS, SSM_STATE)
    return re, im


def _state_from_gp(re, im):
    n = re.shape[0]
    h = jnp.concatenate([re.reshape(n, N_SLABS, SLAB_STATE), im.reshape(n, N_SLABS, SLAB_STATE)],
                        axis=-1)
    return h.transpose(1, 0, 2)


def _col_from_row(row):
    n = row.shape[1]
    r = lax.broadcasted_iota(jnp.int32, (n, n), 0)
    c = lax.broadcasted_iota(jnp.int32, (n, n), 1)
    return jnp.sum(jnp.where(r == c, jnp.broadcast_to(row, (n, n)), 0.0), axis=1, keepdims=True)


def _attn_prompt_kernel(first_ref, q_ref, k_ref, v_ref, cq_ref, ck_ref, o_ref,
                        m_s, l_s, acc_s, cq_s, s_s, *, bq, online):
    nq = q_ref.shape[0] // bq

    def query_block(i, carry):
        _attn_query_block(i, nq, first_ref, q_ref, k_ref, v_ref, cq_ref, ck_ref, o_ref,
                          m_s, l_s, acc_s, cq_s, s_s, bq=bq, online=online)
        return carry

    lax.fori_loop(0, nq, query_block, 0)


def _attn_query_block(i, nq, first_ref, q_ref, k_ref, v_ref, cq_ref, ck_ref, o_ref,
                      m_s, l_s, acc_s, cq_s, s_s, *, bq, online):
    bk = bq // 2
    row0 = pl.multiple_of(i * bq, bq)
    cq_s[...] = jnp.transpose(jnp.broadcast_to(cq_ref[:, pl.ds(row0, bq)], (LANES, bq)))
    if online:
        m_s[...] = jnp.full_like(m_s, NEG_INF)
    l_s[...] = jnp.zeros_like(l_s)
    acc_s[...] = jnp.zeros_like(acc_s)
    nchunk = bk // LANES
    all_rows = slice(0, bq)

    def scores(j, slot, rows=all_rows):
        start = pl.multiple_of(j * bk, bk)
        q_rows = pl.ds(pl.multiple_of(row0 + rows.start, LANES), rows.stop - rows.start)
        s_s[slot, rows, :] = lax.dot_general(q_ref[q_rows, :], k_ref[pl.ds(start, bk), :],
                                             (((1,), (1,)), ((), ())), preferred_element_type=F32)

    def reduce_block(j, slot, rows=all_rows, causal=False):
        start = pl.multiple_of(j * bk, bk)
        v = v_ref[pl.ds(start, bk), :]
        nrows = rows.stop - rows.start

        def logits(c):
            col0 = pl.multiple_of(start + c * LANES, LANES)
            t = s_s[slot, rows, c * LANES:(c + 1) * LANES] - ck_ref[:, pl.ds(col0, LANES)]
            if causal:
                r = lax.broadcasted_iota(jnp.int32, (nrows, LANES), 0)
                col = lax.broadcasted_iota(jnp.int32, (nrows, LANES), 1) + c * LANES
                t = jnp.where(col <= r, t, NEG_INF)
            return t

        if online:
            m_loc = functools.reduce(jnp.maximum, [logits(c) for c in range(nchunk)])
            m_old = m_s[rows, :]
            m_new = jnp.maximum(m_old, jnp.max(m_loc, axis=1, keepdims=True) + cq_s[rows, :])
            m_s[rows, :] = m_new
            alpha = jnp.exp2(m_old - m_new)
            shift = cq_s[rows, :] - m_new
        else:
            shift = cq_s[rows, :]
        ps = [jnp.exp2(logits(c) + shift) for c in range(nchunk)]
        p = jnp.concatenate([pc.astype(BF16) for pc in ps], axis=1)
        l_new = functools.reduce(jnp.add, ps)
        acc_new = jnp.dot(p, v, preferred_element_type=F32)
        if online:
            l_s[rows, :] = alpha * l_s[rows, :] + l_new
            acc_s[rows, :] = alpha * acc_s[rows, :] + acc_new
        else:
            l_s[rows, :] += l_new
            acc_s[rows, :] += acc_new

    first = first_ref[pl.program_id(0) * nq + i]
    scores(2 * first, 0)

    def body(p, carry):
        j = 2 * p
        scores(j + 1, 1)
        reduce_block(j, 0)
        scores(j + 2, 0)
        reduce_block(j + 1, 1)
        return carry

    lax.fori_loop(first, i, body, 0)
    late_rows = slice(bk, bq)
    scores(2 * i + 1, 1, late_rows)
    reduce_block(2 * i, 0, causal=True)
    reduce_block(2 * i + 1, 1, late_rows, causal=True)

    o_ref[pl.ds(row0, bq), :] = acc_s[...] / jnp.sum(l_s[...], axis=1, keepdims=True)


def _attn_prompt(qb, kb, vb, first, cq_rows, ck_rows, *, n_seq, seq_len, bq, online):
    assert seq_len % bq == 0
    kern = functools.partial(_attn_prompt_kernel, bq=bq, online=online)
    seq_map = lambda g, first: (g // N_HEADS, g % N_HEADS)
    c_map = lambda g, first: (g, 0, 0)
    grid_spec = pltpu.PrefetchScalarGridSpec(
        num_scalar_prefetch=1,
        grid=(n_seq * N_HEADS,),
        in_specs=[
            pl.BlockSpec((seq_len, HEAD_DIM), seq_map),
            pl.BlockSpec((seq_len, HEAD_DIM), seq_map),
            pl.BlockSpec((seq_len, HEAD_DIM), seq_map),
            pl.BlockSpec((None, 1, seq_len), c_map),
            pl.BlockSpec((None, 1, seq_len), c_map),
        ],
        out_specs=pl.BlockSpec((seq_len, HEAD_DIM), seq_map),
        scratch_shapes=[pltpu.VMEM((bq, LANES), F32), pltpu.VMEM((bq, LANES), F32),
                        pltpu.VMEM((bq, HEAD_DIM), F32), pltpu.VMEM((bq, LANES), F32),
                        pltpu.VMEM((2, bq, bq // 2), F32)],
    )
    return pl.pallas_call(
        kern,
        grid_spec=grid_spec,
        out_shape=jax.ShapeDtypeStruct((n_seq * seq_len, D_ATTN), F32),
        compiler_params=_params(("arbitrary",)),
        name="attn_prompt",
    )(first, qb, kb, vb, cq_rows, ck_rows)


def _first_live_pair(c_rows, bq):
    g, _, length = c_rows.shape
    nq = length // bq
    blocks = c_rows.reshape(g, nq, bq)
    hi = jnp.max(blocks, axis=-1)
    lo = jnp.min(blocks, axis=-1)
    dead = (hi[:, :, None] - lo[:, None, :]) < EXP2_ZERO - 1.0
    pair = jnp.arange(nq, dtype=jnp.int32)
    dead = dead & (pair[None, None, :] < pair[None, :, None])
    first = jnp.min(jnp.where(dead, nq, pair[None, None, :]), axis=-1)
    return first.reshape(g * nq).astype(jnp.int32)


def _attn_sample_kernel(q_ref, kn_ref, vn_ref, ck_ref, cv_ref, c_ref, o_ref, *, past, s_new):
    r = lax.broadcasted_iota(jnp.int32, (s_new, s_new), 0)
    col = lax.broadcasted_iota(jnp.int32, (s_new, s_new), 1)
    causal = col <= r
    nt = (((1,), (1,)), ((), ()))
    heads = range(N_HEADS)
    lanes = lambda h: slice(h * HEAD_DIM, (h + 1) * HEAD_DIM)
    head_rows = lambda h: pl.ds(h, past, stride=N_HEADS)
    logits = []
    for h in heads:
        q = q_ref[:, lanes(h)]
        c_past = c_ref[h:h + 1, 0:past]
        c_new = c_ref[h:h + 1, past:past + s_new]
        cq = _col_from_row(c_new)
        s_p = lax.dot_general(q, ck_ref[head_rows(h), :].astype(BF16), nt,
                              preferred_element_type=F32)
        s_n = lax.dot_general(q, kn_ref[:, lanes(h)], nt, preferred_element_type=F32)
        logits.append((s_p + cq - c_past, jnp.where(causal, s_n + cq - c_new, NEG_INF)))
    probs = []
    for s_p, s_n in logits:
        m = jnp.maximum(jnp.max(s_p, axis=1, keepdims=True), jnp.max(s_n, axis=1, keepdims=True))
        p_p = jnp.exp2(s_p - m)
        p_n = jnp.exp2(s_n - m)
        l = jnp.sum(p_p, axis=1, keepdims=True) + jnp.sum(p_n, axis=1, keepdims=True)
        probs.append((p_p.astype(BF16), p_n.astype(BF16), l))
    for h, (p_p, p_n, l) in zip(heads, probs):
        acc = (jnp.dot(p_p, cv_ref[head_rows(h), :].astype(BF16), preferred_element_type=F32)
               + jnp.dot(p_n, vn_ref[:, lanes(h)], preferred_element_type=F32))
        o_ref[:, lanes(h)] = acc / l


def _attn_sample(qb, kb, vb, cache_k, cache_v, c_all, *, n_seq, s_new, past):
    kern = functools.partial(_attn_sample_kernel, past=past, s_new=s_new)
    new = pl.BlockSpec((s_new, D_ATTN), lambda b: (b, 0))
    cache_k = cache_k.reshape(n_seq, past * N_HEADS, HEAD_DIM)
    cache_v = cache_v.reshape(n_seq, past * N_HEADS, HEAD_DIM)
    cache = pl.BlockSpec((None, past * N_HEADS, HEAD_DIM), lambda b: (b, 0, 0))
    return pl.pallas_call(
        kern,
        grid=(n_seq,),
        in_specs=[new, new, new, cache, cache,
                  pl.BlockSpec((N_HEADS, c_all.shape[1]), lambda b: (b, 0))],
        out_specs=new,
        out_shape=jax.ShapeDtypeStruct((n_seq * s_new, D_ATTN), F32),
        compiler_params=_params(("arbitrary",)),
        name="attn_sample",
    )(qb, kb, vb, cache_k, cache_v, c_all)


def _gelu_tanh(x):
    return 0.5 * x * (1.0 + jnp.tanh(math.sqrt(2.0 / math.pi) * (x + 0.044715 * (x * x * x))))


def _mix_kernel(x_ref, y_ref, a_ref, wglu_ref, gs_ref, ga_ref, wout_ref, gm_ref,
                x1_ref, hm_ref, park_s, *, tc, nc):
    gy = _gelu_tanh(y_ref[...])
    gate = jax.nn.sigmoid(jnp.dot(gy.astype(BF16), wglu_ref[...], preferred_element_type=F32))
    ssm = _rms(gy * gate, gs_ref[...])
    pitch = nc + 8
    planes = D_SSM // LANES
    for t in range(tc):
        for p in range(planes):
            park_s[p, t * pitch:t * pitch + nc, :] = ssm[t * nc:(t + 1) * nc,
                                                         p * LANES:(p + 1) * LANES]
    ssm = jnp.concatenate(
        [jnp.concatenate([park_s[p, pl.ds(c, tc, stride=pitch), :] for p in range(planes)], axis=1)
         for c in range(nc)], axis=0).astype(BF16)
    att = _rms(a_ref[...], ga_ref[...]).astype(BF16)
    x1 = (x_ref[...]
          + jnp.dot(ssm, wout_ref[0:D_SSM, :], preferred_element_type=F32)
          + jnp.dot(att, wout_ref[D_SSM:D_SSM + D_ATTN, :], preferred_element_type=F32))
    x1_ref[...] = x1
    hm_ref[...] = _rms(x1, gm_ref[...]).astype(BF16)


def _mix(x, y, attn, w_glu, g_ssm, g_attn, w_out, g_mlp, *, tt, tc):
    t = x.shape[0]
    nc = tt // tc
    assert t % tt == 0 and tt % tc == 0 and nc % 8 == 0
    row = lambda i: (i, 0)
    const = lambda i: (0, 0)
    once = pl.Buffered(1)
    return pl.pallas_call(
        functools.partial(_mix_kernel, tc=tc, nc=nc),
        grid=(t // tt,),
        in_specs=[
            pl.BlockSpec((tt, D_MODEL), row),
            pl.BlockSpec((tt, D_SSM), row),
            pl.BlockSpec((tt, D_ATTN), row),
            pl.BlockSpec((D_SSM, D_SSM), const, pipeline_mode=once),
            pl.BlockSpec((1, D_SSM), const),
            pl.BlockSpec((1, D_ATTN), const),
            pl.BlockSpec((D_SSM + D_ATTN, D_MODEL), const, pipeline_mode=once),
            pl.BlockSpec((1, D_MODEL), const),
        ],
        out_specs=[pl.BlockSpec((tt, D_MODEL), row), pl.BlockSpec((tt, D_MODEL), row)],
        out_shape=[jax.ShapeDtypeStruct((t, D_MODEL), F32), jax.ShapeDtypeStruct((t, D_MODEL), BF16)],
        scratch_shapes=[pltpu.VMEM((D_SSM // LANES, tc * (nc + 8), LANES), F32)],
        compiler_params=_params(("arbitrary",)),
        name="mix",
    )(x, y, attn, w_glu, g_ssm, g_attn, w_out, g_mlp)


def _mlp_kernel(x1_ref, hm_ref, wup_ref, wdn_ref, o_ref):
    j = pl.program_id(1)

    @pl.when(j == 0)
    def _():
        o_ref[...] = x1_ref[...]

    a = jnp.maximum(jnp.dot(hm_ref[...], wup_ref[...], preferred_element_type=F32), 0.0)
    o_ref[...] += jnp.dot((a * a).astype(BF16), wdn_ref[...], preferred_element_type=F32)


def _mlp(x1, hm, w_up, w_down, *, tt, tf):
    t = x1.shape[0]
    assert t % tt == 0 and D_FF % tf == 0
    return pl.pallas_call(
        _mlp_kernel,
        grid=(t // tt, D_FF // tf),
        in_specs=[
            pl.BlockSpec((tt, D_MODEL), lambda i, j: (i, 0)),
            pl.BlockSpec((tt, D_MODEL), lambda i, j: (i, 0)),
            pl.BlockSpec((D_MODEL, tf), lambda i, j: (0, j)),
            pl.BlockSpec((tf, D_MODEL), lambda i, j: (j, 0)),
        ],
        out_specs=pl.BlockSpec((tt, D_MODEL), lambda i, j: (i, 0)),
        out_shape=jax.ShapeDtypeStruct((t, D_MODEL), F32),
        compiler_params=_params(("arbitrary", "arbitrary")),
        name="mlp",
    )(x1, hm, w_up, w_down)


def _layer(x, weights, *, n_seq, seq_len, cache=None, h0=None, tiles):
    (g_mix, w_main, w_f, b_f, g_q, g_k, s5_raw, w_glu, g_ssm, g_attn, w_out, g_mlp,
     w_up, w_down) = weights
    t = n_seq * seq_len
    x2 = x.reshape(t, D_MODEL)
    tc = tiles["chunk"] if cache is None else seq_len
    u, qb, k, kb, v, vb, lf_t = _in_proj(x2, g_mix, w_main, w_f, b_f, g_q, g_k, tt=tiles["s5"])
    lf_t = lf_t.reshape(N_HEADS, n_seq, seq_len)
    logf = lf_t.transpose(1, 2, 0)
    lf_rows = lf_t.transpose(1, 0, 2).reshape(n_seq * N_HEADS, seq_len)

    if cache is None:
        tables = _s5_tables(*s5_raw, tc=tc)
        y, hfin = _s5(u, tables, None, tt=tiles["s5"], tc=tc, seq_len=seq_len,
                      carry_mode=True, subs=tiles["s5_subs"])
        c_rows = _cumsum_lanes(lf_rows).reshape(n_seq * N_HEADS, 1, seq_len)
        attend = functools.partial(_attn_prompt, qb, kb, vb, n_seq=n_seq, seq_len=seq_len,
                                   bq=tiles["bq"])
        bound = (HEAD_DIM ** -0.5 * LOG2E) * HEAD_DIM * jnp.max(jnp.abs(g_q)) * jnp.max(jnp.abs(g_k))
        attn = lax.cond(
            bound <= MAX_FIXED_SHIFT,
            lambda: attend(_first_live_pair(c_rows, tiles["bq"]), c_rows - bound, c_rows,
                           online=False),
            lambda: attend(jnp.zeros((c_rows.shape[0] * (seq_len // tiles["bq"]),), jnp.int32),
                           c_rows, c_rows, online=True))
    else:
        cache_k, cache_v, cache_logf = cache
        past = cache_k.shape[1]
        tables = _s5_tables(*s5_raw, tc=tc)
        y, hfin = _s5(u, tables, h0, tt=tiles["s5"], tc=tc, seq_len=seq_len,
                      carry_mode=False, subs=tiles["s5_subs"])
        past_rows = cache_logf.transpose(0, 2, 1).reshape(n_seq * N_HEADS, past)
        total = past + seq_len
        padded = -(-total // LANES) * LANES
        lf_all = jnp.concatenate(
            [past_rows, lf_rows, jnp.zeros((n_seq * N_HEADS, padded - total), F32)], axis=1)
        c_all = _cumsum_lanes(lf_all)
        attn = _attn_sample(qb, kb, vb, cache_k, cache_v, c_all,
                            n_seq=n_seq, s_new=seq_len, past=past)

    x1, hm = _mix(x2, y, attn, w_glu, g_ssm, g_attn, w_out, g_mlp, tt=tiles["s5"], tc=tc)
    out = _mlp(x1, hm, w_up, w_down, tt=tiles["mlp"], tf=tiles["tf"])
    h_re, h_im = _state_to_gp(hfin)
    return (out.reshape(n_seq, seq_len, D_MODEL),
            k.reshape(n_seq, seq_len, N_HEADS, HEAD_DIM),
            v.reshape(n_seq, seq_len, N_HEADS, HEAD_DIM),
            logf, h_re, h_im)


def _cast_kernel(w_ref, wf_ref, o_ref, of_ref):
    o_ref[...] = w_ref[...].astype(o_ref.dtype)
    of_ref[...] = jnp.zeros_like(of_ref)
    of_ref[0:N_HEADS, :] = wf_ref[...].astype(of_ref.dtype)


def _cast_w_in(w_t, n_main, rows_per_step=512):
    n_rows, d = w_t.shape
    assert n_main % rows_per_step == 0 and n_rows - n_main == N_HEADS and n_main % N_HEADS == 0
    return pl.pallas_call(
        _cast_kernel,
        grid=(n_main // rows_per_step,),
        in_specs=[pl.BlockSpec((rows_per_step, d), lambda i: (i, 0)),
                  pl.BlockSpec((N_HEADS, d), lambda i: (n_main // N_HEADS, 0))],
        out_specs=[pl.BlockSpec((rows_per_step, d), lambda i: (i, 0)),
                   pl.BlockSpec((LANES, d), lambda i: (0, 0))],
        out_shape=[jax.ShapeDtypeStruct((n_main, d), BF16),
                   jax.ShapeDtypeStruct((LANES, d), BF16)],
        compiler_params=_params(("arbitrary",)),
        name="cast_w_in",
    )(w_t, w_t)


def _prep_weights(l, g_norm_mix, w_in, b_f, ssm_a_re, ssm_a_im, ssm_log_step, ssm_b_re, ssm_b_im,
                  ssm_c_re, ssm_c_im, ssm_d, w_glu, g_q, g_k, g_out_ssm, g_out_attn, w_out,
                  g_norm_mlp, w_up, w_down):
    n_main = D_SSM + 3 * D_ATTN
    w_main, w_f = _cast_w_in(jnp.swapaxes(w_in[l], 0, 1), n_main)
    b = b_f[l].reshape(N_HEADS, 1)
    s5_raw = (ssm_a_re[l], ssm_a_im[l], ssm_log_step[l], ssm_b_re[l], ssm_b_im[l],
              ssm_c_re[l], ssm_c_im[l], ssm_d[l])
    return (g_norm_mix[l].reshape(1, D_MODEL), w_main, w_f, b,
            g_q[l].reshape(1, HEAD_DIM), g_k[l].reshape(1, HEAD_DIM), s5_raw,
            w_glu[l].astype(BF16), g_out_ssm[l].reshape(1, D_SSM), g_out_attn[l].reshape(1, D_ATTN),
            w_out[l].astype(BF16), g_norm_mlp[l].reshape(1, D_MODEL),
            w_up[l].astype(BF16), w_down[l].astype(BF16))


PROMPT_TILES = dict(s5=512, s5_subs=4, chunk=32, bq=1024, mlp=512, tf=2048)
SAMPLE_TILES = dict(s5=512, s5_subs=1, mlp=512, tf=2048)


def kernel(x_prompt, x_sample, cache_k, cache_v, cache_logf, state_ssm_re, state_ssm_im,
           g_norm_mix, w_in, b_f, ssm_a_re, ssm_a_im, ssm_log_step, ssm_b_re, ssm_b_im,
           ssm_c_re, ssm_c_im, ssm_d, w_glu, g_q, g_k, g_out_ssm, g_out_attn, w_out,
           g_norm_mlp, w_up, w_down):
    depth = w_in.shape[0]
    y_p, y_s = x_prompt, x_sample
    outs_p, outs_s = [], []
    for l in range(depth):
        weights = _prep_weights(l, g_norm_mix, w_in, b_f, ssm_a_re, ssm_a_im, ssm_log_step,
                                ssm_b_re, ssm_b_im, ssm_c_re, ssm_c_im, ssm_d, w_glu, g_q, g_k,
                                g_out_ssm, g_out_attn, w_out, g_norm_mlp, w_up, w_down)
        n_p, l_p = y_p.shape[0], y_p.shape[1]
        y_p, *rest_p = _layer(y_p, weights, n_seq=n_p, seq_len=l_p, tiles=PROMPT_TILES)
        n_s, l_s = y_s.shape[0], y_s.shape[1]
        h0 = _state_from_gp(state_ssm_re[l], state_ssm_im[l])
        y_s, *rest_s = _layer(y_s, weights, n_seq=n_s, seq_len=l_s,
                              cache=(cache_k[l], cache_v[l], cache_logf[l]), h0=h0,
                              tiles=SAMPLE_TILES)
        outs_p.append(rest_p)
        outs_s.append(rest_s)
    stack = lambda outs, idx: jnp.stack([o[idx] for o in outs])
    return (y_p, y_s,
            stack(outs_p, 0), stack(outs_p, 1), stack(outs_p, 2), stack(outs_p, 3), stack(outs_p, 4),
            stack(outs_s, 0), stack(outs_s, 1), stack(outs_s, 2), stack(outs_s, 3), stack(outs_s, 4))
```
